```python
import jax, jax.numpy as jnp
from jax import lax
import numpy as np

D_MODEL = 1024
BATCH = 4
SEQ = 4096
DEPTH = 1

D_MIX = D_MODEL
RW_HEAD_DIM = 64
RW_WIDTH = D_MIX // 2
RW_HEADS = RW_WIDTH // RW_HEAD_DIM
HG_DK = 128
HG_WIDTH = D_MIX - RW_WIDTH
HG_HEADS = HG_WIDTH // HG_DK
HG_DV = HG_WIDTH // HG_HEADS
HG_KEY = HG_HEADS * HG_DK
DECAY_LORA = 64
ICL_LORA = 64
N_DIR = 2
HG_CHUNK = 64
NORM_EPS = 1e-6
GN_EPS = 64e-5
RW_SIZES = (RW_WIDTH, RW_WIDTH, RW_WIDTH, RW_WIDTH, DECAY_LORA, DECAY_LORA, ICL_LORA, ICL_LORA)
HG_SIZES = (HG_KEY, HG_KEY, HG_KEY, HG_WIDTH, HG_WIDTH)
RW_COLS = 4 * RW_WIDTH + N_DIR * (DECAY_LORA + ICL_LORA)
HG_COLS = 3 * HG_KEY + 2 * HG_WIDTH
IN_COLS = RW_COLS + HG_COLS

kernel_name = "hybrid_rwkv7_hgrn2_bidir_layer"


def rms_norm(x, g):
    xf = x.astype(jnp.float32)
    y = xf * lax.rsqrt(jnp.mean(xf * xf, axis=-1, keepdims=True) + NORM_EPS)
    return (y * g.astype(jnp.float32)).astype(x.dtype)


def split_cols(z, sizes):
    offs = np.cumsum(np.array(sizes))[:-1].tolist()
    return jnp.split(z, offs, axis=-1)


def centred_token_shift(p, mu_prev, mu_next):
    zeros = jnp.zeros_like(p[:, :1])
    p_prev = jnp.concatenate([zeros, p[:, :-1]], axis=1)
    p_next = jnp.concatenate([p[:, 1:], zeros], axis=1)
    return p + mu_prev * (p_prev - p) + mu_next * (p_next - p)


def orient(t2):
    return jnp.stack([t2[0], jnp.flip(t2[1], axis=1)], axis=0)


def merge_dirs(y2):
    return y2[0] + jnp.flip(y2[1], axis=1)


def rwkv7_scan(r, w, k, v, kk, b):
    def step(S, inp):
        r_t, w_t, k_t, v_t, kk_t, b_t = inp
        S = (S * w_t[..., None, :]
             - jnp.einsum('dbhij,dbhj->dbhi', S, kk_t)[..., None] * b_t[..., None, :]
             + v_t[..., :, None] * k_t[..., None, :])
        return S, jnp.einsum('dbhij,dbhj->dbhi', S, r_t)
    xs = [jnp.moveaxis(a, 2, 0) for a in (r, w, k, v, kk, b)]
    n_dir, bsz, _, heads, n = r.shape
    S0 = jnp.zeros((n_dir, bsz, heads, n, n), jnp.float32)
    _, y = lax.scan(step, S0, xs)
    return jnp.moveaxis(y, 0, 2)


def rwkv7_mixer(z, w0, w2, a0, a2, k_k, k_a, r_k, ln_w, ln_b):
    bsz, seq, _ = z.shape
    zf = z.astype(jnp.float32)
    r, k, v, g, wd_f, wd_b, ad_f, ad_b = split_cols(zf, RW_SIZES)
    heads = lambda t: t.reshape(t.shape[:-1] + (RW_HEADS, RW_HEAD_DIM))
    w_raw = w0[:, None, None, :] + jnp.einsum('dbtr,drc->dbtc', jnp.tanh(jnp.stack([wd_f, wd_b])), w2)
    w_log = -jax.nn.softplus(-w_raw) - 0.5
    decay = jnp.exp(-jnp.exp(w_log))
    a = jax.nn.sigmoid(a0[:, None, None, :] + jnp.einsum('dbtr,drc->dbtc', jnp.stack([ad_f, ad_b]), a2))
    kk = heads(k * k_k)
    kk = kk / jnp.maximum(jnp.linalg.norm(kk, axis=-1, keepdims=True), 1e-12)
    k_dir = heads(k[None] * (1.0 + (a - 1.0) * k_a))
    a_h = heads(a)
    r_h, v_h = heads(r), heads(v)
    y2 = rwkv7_scan(orient(jnp.stack([r_h, r_h])), orient(heads(decay)), orient(k_dir),
                    orient(jnp.stack([v_h, v_h])), orient(jnp.stack([kk, kk])),
                    orient(kk[None] * a_h))
    y = merge_dirs(y2)
    mu = jnp.mean(y, axis=-1, keepdims=True)
    var = jnp.mean(jnp.square(y - mu), axis=-1, keepdims=True)
    y = (y - mu) * lax.rsqrt(var + GN_EPS) * heads(ln_w) + heads(ln_b)
    bonus = jnp.sum(r_h[None] * k_dir * r_k, axis=(0, -1))[..., None] * v_h
    out = (y + bonus).reshape(bsz, seq, RW_WIDTH) * jax.nn.silu(g)
    return out.astype(z.dtype)


def hgrn2_chunked_scan(q, k, v, log_f):
    n_dir, bsz, seq, heads, dk = q.shape
    dv = v.shape[-1]
    n_chunks = seq // HG_CHUNK
    to_chunks = lambda t: t.reshape(n_dir, bsz, n_chunks, HG_CHUNK, heads, t.shape[-1]).transpose(2, 0, 1, 4, 3, 5)
    causal = jnp.tril(jnp.ones((HG_CHUNK, HG_CHUNK), bool))[:, :, None]

    def step(S, inp):
        q_c, k_c, v_c, g_c = inp
        b = jnp.cumsum(g_c, axis=-2)
        o_inter = jnp.einsum('dbhck,dbhkv->dbhcv', q_c * jnp.exp(b), S)
        diff = b[..., :, None, :] - b[..., None, :, :]
        dec = jnp.where(causal, jnp.exp(jnp.where(causal, diff, 0.0)), 0.0)
        scores = jnp.einsum('dbhtk,dbhtsk,dbhsk->dbhts', q_c, dec, k_c)
        o_intra = jnp.einsum('dbhts,dbhsv->dbhtv', scores, v_c)
        b_end = b[..., -1:, :]
        S = (jnp.exp(b_end[..., 0, :])[..., None] * S
             + jnp.einsum('dbhck,dbhcv->dbhkv', k_c * jnp.exp(b_end - b), v_c))
        return S, o_inter + o_intra

    S0 = jnp.zeros((n_dir, bsz, heads, dk, dv), jnp.float32)
    _, o = lax.scan(step, S0, [to_chunks(t) for t in (q, k, v, log_f)])
    return o.transpose(1, 2, 0, 4, 3, 5).reshape(n_dir, bsz, seq, heads, dv)


def hgrn2_mixer(z, lower_bound, norm_g):
    bsz, seq, _ = z.shape
    zf = z.astype(jnp.float32)
    q, f_f, f_b, i, g = split_cols(zf, HG_SIZES)
    kheads = lambda t: t.reshape(t.shape[:-1] + (HG_HEADS, HG_DK))
    vheads = lambda t: t.reshape(t.shape[:-1] + (HG_HEADS, HG_DV))
    f = lower_bound + (1.0 - lower_bound) * jax.nn.sigmoid(jnp.stack([f_f, f_b]))
    q_h, i_h = kheads(q), vheads(i)
    o2 = hgrn2_chunked_scan(orient(jnp.stack([q_h, q_h])), orient(kheads(1.0 - f)),
                            orient(jnp.stack([i_h, i_h])), orient(kheads(jnp.log(f))))
    o = merge_dirs(o2)
    o = o * lax.rsqrt(jnp.mean(o * o, axis=-1, keepdims=True) + NORM_EPS) * norm_g
    out = o.reshape(bsz, seq, HG_WIDTH) * jax.nn.silu(g)
    return out.astype(z.dtype)


def setup_inputs(seed: int = 0) -> dict:
    key = jax.random.key(seed)
    ks = jax.random.split(key, 20)
    nrm = lambda k, shape: jax.random.normal(k, shape, jnp.float32)
    L = DEPTH
    w0_base = jnp.linspace(-6.0, -1.0, RW_WIDTH, dtype=jnp.float32)
    return {
        "x": nrm(ks[0], (BATCH, SEQ, D_MODEL)),
        "pre_norm_g": 1.0 + 0.02 * nrm(ks[1], (L, D_MODEL)),
        "w_in": nrm(ks[2], (L, D_MODEL, IN_COLS)) * D_MODEL ** -0.5,
        "rw_shift_prev": jax.random.uniform(ks[3], (L, RW_COLS), jnp.float32, 0.0, 0.5),
        "rw_shift_next": jax.random.uniform(ks[4], (L, RW_COLS), jnp.float32, 0.0, 0.5),
        "rw_w0": w0_base + 0.1 * nrm(ks[5], (L, N_DIR, RW_WIDTH)),
        "rw_w2": 0.1 * nrm(ks[6], (L, N_DIR, DECAY_LORA, RW_WIDTH)) * DECAY_LORA ** -0.5,
        "rw_a0": 0.1 * nrm(ks[7], (L, N_DIR, RW_WIDTH)),
        "rw_a2": 0.1 * nrm(ks[8], (L, N_DIR, ICL_LORA, RW_WIDTH)) * ICL_LORA ** -0.5,
        "rw_k_k": 0.85 + 0.02 * nrm(ks[9], (L, RW_WIDTH)),
        "rw_k_a": 1.0 + 0.02 * nrm(ks[10], (L, RW_WIDTH)),
        "rw_r_k": -0.04 + 0.02 * nrm(ks[11], (L, RW_HEADS, RW_HEAD_DIM)),
        "rw_ln_w": 1.0 + 0.02 * nrm(ks[12], (L, RW_WIDTH)),
        "rw_ln_b": 0.02 * nrm(ks[13], (L, RW_WIDTH)),
        "hg_lb_logits": 0.5 * nrm(ks[14], (DEPTH + 1, HG_KEY)),
        "hg_norm_g": 1.0 + 0.02 * nrm(ks[15], (L, HG_DV)),
        "w_out": nrm(ks[16], (L, D_MIX, D_MODEL)) * D_MIX ** -0.5,
        "post_norm_g": 1.0 + 0.02 * nrm(ks[17], (L, D_MODEL)),
    }


def reference(x, pre_norm_g, w_in, rw_shift_prev, rw_shift_next, rw_w0, rw_w2, rw_a0, rw_a2,
              rw_k_k, rw_k_a, rw_r_k, rw_ln_w, rw_ln_b, hg_lb_logits, hg_norm_g, w_out, post_norm_g):
    lower_bounds = jnp.cumsum(jax.nn.softmax(hg_lb_logits.astype(jnp.float32), axis=0), axis=0)
    for l in range(DEPTH):
        h = rms_norm(x, pre_norm_g[l])
        p = h @ w_in[l]
        rw_in = centred_token_shift(p[..., :RW_COLS], rw_shift_prev[l], rw_shift_next[l])
        y_rw = rwkv7_mixer(rw_in, rw_w0[l], rw_w2[l], rw_a0[l], rw_a2[l], rw_k_k[l], rw_k_a[l],
                           rw_r_k[l], rw_ln_w[l], rw_ln_b[l])
        y_hg = hgrn2_mixer(p[..., RW_COLS:], lower_bounds[l], hg_norm_g[l])
        y = jnp.concatenate([y_rw, y_hg], axis=-1) @ w_out[l]
        x = x + rms_norm(y, post_norm_g[l])
    return x
```

```python
import functools

import jax
import jax.numpy as jnp
from jax import lax
from jax.experimental import pallas as pl
from jax.experimental.pallas import tpu as pltpu

D_MODEL = 1024
RW_HEAD_DIM = 64
RW_WIDTH = 512
RW_HEADS = 8
HG_DK = 128
HG_WIDTH = 512
HG_HEADS = 4
LORA = 64
NORM_EPS = 1e-6
GN_EPS = 64e-5
RW_COLS = 4 * RW_WIDTH + 4 * LORA
HG_COLS = 5 * HG_WIDTH
IN_COLS = RW_COLS + HG_COLS

CHUNK = 64
SUB = 16
ROW_TILE = 256
HALO = 8
VMEM_LIMIT = 48 * 1024 * 1024

HI = lax.Precision.HIGHEST


def _dot(a, b, precision=HI):
    return jnp.dot(a, b, preferred_element_type=jnp.float32, precision=precision)


def _dot_nt(a, b, precision=HI):
    return lax.dot_general(a, b, (((1,), (1,)), ((), ())),
                           preferred_element_type=jnp.float32, precision=precision)


def _dot_tn(a, b, precision=HI):
    return lax.dot_general(a, b, (((0,), (0,)), ((), ())),
                           preferred_element_type=jnp.float32, precision=precision)


def _sigmoid(x):
    return 1.0 / (1.0 + jnp.exp(-x))


def _in_proj_kernel(x_ref, g_ref, w_ref, prw_ref, phg_ref):
    x = x_ref[...]
    ms = jnp.mean(x * x, axis=-1, keepdims=True)
    h = (x * lax.rsqrt(ms + NORM_EPS) * g_ref[...]).astype(jnp.bfloat16)
    p = jnp.dot(h, w_ref[...], preferred_element_type=jnp.float32)
    prw_ref[...] = p[:, :RW_COLS]
    phg_ref[...] = p[:, RW_COLS:]


def _in_proj(x2, g, w_bf16):
    rows = x2.shape[0]
    return pl.pallas_call(
        _in_proj_kernel,
        grid=(rows // ROW_TILE,),
        in_specs=[
            pl.BlockSpec((ROW_TILE, D_MODEL), lambda i: (i, 0)),
            pl.BlockSpec((1, D_MODEL), lambda i: (0, 0)),
            pl.BlockSpec((D_MODEL, IN_COLS), lambda i: (0, 0)),
        ],
        out_specs=[
            pl.BlockSpec((ROW_TILE, RW_COLS), lambda i: (i, 0)),
            pl.BlockSpec((ROW_TILE, HG_COLS), lambda i: (i, 0)),
        ],
        out_shape=[
            jax.ShapeDtypeStruct((rows, RW_COLS), jnp.float32),
            jax.ShapeDtypeStruct((rows, HG_COLS), jnp.float32),
        ],
        compiler_params=pltpu.CompilerParams(
            dimension_semantics=("arbitrary",), vmem_limit_bytes=VMEM_LIMIT),
        name="in_proj",
    )(x2, g, w_bf16)


def _scan_masks(d, n):
    tt = lax.broadcasted_iota(jnp.int32, (n, n), 0)
    ss = lax.broadcasted_iota(jnp.int32, (n, n), 1)
    lag = (tt - ss) * _scan_sign(d)
    return lag > 0, lag >= 0


def _scan_sign(d):
    return 1 - 2 * d


def _chunk_index(d, c, n_chunks):
    return jnp.where(d == 0, c, n_chunks - 1 - c)


def _rw_kernel(pc_ref, pp_ref, pn_ref, mup_ref, mun_ref, w0_ref, w2_ref, a0_ref, a2_ref,
               kk_ref, ka_ref, rk_ref, seg_ref,
               y_ref, bonus_ref, gate_ref, s_ref, *, n_chunks):
    d = pl.program_id(0)
    c = pl.program_id(2)
    ci = _chunk_index(d, c, n_chunks)

    @pl.when(c == 0)
    def _():
        s_ref[...] = jnp.zeros_like(s_ref)

    p = pc_ref[...]
    row = lax.broadcasted_iota(jnp.int32, (CHUNK, RW_COLS), 0)
    prev_row = jnp.where(ci > 0, pp_ref[HALO - 1:HALO, :], 0.0)
    next_row = jnp.where(ci < n_chunks - 1, pn_ref[0:1, :], 0.0)
    p_prev = jnp.where(row == 0, prev_row, pltpu.roll(p, 1, 0))
    p_next = jnp.where(row == CHUNK - 1, next_row, pltpu.roll(p, CHUNK - 1, 0))
    z = p + mup_ref[...] * (p_prev - p) + mun_ref[...] * (p_next - p)

    W = RW_WIDTH
    r, k, v, g = z[:, 0:W], z[:, W:2 * W], z[:, 2 * W:3 * W], z[:, 3 * W:4 * W]
    wd = z[:, 4 * W:4 * W + 2 * LORA]
    ad = z[:, 4 * W + 2 * LORA:4 * W + 4 * LORA]

    w_raw = w0_ref[...] + _dot(jnp.tanh(wd), w2_ref[...])
    lw = -jnp.exp(jnp.float32(-0.5)) * _sigmoid(w_raw)
    a = _sigmoid(a0_ref[...] + _dot(ad, a2_ref[...]))

    seg = seg_ref[...]
    kk = k * kk_ref[...]
    kk = kk * lax.rsqrt(jnp.maximum(_dot(kk * kk, seg), 1e-24))
    k_d = k * (1.0 + (a - 1.0) * ka_ref[...])
    b_d = kk * a

    bonus_ref[...] = _dot(r * k_d * rk_ref[...], seg) * v
    gate_ref[...] = g * _sigmoid(g)

    strict, incl = _scan_masks(d, CHUNK)
    cl = _dot(incl.astype(jnp.float32), lw)
    total = jnp.sum(lw, axis=0, keepdims=True)
    e_in = jnp.exp(cl)
    e_out = jnp.exp(-cl)
    r_t = r * e_in
    a_t = kk * jnp.exp(cl - lw)
    k_t = k_d * e_out
    b_t = b_d * e_out
    e_end = jnp.exp(total - cl)
    k_end = k_d * e_end
    b_end = b_d * e_end
    g_end = jnp.exp(total)

    for h in range(RW_HEADS):
        sl = slice(h * RW_HEAD_DIM, (h + 1) * RW_HEAD_DIM)
        rh, ah, kh, bh, vh = r_t[:, sl], a_t[:, sl], k_t[:, sl], b_t[:, sl], v[:, sl]
        l_ab = jnp.where(strict, _dot_nt(ah, bh), 0.0)
        l_ak = jnp.where(strict, _dot_nt(ah, kh), 0.0)
        a_rk = jnp.where(incl, _dot_nt(rh, kh), 0.0)
        a_rb = jnp.where(incl, _dot_nt(rh, bh), 0.0)
        s0 = s_ref[h]

        x = jnp.concatenate([ah, _dot(l_ak, vh)], axis=1)
        m = -l_ab
        for it in range(6):
            x = x + _dot(m, x)
            if it < 5:
                m = _dot(m, m)
        u = -(_dot_nt(x[:, :RW_HEAD_DIM], s0) + x[:, RW_HEAD_DIM:])

        y_ref[:, sl] = _dot_nt(rh, s0) + _dot(a_rk, vh) + _dot(a_rb, u)
        s_ref[h] = (s0 * g_end[:, sl] + _dot_tn(vh, k_end[:, sl]) + _dot_tn(u, b_end[:, sl]))


def _rw_scan(p_rw, mup, mun, w0, w2p, a0, a2p, k_k, k_a, r_k, seg):
    bsz, seq, _ = p_rw.shape
    n_chunks = seq // CHUNK
    hb = CHUNK // HALO
    n_halo = seq // HALO

    def cur_map(d, b, c):
        return (b, _chunk_index(d, c, n_chunks), 0)

    def prev_map(d, b, c):
        return (b, jnp.maximum(_chunk_index(d, c, n_chunks) * hb - 1, 0), 0)

    def next_map(d, b, c):
        return (b, jnp.minimum((_chunk_index(d, c, n_chunks) + 1) * hb, n_halo - 1), 0)

    def out_map(d, b, c):
        return (d, b, _chunk_index(d, c, n_chunks), 0)

    row_spec = lambda n: pl.BlockSpec((1, n), lambda d, b, c: (0, 0))
    dir_row_spec = lambda n: pl.BlockSpec((None, 1, n), lambda d, b, c: (d, 0, 0))
    out_spec = pl.BlockSpec((None, None, CHUNK, RW_WIDTH), out_map)
    out_sds = jax.ShapeDtypeStruct((2, bsz, seq, RW_WIDTH), jnp.float32)
    return pl.pallas_call(
        functools.partial(_rw_kernel, n_chunks=n_chunks),
        grid=(2, bsz, n_chunks),
        in_specs=[
            pl.BlockSpec((None, CHUNK, RW_COLS), cur_map),
            pl.BlockSpec((None, HALO, RW_COLS), prev_map),
            pl.BlockSpec((None, HALO, RW_COLS), next_map),
            row_spec(RW_COLS), row_spec(RW_COLS),
            dir_row_spec(RW_WIDTH),
            pl.BlockSpec((None, 2 * LORA, RW_WIDTH), lambda d, b, c: (d, 0, 0)),
            dir_row_spec(RW_WIDTH),
            pl.BlockSpec((None, 2 * LORA, RW_WIDTH), lambda d, b, c: (d, 0, 0)),
            row_spec(RW_WIDTH), row_spec(RW_WIDTH), row_spec(RW_WIDTH),
            pl.BlockSpec((RW_WIDTH, RW_WIDTH), lambda d, b, c: (0, 0)),
        ],
        out_specs=[out_spec, out_spec, out_spec],
        out_shape=[out_sds, out_sds, out_sds],
        scratch_shapes=[pltpu.VMEM((RW_HEADS, RW_HEAD_DIM, RW_HEAD_DIM), jnp.float32)],
        compiler_params=pltpu.CompilerParams(
            dimension_semantics=("arbitrary", "arbitrary", "arbitrary"),
            vmem_limit_bytes=VMEM_LIMIT),
        name="rwkv7_scan",
    )(p_rw, p_rw, p_rw, mup, mun, w0, w2p, a0, a2p, k_k, k_a, r_k, seg)


def _hg_kernel(q_ref, f_ref, i_ref, lbl_ref, o_ref, s_ref, *, n_chunks, layer):
    d = pl.program_id(0)
    c = pl.program_id(2)

    @pl.when(c == 0)
    def _():
        s_ref[...] = jnp.zeros_like(s_ref)

    lg = lbl_ref[...]
    e = jnp.exp(lg - jnp.max(lg, axis=0, keepdims=True))
    lb = jnp.sum(e[0:layer + 1, :], axis=0, keepdims=True) / jnp.sum(e, axis=0, keepdims=True)

    q = q_ref[...]
    v = i_ref[...]
    fx = f_ref[...]
    f = lb + (1.0 - lb) * _sigmoid(fx)
    kf = (1.0 - lb) * _sigmoid(-fx)
    gl = jnp.log(f)

    strict, incl = _scan_masks(d, CHUNK)
    b = _dot(incl.astype(jnp.float32), gl)
    total = jnp.sum(gl, axis=0, keepdims=True)
    q_in = q * jnp.exp(b)
    k_end = kf * jnp.exp(total - b)
    g_end = jnp.exp(total)

    fwd = d == 0
    sgn = _scan_sign(d)
    row = lax.broadcasted_iota(jnp.int32, (CHUNK, HG_WIDTH), 0)
    trow = lax.broadcasted_iota(jnp.int32, (SUB, HG_WIDTH), 0)
    zero_row = jnp.zeros((1, HG_WIDTH), jnp.float32)
    n_sub = CHUNK // SUB

    for i in range(n_sub):
        lo, hi = i * SUB, (i + 1) * SUB
        q_i, b_i = q[lo:hi, :], b[lo:hi, :]
        ref_f = b[lo - 1:lo, :] if i > 0 else zero_row
        ref_b = b[hi:hi + 1, :] if i < n_sub - 1 else zero_row
        ref = jnp.where(fwd, ref_f, ref_b)
        earlier = row * sgn < jnp.where(fwd, lo, 1 - hi)
        q_off = q_i * jnp.exp(b_i - ref)
        k_off = jnp.where(earlier, kf * jnp.exp(jnp.where(earlier, ref - b, 0.0)), 0.0)

        acc = [None] * HG_HEADS
        for s in range(SUB):
            rs = lo + s
            m = (trow - s) * sgn >= 0
            dec = jnp.where(m, jnp.exp(jnp.where(m, b_i - b[rs:rs + 1, :], 0.0)), 0.0)
            xs = q_i * dec * kf[rs:rs + 1, :]
            for h in range(HG_HEADS):
                sl = slice(h * HG_DK, (h + 1) * HG_DK)
                w = jnp.sum(xs[:, sl], axis=-1, keepdims=True)
                term = w * v[rs:rs + 1, sl]
                acc[h] = term if acc[h] is None else acc[h] + term

        for h in range(HG_HEADS):
            sl = slice(h * HG_DK, (h + 1) * HG_DK)
            st = s_ref[h]
            scores = _dot_nt(q_off[:, sl], k_off[:, sl])
            o_ref[lo:hi, sl] = acc[h] + _dot(scores, v[:, sl]) + _dot_nt(q_in[lo:hi, sl], st)

    for h in range(HG_HEADS):
        sl = slice(h * HG_DK, (h + 1) * HG_DK)
        s_ref[h] = s_ref[h] * g_end[:, sl] + _dot_tn(v[:, sl], k_end[:, sl])


def _hg_scan(p_hg, lb_logits, layer):
    bsz, seq, _ = p_hg.shape
    n_chunks = seq // CHUNK

    def col_map(col):
        return lambda d, b, c: (b, _chunk_index(d, c, n_chunks), col)

    blk = (None, CHUNK, HG_WIDTH)
    return pl.pallas_call(
        functools.partial(_hg_kernel, n_chunks=n_chunks, layer=layer),
        grid=(2, bsz, n_chunks),
        in_specs=[
            pl.BlockSpec(blk, col_map(0)),
            pl.BlockSpec(blk, lambda d, b, c: (b, _chunk_index(d, c, n_chunks), 1 + d)),
            pl.BlockSpec(blk, col_map(3)),
            pl.BlockSpec(lb_logits.shape, lambda d, b, c: (0, 0)),
        ],
        out_specs=pl.BlockSpec((None, None, CHUNK, HG_WIDTH),
                               lambda d, b, c: (d, b, _chunk_index(d, c, n_chunks), 0)),
        out_shape=jax.ShapeDtypeStruct((2, bsz, seq, HG_WIDTH), jnp.float32),
        scratch_shapes=[pltpu.VMEM((HG_HEADS, HG_DK, HG_DK), jnp.float32)],
        compiler_params=pltpu.CompilerParams(
            dimension_semantics=("arbitrary", "arbitrary", "arbitrary"),
            vmem_limit_bytes=VMEM_LIMIT),
        name="hgrn2_scan",
    )(p_hg, p_hg, p_hg, lb_logits)


def _out_kernel(x_ref, yf_ref, yb_ref, bf_ref, bb_ref, gate_ref, of_ref, ob_ref, g_ref,
                lnw_ref, lnb_ref, hgn_ref, seg_ref, w_ref, post_ref, out_ref):
    y = yf_ref[...] + yb_ref[...]
    seg = seg_ref[...] * (1.0 / RW_HEAD_DIM)
    mu = _dot(y, seg)
    yc = y - mu
    var = _dot(yc * yc, seg)
    y = yc * lax.rsqrt(var + GN_EPS) * lnw_ref[...] + lnb_ref[...]
    rw = (y + bf_ref[...] + bb_ref[...]) * gate_ref[...]

    o = of_ref[...] + ob_ref[...]
    g = g_ref[...]
    parts = []
    for h in range(HG_HEADS):
        oh = o[:, h * HG_DK:(h + 1) * HG_DK]
        parts.append(oh * lax.rsqrt(jnp.mean(oh * oh, axis=-1, keepdims=True) + NORM_EPS))
    hg = jnp.concatenate(parts, axis=1) * hgn_ref[...] * (g * _sigmoid(g))

    cat = jnp.concatenate([rw, hg], axis=1).astype(jnp.bfloat16)
    yo = jnp.dot(cat, w_ref[...], preferred_element_type=jnp.float32)
    ms = jnp.mean(yo * yo, axis=-1, keepdims=True)
    out_ref[...] = x_ref[...] + yo * lax.rsqrt(ms + NORM_EPS) * post_ref[...]


def _out_proj(x2, y2, bonus2, gate2, o2, p_hg2, ln_w, ln_b, hg_norm, seg, w_bf16, post_g):
    rows = x2.shape[0]
    tile = lambda n: pl.BlockSpec((ROW_TILE, n), lambda i: (i, 0))
    dir_tile = lambda dd: pl.BlockSpec((None, ROW_TILE, RW_WIDTH), lambda i: (dd, i, 0))
    row_spec = lambda n: pl.BlockSpec((1, n), lambda i: (0, 0))
    return pl.pallas_call(
        _out_kernel,
        grid=(rows // ROW_TILE,),
        in_specs=[
            tile(D_MODEL),
            dir_tile(0), dir_tile(1),
            dir_tile(0), dir_tile(1),
            dir_tile(0),
            dir_tile(0), dir_tile(1),
            pl.BlockSpec((ROW_TILE, HG_WIDTH), lambda i: (i, 4)),
            row_spec(RW_WIDTH), row_spec(RW_WIDTH), row_spec(HG_WIDTH),
            pl.BlockSpec((RW_WIDTH, RW_WIDTH), lambda i: (0, 0)),
            pl.BlockSpec((D_MODEL, D_MODEL), lambda i: (0, 0)),
            row_spec(D_MODEL),
        ],
        out_specs=tile(D_MODEL),
        out_shape=jax.ShapeDtypeStruct((rows, D_MODEL), jnp.float32),
        compiler_params=pltpu.CompilerParams(
            dimension_semantics=("arbitrary",), vmem_limit_bytes=VMEM_LIMIT),
        name="out_proj",
    )(x2, y2, y2, bonus2, bonus2, gate2, o2, o2, p_hg2, ln_w, ln_b, hg_norm, seg, w_bf16, post_g)


def _pad_lora(w):
    z = jnp.zeros_like(w[0])
    return jnp.stack([jnp.concatenate([w[0], z], axis=0), jnp.concatenate([z, w[1]], axis=0)])


def kernel(x, pre_norm_g, w_in, rw_shift_prev, rw_shift_next, rw_w0, rw_w2, rw_a0, rw_a2, rw_k_k, rw_k_a,
           rw_r_k, rw_ln_w, rw_ln_b, hg_lb_logits, hg_norm_g, w_out, post_norm_g):
    bsz, seq, dm = x.shape
    depth = w_in.shape[0]
    rows = bsz * seq
    lane = jnp.arange(RW_WIDTH) // RW_HEAD_DIM
    seg = (lane[:, None] == lane[None, :]).astype(jnp.float32)
    row = lambda t: t.reshape(1, -1)
    for l in range(depth):
        x2 = x.reshape(rows, dm)
        p_rw, p_hg = _in_proj(x2, row(pre_norm_g[l]), w_in[l].astype(jnp.bfloat16))
        y2, bonus2, gate2 = _rw_scan(
            p_rw.reshape(bsz, seq, RW_COLS), row(rw_shift_prev[l]), row(rw_shift_next[l]),
            rw_w0[l][:, None, :], _pad_lora(rw_w2[l]), rw_a0[l][:, None, :], _pad_lora(rw_a2[l]),
            row(rw_k_k[l]), row(rw_k_a[l]), row(rw_r_k[l]), seg)
        o2 = _hg_scan(p_hg.reshape(bsz, seq, HG_COLS), hg_lb_logits, l)
        out = _out_proj(
            x2, y2.reshape(2, rows, RW_WIDTH), bonus2.reshape(2, rows, RW_WIDTH),
            gate2.reshape(2, rows, RW_WIDTH), o2.reshape(2, rows, HG_WIDTH), p_hg,
            row(rw_ln_w[l]), row(rw_ln_b[l]), row(jnp.tile(hg_norm_g[l], HG_HEADS)), seg,
            w_out[l].astype(jnp.bfloat16), row(post_norm_g[l]))
        x = out.reshape(bsz, seq, dm)
    return x
```

```python
import functools

import jax
import jax.numpy as jnp
from jax import lax
from jax.experimental import pallas as pl
from jax.experimental.pallas import tpu as pltpu

D_MODEL = 1024
RW_HEAD_DIM = 64
RW_WIDTH = 512
RW_HEADS = 8
HG_DK = 128
HG_WIDTH = 512
HG_HEADS = 4
LORA = 64
NORM_EPS = 1e-6
GN_EPS = 64e-5
RW_COLS = 4 * RW_WIDTH + 4 * LORA
HG_COLS = 5 * HG_WIDTH
IN_COLS = RW_COLS + HG_COLS

CHUNK = 64
SUB = 16
ROW_TILE = 256
HALO = 8
VMEM_LIMIT = 48 * 1024 * 1024

HI = 6
HP = 1
_NN = (((1,), (0,)), ((), ()))
_NT = (((1,), (1,)), ((), ()))
_TN = (((0,), (0,)), ((), ()))


def _split_bf16(x, terms=2):
    pieces = []
    for _ in range(terms - 1):
        hi = x.astype(jnp.bfloat16)
        pieces.append(hi)
        x = x - hi.astype(jnp.float32)
    pieces.append(x.astype(jnp.bfloat16))
    return pieces


def _select_rows(sel, x, terms):
    sel = sel.astype(jnp.bfloat16)
    return sum(jnp.dot(sel, p, preferred_element_type=jnp.float32) for p in _split_bf16(x, terms))


def _select_cols(x, sel, terms):
    sel = sel.astype(jnp.bfloat16)
    return sum(jnp.dot(p, sel, preferred_element_type=jnp.float32) for p in _split_bf16(x, terms))


def _mm(a, b, dims, passes):
    if passes == 6:
        return lax.dot_general(a, b, dims, preferred_element_type=jnp.float32,
                               precision=lax.Precision.HIGHEST)
    dg = functools.partial(lax.dot_general, dimension_numbers=dims, preferred_element_type=jnp.float32)
    if passes == 1:
        return dg(a.astype(jnp.bfloat16), b.astype(jnp.bfloat16))
    a_hi, a_lo = _split_bf16(a)
    b_hi, b_lo = _split_bf16(b)
    return dg(a_hi, b_hi) + (dg(a_hi, b_lo) + dg(a_lo, b_hi))


def _dot(a, b, passes=HI):
    return _mm(a, b, _NN, passes)


def _dot_nt(a, b, passes=HI):
    return _mm(a, b, _NT, passes)


def _dot_tn(a, b, passes=HI):
    return _mm(a, b, _TN, passes)


def _sigmoid(x):
    return 1.0 / (1.0 + jnp.exp(-x))


def _in_proj_kernel(x_ref, g_ref, w_ref, prw_ref, phg_ref):
    x = x_ref[...]
    ms = jnp.mean(x * x, axis=-1, keepdims=True)
    h = (x * lax.rsqrt(ms + NORM_EPS) * g_ref[...]).astype(jnp.bfloat16)
    p = jnp.dot(h, w_ref[...], preferred_element_type=jnp.float32)
    prw_ref[...] = p[:, :RW_COLS]
    phg_ref[...] = p[:, RW_COLS:]


def _in_proj(x2, g, w_bf16):
    rows = x2.shape[0]
    return pl.pallas_call(
        _in_proj_kernel,
        grid=(rows // ROW_TILE,),
        in_specs=[
            pl.BlockSpec((ROW_TILE, D_MODEL), lambda i: (i, 0)),
            pl.BlockSpec((1, D_MODEL), lambda i: (0, 0)),
            pl.BlockSpec((D_MODEL, IN_COLS), lambda i: (0, 0)),
        ],
        out_specs=[
            pl.BlockSpec((ROW_TILE, RW_COLS), lambda i: (i, 0)),
            pl.BlockSpec((ROW_TILE, HG_COLS), lambda i: (i, 0)),
        ],
        out_shape=[
            jax.ShapeDtypeStruct((rows, RW_COLS), jnp.float32),
            jax.ShapeDtypeStruct((rows, HG_COLS), jnp.float32),
        ],
        compiler_params=pltpu.CompilerParams(
            dimension_semantics=("arbitrary",), vmem_limit_bytes=VMEM_LIMIT),
        name="in_proj",
    )(x2, g, w_bf16)


def _scan_masks(d, n):
    tt = lax.broadcasted_iota(jnp.int32, (n, n), 0)
    ss = lax.broadcasted_iota(jnp.int32, (n, n), 1)
    lag = (tt - ss) * _scan_sign(d)
    return lag > 0, lag >= 0


def _scan_sign(d):
    return 1 - 2 * d


def _chunk_index(d, c, n_chunks):
    return jnp.where(d == 0, c, n_chunks - 1 - c)


def _rw_kernel(pc_ref, pp_ref, pn_ref, mup_ref, mun_ref, w0_ref, w2_ref, a0_ref, a2_ref,
               kk_ref, ka_ref, rk_ref, seg_ref,
               y_ref, bonus_ref, gate_ref, s_ref, *, n_chunks):
    d = pl.program_id(0)
    c = pl.program_id(2)
    ci = _chunk_index(d, c, n_chunks)

    @pl.when(c == 0)
    def _():
        s_ref[...] = jnp.zeros_like(s_ref)

    p = pc_ref[...]
    row = lax.broadcasted_iota(jnp.int32, (CHUNK, RW_COLS), 0)
    prev_row = jnp.where(ci > 0, pp_ref[HALO - 1:HALO, :], 0.0)
    next_row = jnp.where(ci < n_chunks - 1, pn_ref[0:1, :], 0.0)
    p_prev = jnp.where(row == 0, prev_row, pltpu.roll(p, 1, 0))
    p_next = jnp.where(row == CHUNK - 1, next_row, pltpu.roll(p, CHUNK - 1, 0))
    z = p + mup_ref[...] * (p_prev - p) + mun_ref[...] * (p_next - p)

    W = RW_WIDTH
    r, k, v, g = z[:, 0:W], z[:, W:2 * W], z[:, 2 * W:3 * W], z[:, 3 * W:4 * W]
    wd = z[:, 4 * W:4 * W + 2 * LORA]
    ad = z[:, 4 * W + 2 * LORA:4 * W + 4 * LORA]

    w_raw = w0_ref[...] + _dot(jnp.tanh(wd), w2_ref[...], 3)
    lw = -jnp.exp(jnp.float32(-0.5)) * _sigmoid(w_raw)
    a = _sigmoid(a0_ref[...] + _dot(ad, a2_ref[...], 3))

    seg = seg_ref[...]
    kk = k * kk_ref[...]
    kk = kk * lax.rsqrt(jnp.maximum(_select_cols(kk * kk, seg, 2), 1e-24))
    k_d = k * (1.0 + (a - 1.0) * ka_ref[...])
    b_d = kk * a

    bonus_ref[...] = _select_cols(r * k_d * rk_ref[...], seg, 2) * v
    gate_ref[...] = g * _sigmoid(g)

    strict, incl = _scan_masks(d, CHUNK)
    cl = _select_rows(incl, lw, 3)
    total = jnp.sum(lw, axis=0, keepdims=True)
    e_in = jnp.exp(cl)
    e_out = jnp.exp(-cl)
    r_t = r * e_in
    a_t = kk * jnp.exp(cl - lw)
    k_t = k_d * e_out
    b_t = b_d * e_out
    e_end = jnp.exp(total - cl)
    k_end = k_d * e_end
    b_end = b_d * e_end
    g_end = jnp.exp(total)

    heads = range(RW_HEADS)
    sls = [slice(h * RW_HEAD_DIM, (h + 1) * RW_HEAD_DIM) for h in heads]
    rh = [r_t[:, sl] for sl in sls]
    ah = [a_t[:, sl] for sl in sls]
    kh = [k_t[:, sl] for sl in sls]
    bh = [b_t[:, sl] for sl in sls]
    vh = [v[:, sl] for sl in sls]
    s0 = [s_ref[h] for h in heads]
    m = [-jnp.where(strict, _dot_nt(ah[h], bh[h], HP), 0.0) for h in heads]
    l_ak = [jnp.where(strict, _dot_nt(ah[h], kh[h], HP), 0.0) for h in heads]
    a_rk = [jnp.where(incl, _dot_nt(rh[h], kh[h], HP), 0.0) for h in heads]
    a_rb = [jnp.where(incl, _dot_nt(rh[h], bh[h], HP), 0.0) for h in heads]
    x = [jnp.concatenate([ah[h], _dot(l_ak[h], vh[h], HP)], axis=1) for h in heads]
    for it in range(6):
        x = [x[h] + _dot(m[h], x[h], HP) for h in heads]
        if it < 5:
            m = [_dot(m[h], m[h], HP) for h in heads]
    u = [-(_dot_nt(x[h][:, :RW_HEAD_DIM], s0[h], HP) + x[h][:, RW_HEAD_DIM:]) for h in heads]
    y = [_dot_nt(rh[h], s0[h], HP) + _dot(a_rk[h], vh[h], HP) + _dot(a_rb[h], u[h], HP) for h in heads]
    s1 = [s0[h] * g_end[:, sls[h]] + _dot_tn(vh[h], k_end[:, sls[h]], HP) + _dot_tn(u[h], b_end[:, sls[h]], HP)
          for h in heads]
    for h in heads:
        y_ref[:, sls[h]] = y[h]
        s_ref[h] = s1[h]


def _rw_scan(p_rw, mup, mun, w0, w2p, a0, a2p, k_k, k_a, r_k, seg):
    bsz, seq, _ = p_rw.shape
    n_chunks = seq // CHUNK
    hb = CHUNK // HALO
    n_halo = seq // HALO

    def cur_map(d, b, c):
        return (b, _chunk_index(d, c, n_chunks), 0)

    def prev_map(d, b, c):
        return (b, jnp.maximum(_chunk_index(d, c, n_chunks) * hb - 1, 0), 0)

    def next_map(d, b, c):
        return (b, jnp.minimum((_chunk_index(d, c, n_chunks) + 1) * hb, n_halo - 1), 0)

    def out_map(d, b, c):
        return (d, b, _chunk_index(d, c, n_chunks), 0)

    row_spec = lambda n: pl.BlockSpec((1, n), lambda d, b, c: (0, 0))
    dir_row_spec = lambda n: pl.BlockSpec((None, 1, n), lambda d, b, c: (d, 0, 0))
    out_spec = pl.BlockSpec((None, None, CHUNK, RW_WIDTH), out_map)
    out_sds = jax.ShapeDtypeStruct((2, bsz, seq, RW_WIDTH), jnp.float32)
    return pl.pallas_call(
        functools.partial(_rw_kernel, n_chunks=n_chunks),
        grid=(2, bsz, n_chunks),
        in_specs=[
            pl.BlockSpec((None, CHUNK, RW_COLS), cur_map),
            pl.BlockSpec((None, HALO, RW_COLS), prev_map),
            pl.BlockSpec((None, HALO, RW_COLS), next_map),
            row_spec(RW_COLS), row_spec(RW_COLS),
            dir_row_spec(RW_WIDTH),
            pl.BlockSpec((None, 2 * LORA, RW_WIDTH), lambda d, b, c: (d, 0, 0)),
            dir_row_spec(RW_WIDTH),
            pl.BlockSpec((None, 2 * LORA, RW_WIDTH), lambda d, b, c: (d, 0, 0)),
            row_spec(RW_WIDTH), row_spec(RW_WIDTH), row_spec(RW_WIDTH),
            pl.BlockSpec((RW_WIDTH, RW_WIDTH), lambda d, b, c: (0, 0)),
        ],
        out_specs=[out_spec, out_spec, out_spec],
        out_shape=[out_sds, out_sds, out_sds],
        scratch_shapes=[pltpu.VMEM((RW_HEADS, RW_HEAD_DIM, RW_HEAD_DIM), jnp.float32)],
        compiler_params=pltpu.CompilerParams(
            dimension_semantics=("arbitrary", "arbitrary", "arbitrary"),
            vmem_limit_bytes=VMEM_LIMIT),
        name="rwkv7_scan",
    )(p_rw, p_rw, p_rw, mup, mun, w0, w2p, a0, a2p, k_k, k_a, r_k, seg)


def _hg_kernel(q_ref, f_ref, i_ref, lbl_ref, o_ref, s_ref, *, n_chunks, layer):
    d = pl.program_id(0)
    c = pl.program_id(2)

    @pl.when(c == 0)
    def _():
        s_ref[...] = jnp.zeros_like(s_ref)

    lg = lbl_ref[...]
    e = jnp.exp(lg - jnp.max(lg, axis=0, keepdims=True))
    lb = jnp.sum(e[0:layer + 1, :], axis=0, keepdims=True) / jnp.sum(e, axis=0, keepdims=True)

    q = q_ref[...]
    v = i_ref[...]
    fx = f_ref[...]
    f = lb + (1.0 - lb) * _sigmoid(fx)
    kf = (1.0 - lb) * _sigmoid(-fx)
    gl = jnp.log(f)

    _, incl = _scan_masks(d, CHUNK)
    b = _select_rows(incl, gl, 3)
    total = jnp.sum(gl, axis=0, keepdims=True)
    q_in = q * jnp.exp(b)
    k_end = kf * jnp.exp(total - b)
    g_end = jnp.exp(total)

    fwd = d == 0
    sgn = _scan_sign(d)
    row = lax.broadcasted_iota(jnp.int32, (CHUNK, HG_WIDTH), 0)
    trow = lax.broadcasted_iota(jnp.int32, (SUB, HG_WIDTH), 0)
    zero_row = jnp.zeros((1, HG_WIDTH), jnp.float32)
    n_sub = CHUNK // SUB
    heads = range(HG_HEADS)
    sls = [slice(h * HG_DK, (h + 1) * HG_DK) for h in heads]
    st = [s_ref[h] for h in heads]

    o_mx = [_dot_nt(q_in[:, sls[h]], st[h], HP) for h in heads]
    o_off = []
    for i in range(n_sub):
        lo, hi = i * SUB, (i + 1) * SUB
        ref_f = b[lo - 1:lo, :] if i > 0 else zero_row
        ref_b = b[hi:hi + 1, :] if i < n_sub - 1 else zero_row
        ref = jnp.where(fwd, ref_f, ref_b)
        earlier = row * sgn < jnp.where(fwd, lo, 1 - hi)
        q_off = q[lo:hi, :] * jnp.exp(b[lo:hi, :] - ref)
        k_off = jnp.where(earlier, kf * jnp.exp(ref - b), 0.0)
        scores = [_dot_nt(q_off[:, sls[h]], k_off[:, sls[h]], HP) for h in heads]
        o_off.append([_dot(scores[h], v[:, sls[h]], HP) for h in heads])
    s1 = [st[h] * g_end[:, sls[h]] + _dot_tn(v[:, sls[h]], k_end[:, sls[h]], HP) for h in heads]

    for i in range(n_sub):
        lo, hi = i * SUB, (i + 1) * SUB
        q_i, b_i = q[lo:hi, :], b[lo:hi, :]
        acc = [o_mx[h][lo:hi, :] + o_off[i][h] for h in heads]
        for s in range(SUB):
            rs = lo + s
            m = (trow - s) * sgn >= 0
            xs = jnp.where(m, q_i * jnp.exp(b_i - b[rs:rs + 1, :]), 0.0) * kf[rs:rs + 1, :]
            for h in heads:
                w = jnp.sum(xs[:, sls[h]], axis=-1, keepdims=True)
                acc[h] = acc[h] + w * v[rs:rs + 1, sls[h]]
        for h in heads:
            o_ref[lo:hi, sls[h]] = acc[h]

    for h in heads:
        s_ref[h] = s1[h]


def _hg_scan(p_hg, lb_logits, layer):
    bsz, seq, _ = p_hg.shape
    n_chunks = seq // CHUNK

    def col_map(col):
        return lambda d, b, c: (b, _chunk_index(d, c, n_chunks), col)

    blk = (None, CHUNK, HG_WIDTH)
    return pl.pallas_call(
        functools.partial(_hg_kernel, n_chunks=n_chunks, layer=layer),
        grid=(2, bsz, n_chunks),
        in_specs=[
            pl.BlockSpec(blk, col_map(0)),
            pl.BlockSpec(blk, lambda d, b, c: (b, _chunk_index(d, c, n_chunks), 1 + d)),
            pl.BlockSpec(blk, col_map(3)),
            pl.BlockSpec(lb_logits.shape, lambda d, b, c: (0, 0)),
        ],
        out_specs=pl.BlockSpec((None, None, CHUNK, HG_WIDTH),
                               lambda d, b, c: (d, b, _chunk_index(d, c, n_chunks), 0)),
        out_shape=jax.ShapeDtypeStruct((2, bsz, seq, HG_WIDTH), jnp.float32),
        scratch_shapes=[pltpu.VMEM((HG_HEADS, HG_DK, HG_DK), jnp.float32)],
        compiler_params=pltpu.CompilerParams(
            dimension_semantics=("arbitrary", "arbitrary", "arbitrary"),
            vmem_limit_bytes=VMEM_LIMIT),
        name="hgrn2_scan",
    )(p_hg, p_hg, p_hg, lb_logits)


def _out_kernel(x_ref, yf_ref, yb_ref, bf_ref, bb_ref, gate_ref, of_ref, ob_ref, g_ref,
                lnw_ref, lnb_ref, hgn_ref, seg_ref, w_ref, post_ref, out_ref):
    y = yf_ref[...] + yb_ref[...]
    seg = seg_ref[...]
    mu = _select_cols(y, seg, 3) * (1.0 / RW_HEAD_DIM)
    yc = y - mu
    var = _select_cols(yc * yc, seg, 2) * (1.0 / RW_HEAD_DIM)
    y = yc * lax.rsqrt(var + GN_EPS) * lnw_ref[...] + lnb_ref[...]
    rw = (y + bf_ref[...] + bb_ref[...]) * gate_ref[...]

    o = of_ref[...] + ob_ref[...]
    g = g_ref[...]
    parts = []
    for h in range(HG_HEADS):
        oh = o[:, h * HG_DK:(h + 1) * HG_DK]
        parts.append(oh * lax.rsqrt(jnp.mean(oh * oh, axis=-1, keepdims=True) + NORM_EPS))
    hg = jnp.concatenate(parts, axis=1) * hgn_ref[...] * (g * _sigmoid(g))

    cat = jnp.concatenate([rw, hg], axis=1).astype(jnp.bfloat16)
    yo = jnp.dot(cat, w_ref[...], preferred_element_type=jnp.float32)
    ms = jnp.mean(yo * yo, axis=-1, keepdims=True)
    out_ref[...] = x_ref[...] + yo * lax.rsqrt(ms + NORM_EPS) * post_ref[...]


def _out_proj(x2, y2, bonus2, gate2, o2, p_hg2, ln_w, ln_b, hg_norm, seg, w_bf16, post_g):
    rows = x2.shape[0]
    tile = lambda n: pl.BlockSpec((ROW_TILE, n), lambda i: (i, 0))
    dir_tile = lambda dd: pl.BlockSpec((None, ROW_TILE, RW_WIDTH), lambda i: (dd, i, 0))
    row_spec = lambda n: pl.BlockSpec((1, n), lambda i: (0, 0))
    return pl.pallas_call(
        _out_kernel,
        grid=(rows // ROW_TILE,),
        in_specs=[
            tile(D_MODEL),
            dir_tile(0), dir_tile(1),
            dir_tile(0), dir_tile(1),
            dir_tile(0),
            dir_tile(0), dir_tile(1),
            pl.BlockSpec((ROW_TILE, HG_WIDTH), lambda i: (i, 4)),
            row_spec(RW_WIDTH), row_spec(RW_WIDTH), row_spec(HG_WIDTH),
            pl.BlockSpec((RW_WIDTH, RW_WIDTH), lambda i: (0, 0)),
            pl.BlockSpec((D_MODEL, D_MODEL), lambda i: (0, 0)),
            row_spec(D_MODEL),
        ],
        out_specs=tile(D_MODEL),
        out_shape=jax.ShapeDtypeStruct((rows, D_MODEL), jnp.float32),
        compiler_params=pltpu.CompilerParams(
            dimension_semantics=("arbitrary",), vmem_limit_bytes=VMEM_LIMIT),
        name="out_proj",
    )(x2, y2, y2, bonus2, bonus2, gate2, o2, o2, p_hg2, ln_w, ln_b, hg_norm, seg, w_bf16, post_g)


def _pad_lora(w):
    z = jnp.zeros_like(w[0])
    return jnp.stack([jnp.concatenate([w[0], z], axis=0), jnp.concatenate([z, w[1]], axis=0)])


def kernel(x, pre_norm_g, w_in, rw_shift_prev, rw_shift_next, rw_w0, rw_w2, rw_a0, rw_a2, rw_k_k, rw_k_a,
           rw_r_k, rw_ln_w, rw_ln_b, hg_lb_logits, hg_norm_g, w_out, post_norm_g):
    bsz, seq, dm = x.shape
    depth = w_in.shape[0]
    rows = bsz * seq
    lane = jnp.arange(RW_WIDTH) // RW_HEAD_DIM
    seg = (lane[:, None] == lane[None, :]).astype(jnp.bfloat16)
    row = lambda t: t.reshape(1, -1)
    for l in range(depth):
        x2 = x.reshape(rows, dm)
        p_rw, p_hg = _in_proj(x2, row(pre_norm_g[l]), w_in[l].astype(jnp.bfloat16))
        y2, bonus2, gate2 = _rw_scan(
            p_rw.reshape(bsz, seq, RW_COLS), row(rw_shift_prev[l]), row(rw_shift_next[l]),
            rw_w0[l][:, None, :], _pad_lora(rw_w2[l]), rw_a0[l][:, None, :], _pad_lora(rw_a2[l]),
            row(rw_k_k[l]), row(rw_k_a[l]), row(rw_r_k[l]), seg)
        o2 = _hg_scan(p_hg.reshape(bsz, seq, HG_COLS), hg_lb_logits, l)
        out = _out_proj(
            x2, y2.reshape(2, rows, RW_WIDTH), bonus2.reshape(2, rows, RW_WIDTH),
            gate2.reshape(2, rows, RW_WIDTH), o2.reshape(2, rows, HG_WIDTH), p_hg,
            row(rw_ln_w[l]), row(rw_ln_b[l]), row(jnp.tile(hg_norm_g[l], HG_HEADS)), seg,
            w_out[l].astype(jnp.bfloat16), row(post_norm_g[l]))
        x = out.reshape(bsz, seq, dm)
    return x
```

```python
import functools

import jax
import jax.numpy as jnp
from jax import lax
from jax.experimental import pallas as pl
from jax.experimental.pallas import tpu as pltpu

D_MODEL = 1024
RW_HEAD_DIM = 64
RW_WIDTH = 512
RW_HEADS = 8
HG_DK = 128
HG_WIDTH = 512
HG_HEADS = 4
LORA = 64
NORM_EPS = 1e-6
GN_EPS = 64e-5
RW_COLS = 4 * RW_WIDTH + 4 * LORA
HG_COLS = 5 * HG_WIDTH
IN_COLS = RW_COLS + HG_COLS
N_DIR = 2
RW_OPS = 6

LANES = 128
CHUNK = 64
SUB = 16
ROW_TILE = 256
HALO = 8
SCAN_BATCH = 4
VMEM_LIMIT = 48 * 1024 * 1024

HP = 1
_NN = (((1,), (0,)), ((), ()))
_NT = (((1,), (1,)), ((), ()))
_TN = (((0,), (0,)), ((), ()))
BF16 = jnp.bfloat16


def _split_bf16(x, terms=2):
    pieces = []
    for _ in range(terms - 1):
        hi = x.astype(BF16)
        pieces.append(hi)
        x = x - hi.astype(jnp.float32)
    pieces.append(x.astype(BF16))
    return pieces


def _select_rows(sel, x, terms):
    sel = sel.astype(BF16)
    acc = None
    for piece in _split_bf16(x, terms):
        t = jnp.dot(sel, piece, preferred_element_type=jnp.float32)
        acc = t if acc is None else acc + t
    return acc


def _select_cols(x, sel, terms):
    sel = sel.astype(BF16)
    acc = None
    for piece in _split_bf16(x, terms):
        t = jnp.dot(piece, sel, preferred_element_type=jnp.float32)
        acc = t if acc is None else acc + t
    return acc


def _mm(a, b, dims, passes):
    dg = functools.partial(lax.dot_general, dimension_numbers=dims, preferred_element_type=jnp.float32)
    if passes == 1:
        return dg(a.astype(BF16), b.astype(BF16))
    a_hi, a_lo = _split_bf16(a)
    b_hi, b_lo = _split_bf16(b)
    return dg(a_hi, b_hi) + (dg(a_hi, b_lo) + dg(a_lo, b_hi))


def _dot(a, b, passes):
    return _mm(a, b, _NN, passes)


def _dot_nt(a, b, passes):
    return _mm(a, b, _NT, passes)


def _dot_tn(a, b, passes):
    return _mm(a, b, _TN, passes)


def _sigmoid(x):
    return 1.0 / (1.0 + jnp.exp(-x))


def _in_proj_kernel(x_ref, g_ref, w_ref, prw_ref, phg_ref):
    x = x_ref[...]
    ms = jnp.mean(x * x, axis=-1, keepdims=True)
    h = (x * lax.rsqrt(ms + NORM_EPS) * g_ref[...]).astype(BF16)
    p = jnp.dot(h, w_ref[...], preferred_element_type=jnp.float32)
    prw_ref[...] = p[:, :RW_COLS]
    phg_ref[...] = p[:, RW_COLS:]


def _in_proj(x2, g, w_bf16):
    rows = x2.shape[0]
    return pl.pallas_call(
        _in_proj_kernel,
        grid=(rows // ROW_TILE,),
        in_specs=[
            pl.BlockSpec((ROW_TILE, D_MODEL), lambda i: (i, 0)),
            pl.BlockSpec((1, D_MODEL), lambda i: (0, 0)),
            pl.BlockSpec((D_MODEL, IN_COLS), lambda i: (0, 0)),
        ],
        out_specs=[
            pl.BlockSpec((ROW_TILE, RW_COLS), lambda i: (i, 0)),
            pl.BlockSpec((ROW_TILE, HG_COLS), lambda i: (i, 0)),
        ],
        out_shape=[
            jax.ShapeDtypeStruct((rows, RW_COLS), jnp.float32),
            jax.ShapeDtypeStruct((rows, HG_COLS), jnp.float32),
        ],
        compiler_params=pltpu.CompilerParams(
            dimension_semantics=("arbitrary",), vmem_limit_bytes=VMEM_LIMIT),
        name="in_proj",
    )(x2, g, w_bf16)


def _rw_prep_kernel(pc_ref, pp_ref, pn_ref, mup_ref, mun_ref, w0_ref, w2_ref, a0_ref, a2_ref,
                    kk_ref, ka_ref, rk_ref, seg_ref, tri_ref, blk_ref,
                    ops_ref, v_ref, gend_ref, bonus_ref, gate_ref, *, n_tiles):
    j = pl.program_id(1)

    p = pc_ref[...]
    row = lax.broadcasted_iota(jnp.int32, (ROW_TILE, RW_COLS), 0)
    prev_row = jnp.where(j > 0, pp_ref[HALO - 1:HALO, :], 0.0)
    next_row = jnp.where(j < n_tiles - 1, pn_ref[0:1, :], 0.0)
    p_prev = jnp.where(row == 0, prev_row, pltpu.roll(p, 1, 0))
    p_next = jnp.where(row == ROW_TILE - 1, next_row, pltpu.roll(p, ROW_TILE - 1, 0))
    z = p + mup_ref[...] * (p_prev - p) + mun_ref[...] * (p_next - p)

    W = RW_WIDTH
    r, k, v, g = z[:, 0:W], z[:, W:2 * W], z[:, 2 * W:3 * W], z[:, 3 * W:4 * W]
    wd = z[:, 4 * W:4 * W + 2 * LORA]
    ad = z[:, 4 * W + 2 * LORA:4 * W + 4 * LORA]
    gate_ref[...] = g * _sigmoid(g)
    v_ref[...] = v.astype(BF16)

    seg = seg_ref[...]
    blk = blk_ref[...]
    kk = k * kk_ref[...]
    kk = kk * lax.rsqrt(jnp.maximum(_select_cols(kk * kk, seg, 2), 1e-24))
    tanh_wd = jnp.tanh(wd)

    k_sum = None
    for d in range(N_DIR):
        w_raw = w0_ref[d:d + 1, :] + _dot(tanh_wd, w2_ref[d], 3)
        lw = -jnp.exp(jnp.float32(-0.5)) * _sigmoid(w_raw)
        a = _sigmoid(a0_ref[d:d + 1, :] + _dot(ad, a2_ref[d], 3))
        k_d = k * (1.0 + (a - 1.0) * ka_ref[...])
        b_d = kk * a
        k_sum = k_d if k_sum is None else k_sum + k_d
        cl = _select_rows(tri_ref[d], lw, 3)
        total = _select_rows(blk, lw, 3)
        e_out = jnp.exp(-cl)
        e_end = jnp.exp(total - cl)
        ops = (r * jnp.exp(cl),
               kk * jnp.exp(cl - lw),
               k_d * e_out, b_d * e_out,
               k_d * e_end, b_d * e_end)
        for i, op in enumerate(ops):
            ops_ref[d, :, i * W:(i + 1) * W] = op.astype(BF16)
        for q in range(ROW_TILE // CHUNK):
            gend_ref[d, q] = jnp.exp(total[q * CHUNK:q * CHUNK + HALO, :])

    bonus_ref[...] = _select_cols(r * k_sum * rk_ref[...], seg, 2) * v


def _rw_prep(p_rw, mup, mun, w0, w2p, a0, a2p, k_k, k_a, r_k, seg, tri, blk):
    bsz, seq, _ = p_rw.shape
    n_tiles = seq // ROW_TILE
    hb = ROW_TILE // HALO
    n_halo = seq // HALO
    n_chunks = seq // CHUNK
    cpt = ROW_TILE // CHUNK
    full = lambda a: pl.BlockSpec(a.shape, lambda b, j: (0,) * a.ndim)
    tile = lambda n: pl.BlockSpec((None, ROW_TILE, n), lambda b, j: (b, j, 0))
    return pl.pallas_call(
        functools.partial(_rw_prep_kernel, n_tiles=n_tiles),
        grid=(bsz, n_tiles),
        in_specs=[
            tile(RW_COLS),
            pl.BlockSpec((None, HALO, RW_COLS), lambda b, j: (b, jnp.maximum(j * hb - 1, 0), 0)),
            pl.BlockSpec((None, HALO, RW_COLS), lambda b, j: (b, jnp.minimum((j + 1) * hb, n_halo - 1), 0)),
            full(mup), full(mun), full(w0), full(w2p), full(a0), full(a2p),
            full(k_k), full(k_a), full(r_k), full(seg), full(tri), full(blk),
        ],
        out_specs=[
            pl.BlockSpec((N_DIR, None, ROW_TILE, RW_OPS * RW_WIDTH), lambda b, j: (0, b, j, 0)),
            tile(RW_WIDTH),
            pl.BlockSpec((N_DIR, None, cpt, HALO, RW_WIDTH), lambda b, j: (0, b, j, 0, 0)),
            tile(RW_WIDTH),
            tile(RW_WIDTH),
        ],
        out_shape=[
            jax.ShapeDtypeStruct((N_DIR, bsz, seq, RW_OPS * RW_WIDTH), BF16),
            jax.ShapeDtypeStruct((bsz, seq, RW_WIDTH), BF16),
            jax.ShapeDtypeStruct((N_DIR, bsz, n_chunks, HALO, RW_WIDTH), jnp.float32),
            jax.ShapeDtypeStruct((bsz, seq, RW_WIDTH), jnp.float32),
            jax.ShapeDtypeStruct((bsz, seq, RW_WIDTH), jnp.float32),
        ],
        compiler_params=pltpu.CompilerParams(
            dimension_semantics=("arbitrary", "arbitrary"), vmem_limit_bytes=VMEM_LIMIT),
        name="rwkv7_prep",
    )(p_rw, p_rw, p_rw, mup, mun, w0, w2p, a0, a2p, k_k, k_a, r_k, seg, tri, blk)


def _rw_scan_kernel(of_ref, ob_ref, vf_ref, vb_ref, gf_ref, gb_ref, yf_ref, yb_ref, s_ref):
    c = pl.program_id(1)

    @pl.when(c == 0)
    def _():
        s_ref[...] = jnp.zeros_like(s_ref)

    N, W = RW_HEAD_DIM, RW_WIDTH
    lane = lax.broadcasted_iota(jnp.int32, (CHUNK, LANES), 1)
    tt = lax.broadcasted_iota(jnp.int32, (CHUNK, LANES), 0)
    half = (lane < N, lane >= N)
    diag = ((lax.broadcasted_iota(jnp.int32, (LANES, LANES), 0) < N)
            == (lax.broadcasted_iota(jnp.int32, (LANES, LANES), 1) < N))
    dir_refs = ((of_ref, vf_ref, gf_ref, yf_ref), (ob_ref, vb_ref, gb_ref, yb_ref))

    pairs = [(q, d, j) for q in range(SCAN_BATCH) for d in range(N_DIR) for j in range(W // LANES)]
    probs = [(pi, par) for pi in range(len(pairs)) for par in range(2)]

    def op(pi, i):
        q, d, j = pairs[pi]
        return dir_refs[d][0][q, :, i * W + j * LANES:i * W + (j + 1) * LANES]

    r_p = [op(pi, 0) for pi in range(len(pairs))]
    a_p = [op(pi, 1) for pi in range(len(pairs))]
    k_p = [op(pi, 2) for pi in range(len(pairs))]
    b_p = [op(pi, 3) for pi in range(len(pairs))]
    ke_p = [op(pi, 4) for pi in range(len(pairs))]
    be_p = [op(pi, 5) for pi in range(len(pairs))]
    v_p = [dir_refs[d][1][q, :, j * LANES:(j + 1) * LANES] for q, d, j in pairs]
    s_p = [s_ref[q, d, j] for q, d, j in pairs]

    def lag(pi):
        d = pairs[pi][1]
        delta = tt - (lane & (N - 1))
        return delta if d == 0 else -delta

    gram = []
    for pi, par in probs:
        lhs = jnp.concatenate([a_p[pi], r_p[pi]], axis=0)
        lhs = jnp.where(jnp.concatenate([half[par], half[par]], axis=0), lhs, jnp.zeros_like(lhs))
        rhs = jnp.concatenate([k_p[pi], b_p[pi]] if par == 0 else [b_p[pi], k_p[pi]], axis=0)
        gram.append(lax.dot_general(lhs, rhs, _NT, preferred_element_type=jnp.float32))

    za, zb, bot = [], [], []
    for n, (pi, par) in enumerate(probs):
        top = gram[n][:CHUNK]
        lg = lag(pi)
        m0 = jnp.where(jnp.where(half[1 - par], lg, 0) > 0, -top, 0.0)
        l_ak = jnp.where(jnp.where(half[par], lg, 0) > 0, top, 0.0)
        bot.append(jnp.where(lg >= 0, gram[n][CHUNK:], 0.0).astype(BF16))
        vv = jnp.concatenate([v_p[pi], v_p[pi]], axis=0)
        zb.append(jnp.dot(l_ak.astype(BF16), vv, preferred_element_type=jnp.float32))
        za.append(jnp.where(half[par], a_p[pi].astype(jnp.float32), m0))

    for level in range(6):
        new_za, new_zb = [], []
        for n, (pi, par) in enumerate(probs):
            m = jnp.where(half[1 - par], za[n], 0.0).astype(BF16)
            z2 = jnp.concatenate([za[n], zb[n]], axis=1).astype(BF16)
            p = jnp.dot(m, jnp.concatenate([z2, z2], axis=0), preferred_element_type=jnp.float32)
            new_za.append(p[:, :LANES] + jnp.where(half[par], za[n], 0.0))
            new_zb.append(p[:, LANES:] + zb[n])
        za, zb = new_za, new_zb

    n_pairs = range(len(pairs))
    s_bf = [s_p[pi].astype(BF16) for pi in n_pairs]
    wr = [jnp.concatenate([jnp.where(half[0], za[2 * pi], za[2 * pi + 1]).astype(BF16), r_p[pi]], axis=0)
          for pi in n_pairs]
    ws = [lax.dot_general(wr[pi], s_bf[pi], _NT, preferred_element_type=jnp.float32) for pi in n_pairs]
    u_bf = [(-(ws[pi][:CHUNK] + jnp.where(half[0], zb[2 * pi], zb[2 * pi + 1]))).astype(BF16)
            for pi in n_pairs]
    y_e = [jnp.dot(bot[2 * pi], jnp.concatenate([v_p[pi], u_bf[pi]], axis=0),
                   preferred_element_type=jnp.float32) for pi in n_pairs]
    y_o = [jnp.dot(bot[2 * pi + 1], jnp.concatenate([u_bf[pi], v_p[pi]], axis=0),
                   preferred_element_type=jnp.float32) for pi in n_pairs]
    upd = [lax.dot_general(jnp.concatenate([v_p[pi], u_bf[pi]], axis=0),
                           jnp.concatenate([ke_p[pi], be_p[pi]], axis=0), _TN,
                           preferred_element_type=jnp.float32) for pi in n_pairs]
    for pi, (q, d, j) in enumerate(pairs):
        dir_refs[d][3][q, :, j * LANES:(j + 1) * LANES] = ws[pi][CHUNK:] + jnp.where(half[0], y_e[pi], y_o[pi])
        g_end = dir_refs[d][2][q, 0:1, j * LANES:(j + 1) * LANES]
        s_ref[q, d, j] = s_p[pi] * g_end + jnp.where(diag, upd[pi], 0.0)


def _rw_scan(ops, v_bf, gend):
    _, bsz, seq, _ = ops.shape
    n_chunks = seq // CHUNK
    rev = lambda c: n_chunks - 1 - c
    return pl.pallas_call(
        _rw_scan_kernel,
        grid=(bsz // SCAN_BATCH, n_chunks),
        in_specs=[
            pl.BlockSpec((None, SCAN_BATCH, CHUNK, RW_OPS * RW_WIDTH), lambda b, c: (0, b, c, 0)),
            pl.BlockSpec((None, SCAN_BATCH, CHUNK, RW_OPS * RW_WIDTH), lambda b, c: (1, b, rev(c), 0)),
            pl.BlockSpec((SCAN_BATCH, CHUNK, RW_WIDTH), lambda b, c: (b, c, 0)),
            pl.BlockSpec((SCAN_BATCH, CHUNK, RW_WIDTH), lambda b, c: (b, rev(c), 0)),
            pl.BlockSpec((None, SCAN_BATCH, None, HALO, RW_WIDTH), lambda b, c: (0, b, c, 0, 0)),
            pl.BlockSpec((None, SCAN_BATCH, None, HALO, RW_WIDTH), lambda b, c: (1, b, rev(c), 0, 0)),
        ],
        out_specs=[
            pl.BlockSpec((SCAN_BATCH, CHUNK, RW_WIDTH), lambda b, c: (b, c, 0)),
            pl.BlockSpec((SCAN_BATCH, CHUNK, RW_WIDTH), lambda b, c: (b, rev(c), 0)),
        ],
        out_shape=[jax.ShapeDtypeStruct((bsz, seq, RW_WIDTH), jnp.float32)] * 2,
        scratch_shapes=[pltpu.VMEM((SCAN_BATCH, N_DIR, RW_WIDTH // LANES, LANES, LANES), jnp.float32)],
        compiler_params=pltpu.CompilerParams(
            dimension_semantics=("arbitrary", "arbitrary"), vmem_limit_bytes=VMEM_LIMIT),
        name="rwkv7_scan",
    )(ops, ops, v_bf, v_bf, gend, gend)


def _scan_sign(d):
    return 1 - 2 * d


def _chunk_index(d, c, n_chunks):
    return jnp.where(d == 0, c, n_chunks - 1 - c)


def _hg_kernel(q_ref, f_ref, i_ref, lbl_ref, o_ref, s_ref, *, n_chunks, layer):
    d = pl.program_id(0)
    c = pl.program_id(2)

    @pl.when(c == 0)
    def _():
        s_ref[...] = jnp.zeros_like(s_ref)

    lg = lbl_ref[...]
    e = jnp.exp(lg - jnp.max(lg, axis=0, keepdims=True))
    lb = jnp.sum(e[0:layer + 1, :], axis=0, keepdims=True) / jnp.sum(e, axis=0, keepdims=True)

    q = q_ref[...]
    v = i_ref[...]
    fx = f_ref[...]
    f = lb + (1.0 - lb) * _sigmoid(fx)
    kf = (1.0 - lb) * _sigmoid(-fx)
    gl = jnp.log(f)

    sgn = _scan_sign(d)
    tt = lax.broadcasted_iota(jnp.int32, (CHUNK, CHUNK), 0)
    ss = lax.broadcasted_iota(jnp.int32, (CHUNK, CHUNK), 1)
    incl = (tt - ss) * sgn >= 0
    b = _select_rows(incl, gl, 3)
    total = jnp.sum(gl, axis=0, keepdims=True)
    q_in = q * jnp.exp(b)
    k_end = kf * jnp.exp(total - b)
    g_end = jnp.exp(total)

    fwd = d == 0
    row = lax.broadcasted_iota(jnp.int32, (CHUNK, HG_WIDTH), 0)
    trow = lax.broadcasted_iota(jnp.int32, (SUB, HG_WIDTH), 0)
    zero_row = jnp.zeros((1, HG_WIDTH), jnp.float32)
    n_sub = CHUNK // SUB
    heads = range(HG_HEADS)
    sls = [slice(h * HG_DK, (h + 1) * HG_DK) for h in heads]
    st = [s_ref[h] for h in heads]

    o_mx = [_dot_nt(q_in[:, sls[h]], st[h], HP) for h in heads]
    o_off = []
    for i in range(n_sub):
        lo, hi = i * SUB, (i + 1) * SUB
        ref_f = b[lo - 1:lo, :] if i > 0 else zero_row
        ref_b = b[hi:hi + 1, :] if i < n_sub - 1 else zero_row
        ref = jnp.where(fwd, ref_f, ref_b)
        earlier = row * sgn < jnp.where(fwd, lo, 1 - hi)
        q_off = q[lo:hi, :] * jnp.exp(b[lo:hi, :] - ref)
        k_off = jnp.where(earlier, kf * jnp.exp(ref - b), 0.0)
        scores = [_dot_nt(q_off[:, sls[h]], k_off[:, sls[h]], HP) for h in heads]
        o_off.append([_dot(scores[h], v[:, sls[h]], HP) for h in heads])
    s1 = [st[h] * g_end[:, sls[h]] + _dot_tn(v[:, sls[h]], k_end[:, sls[h]], HP) for h in heads]

    for i in range(n_sub):
        lo, hi = i * SUB, (i + 1) * SUB
        q_i, b_i = q[lo:hi, :], b[lo:hi, :]
        acc = [o_mx[h][lo:hi, :] + o_off[i][h] for h in heads]
        for s in range(SUB):
            rs = lo + s
            m = (trow - s) * sgn >= 0
            xs = jnp.where(m, q_i * jnp.exp(b_i - b[rs:rs + 1, :]), 0.0) * kf[rs:rs + 1, :]
            for h in heads:
                w = jnp.sum(xs[:, sls[h]], axis=-1, keepdims=True)
                acc[h] = acc[h] + w * v[rs:rs + 1, sls[h]]
        for h in heads:
            o_ref[lo:hi, sls[h]] = acc[h]

    for h in heads:
        s_ref[h] = s1[h]


def _hg_scan(p_hg, lb_logits, layer):
    bsz, seq, _ = p_hg.shape
    n_chunks = seq // CHUNK

    def col_map(col):
        return lambda d, b, c: (b, _chunk_index(d, c, n_chunks), col)

    blk = (None, CHUNK, HG_WIDTH)
    return pl.pallas_call(
        functools.partial(_hg_kernel, n_chunks=n_chunks, layer=layer),
        grid=(2, bsz, n_chunks),
        in_specs=[
            pl.BlockSpec(blk, col_map(0)),
            pl.BlockSpec(blk, lambda d, b, c: (b, _chunk_index(d, c, n_chunks), 1 + d)),
            pl.BlockSpec(blk, col_map(3)),
            pl.BlockSpec(lb_logits.shape, lambda d, b, c: (0, 0)),
        ],
        out_specs=pl.BlockSpec((None, None, CHUNK, HG_WIDTH),
                               lambda d, b, c: (d, b, _chunk_index(d, c, n_chunks), 0)),
        out_shape=jax.ShapeDtypeStruct((2, bsz, seq, HG_WIDTH), jnp.float32),
        scratch_shapes=[pltpu.VMEM((HG_HEADS, HG_DK, HG_DK), jnp.float32)],
        compiler_params=pltpu.CompilerParams(
            dimension_semantics=("arbitrary", "arbitrary", "arbitrary"),
            vmem_limit_bytes=VMEM_LIMIT),
        name="hgrn2_scan",
    )(p_hg, p_hg, p_hg, lb_logits)


def _out_kernel(x_ref, yf_ref, yb_ref, bonus_ref, gate_ref, of_ref, ob_ref, g_ref,
                lnw_ref, lnb_ref, hgn_ref, seg_ref, w_ref, post_ref, out_ref):
    y = yf_ref[...] + yb_ref[...]
    seg = seg_ref[...]
    mu = _select_cols(y, seg, 3) * (1.0 / RW_HEAD_DIM)
    yc = y - mu
    var = _select_cols(yc * yc, seg, 2) * (1.0 / RW_HEAD_DIM)
    y = yc * lax.rsqrt(var + GN_EPS) * lnw_ref[...] + lnb_ref[...]
    rw = (y + bonus_ref[...]) * gate_ref[...]

    o = of_ref[...] + ob_ref[...]
    g = g_ref[...]
    parts = []
    for h in range(HG_HEADS):
        oh = o[:, h * HG_DK:(h + 1) * HG_DK]
        parts.append(oh * lax.rsqrt(jnp.mean(oh * oh, axis=-1, keepdims=True) + NORM_EPS))
    hg = jnp.concatenate(parts, axis=1) * hgn_ref[...] * (g * _sigmoid(g))

    cat = jnp.concatenate([rw, hg], axis=1).astype(BF16)
    yo = jnp.dot(cat, w_ref[...], preferred_element_type=jnp.float32)
    ms = jnp.mean(yo * yo, axis=-1, keepdims=True)
    out_ref[...] = x_ref[...] + yo * lax.rsqrt(ms + NORM_EPS) * post_ref[...]


def _out_proj(x2, y_f, y_b, bonus, gate, o2, p_hg2, ln_w, ln_b, hg_norm, seg, w_bf16, post_g):
    rows = x2.shape[0]
    tile = lambda n: pl.BlockSpec((ROW_TILE, n), lambda i: (i, 0))
    dir_tile = lambda dd: pl.BlockSpec((None, ROW_TILE, HG_WIDTH), lambda i: (dd, i, 0))
    row_spec = lambda n: pl.BlockSpec((1, n), lambda i: (0, 0))
    return pl.pallas_call(
        _out_kernel,
        grid=(rows // ROW_TILE,),
        in_specs=[
            tile(D_MODEL),
            tile(RW_WIDTH), tile(RW_WIDTH), tile(RW_WIDTH), tile(RW_WIDTH),
            dir_tile(0), dir_tile(1),
            pl.BlockSpec((ROW_TILE, HG_WIDTH), lambda i: (i, 4)),
            row_spec(RW_WIDTH), row_spec(RW_WIDTH), row_spec(HG_WIDTH),
            pl.BlockSpec((RW_WIDTH, RW_WIDTH), lambda i: (0, 0)),
            pl.BlockSpec((D_MODEL, D_MODEL), lambda i: (0, 0)),
            row_spec(D_MODEL),
        ],
        out_specs=tile(D_MODEL),
        out_shape=jax.ShapeDtypeStruct((rows, D_MODEL), jnp.float32),
        compiler_params=pltpu.CompilerParams(
            dimension_semantics=("arbitrary",), vmem_limit_bytes=VMEM_LIMIT),
        name="out_proj",
    )(x2, y_f, y_b, bonus, gate, o2, o2, p_hg2, ln_w, ln_b, hg_norm, seg, w_bf16, post_g)


def _pad_lora(w):
    z = jnp.zeros_like(w[0])
    return jnp.stack([jnp.concatenate([w[0], z], axis=0), jnp.concatenate([z, w[1]], axis=0)])


def _chunk_selectors():
    t = jnp.arange(ROW_TILE)
    same = (t[:, None] // CHUNK) == (t[None, :] // CHUNK)
    fwd = same & (t[None, :] <= t[:, None])
    bwd = same & (t[None, :] >= t[:, None])
    return jnp.stack([fwd, bwd]).astype(BF16), same.astype(BF16)


def kernel(x, pre_norm_g, w_in, rw_shift_prev, rw_shift_next, rw_w0, rw_w2, rw_a0, rw_a2, rw_k_k, rw_k_a,
           rw_r_k, rw_ln_w, rw_ln_b, hg_lb_logits, hg_norm_g, w_out, post_norm_g):
    bsz, seq, dm = x.shape
    depth = w_in.shape[0]
    rows = bsz * seq
    lane = jnp.arange(RW_WIDTH) // RW_HEAD_DIM
    seg = (lane[:, None] == lane[None, :]).astype(BF16)
    tri, blk = _chunk_selectors()
    row = lambda t: t.reshape(1, -1)
    flat = lambda t: t.reshape(rows, t.shape[-1])
    for l in range(depth):
        x2 = x.reshape(rows, dm)
        p_rw, p_hg = _in_proj(x2, row(pre_norm_g[l]), w_in[l].astype(BF16))
        ops, v_bf, gend, bonus, gate = _rw_prep(
            p_rw.reshape(bsz, seq, RW_COLS), row(rw_shift_prev[l]), row(rw_shift_next[l]),
            rw_w0[l], _pad_lora(rw_w2[l]), rw_a0[l], _pad_lora(rw_a2[l]),
            row(rw_k_k[l]), row(rw_k_a[l]), row(rw_r_k[l]), seg, tri, blk)
        y_f, y_b = _rw_scan(ops, v_bf, gend)
        o2 = _hg_scan(p_hg.reshape(bsz, seq, HG_COLS), hg_lb_logits, l)
        out = _out_proj(
            x2, flat(y_f), flat(y_b), flat(bonus), flat(gate), o2.reshape(2, rows, HG_WIDTH), p_hg,
            row(rw_ln_w[l]), row(rw_ln_b[l]), row(jnp.tile(hg_norm_g[l], HG_HEADS)), seg,
            w_out[l].astype(BF16), row(post_norm_g[l]))
        x = out.reshape(bsz, seq, dm)
    return x
```

```python
import functools

import jax
import jax.numpy as jnp
from jax import lax
from jax.experimental import pallas as pl
from jax.experimental.pallas import tpu as pltpu

D_MODEL = 1024
RW_HEAD_DIM = 64
RW_WIDTH = 512
RW_HEADS = 8
HG_DK = 128
HG_WIDTH = 512
HG_HEADS = 4
LORA = 64
NORM_EPS = 1e-6
GN_EPS = 64e-5
RW_COLS = 4 * RW_WIDTH + 4 * LORA
HG_COLS = 5 * HG_WIDTH
IN_COLS = RW_COLS + HG_COLS
N_DIR = 2
RW_OPS = 6

LANES = 128
CHUNK = 64
SUB = 8
ROW_TILE = 256
HG_TILE = 128
HALO = 8
SCAN_BATCH = 4
VMEM_LIMIT = 48 * 1024 * 1024

_NN = (((1,), (0,)), ((), ()))
_NT = (((1,), (1,)), ((), ()))
_TN = (((0,), (0,)), ((), ()))
BF16 = jnp.bfloat16
LOG2E = 1.4426950408889634


def _split_bf16(x, terms=2):
    pieces = []
    for _ in range(terms - 1):
        hi = x.astype(BF16)
        pieces.append(hi)
        x = x - hi.astype(jnp.float32)
    pieces.append(x.astype(BF16))
    return pieces


def _select_rows(sel, x, terms):
    sel = sel.astype(BF16)
    acc = None
    for piece in _split_bf16(x, terms):
        t = jnp.dot(sel, piece, preferred_element_type=jnp.float32)
        acc = t if acc is None else acc + t
    return acc


def _select_cols(x, sel, terms):
    sel = sel.astype(BF16)
    acc = None
    for piece in _split_bf16(x, terms):
        t = jnp.dot(piece, sel, preferred_element_type=jnp.float32)
        acc = t if acc is None else acc + t
    return acc


def _dot3(a, b):
    dg = functools.partial(jnp.dot, preferred_element_type=jnp.float32)
    a_hi, a_lo = _split_bf16(a)
    b_hi, b_lo = _split_bf16(b)
    return dg(a_hi, b_hi) + (dg(a_hi, b_lo) + dg(a_lo, b_hi))


def _sigmoid(x):
    return 1.0 / (1.0 + jnp.exp(-x))


def _in_proj_kernel(x_ref, g_ref, w_ref, prw_ref, phg_ref):
    x = x_ref[...]
    ms = jnp.mean(x * x, axis=-1, keepdims=True)
    h = (x * lax.rsqrt(ms + NORM_EPS) * g_ref[...]).astype(BF16)
    p = jnp.dot(h, w_ref[...], preferred_element_type=jnp.float32)
    prw_ref[...] = p[:, :RW_COLS]
    phg_ref[...] = p[:, RW_COLS:]


def _in_proj(x2, g, w_bf16):
    rows = x2.shape[0]
    return pl.pallas_call(
        _in_proj_kernel,
        grid=(rows // ROW_TILE,),
        in_specs=[
            pl.BlockSpec((ROW_TILE, D_MODEL), lambda i: (i, 0)),
            pl.BlockSpec((1, D_MODEL), lambda i: (0, 0)),
            pl.BlockSpec((D_MODEL, IN_COLS), lambda i: (0, 0)),
        ],
        out_specs=[
            pl.BlockSpec((ROW_TILE, RW_COLS), lambda i: (i, 0)),
            pl.BlockSpec((ROW_TILE, HG_COLS), lambda i: (i, 0)),
        ],
        out_shape=[
            jax.ShapeDtypeStruct((rows, RW_COLS), jnp.float32),
            jax.ShapeDtypeStruct((rows, HG_COLS), jnp.float32),
        ],
        compiler_params=pltpu.CompilerParams(
            dimension_semantics=("arbitrary",), vmem_limit_bytes=VMEM_LIMIT),
        name="in_proj",
    )(x2, g, w_bf16)


def _rw_prep_kernel(pc_ref, pp_ref, pn_ref, mup_ref, mun_ref, w0_ref, w2_ref, a0_ref, a2_ref,
                    kk_ref, ka_ref, rk_ref, seg_ref, tri_ref, blk_ref,
                    ops_ref, v_ref, gend_ref, bonus_ref, gate_ref, *, n_tiles):
    j = pl.program_id(1)

    p = pc_ref[...]
    row = lax.broadcasted_iota(jnp.int32, (ROW_TILE, RW_COLS), 0)
    prev_row = jnp.where(j > 0, pp_ref[HALO - 1:HALO, :], 0.0)
    next_row = jnp.where(j < n_tiles - 1, pn_ref[0:1, :], 0.0)
    p_prev = jnp.where(row == 0, prev_row, pltpu.roll(p, 1, 0))
    p_next = jnp.where(row == ROW_TILE - 1, next_row, pltpu.roll(p, ROW_TILE - 1, 0))
    z = p + mup_ref[...] * (p_prev - p) + mun_ref[...] * (p_next - p)

    W = RW_WIDTH
    r, k, v, g = z[:, 0:W], z[:, W:2 * W], z[:, 2 * W:3 * W], z[:, 3 * W:4 * W]
    wd = z[:, 4 * W:4 * W + 2 * LORA]
    ad = z[:, 4 * W + 2 * LORA:4 * W + 4 * LORA]
    gate_ref[...] = g * _sigmoid(g)
    v_ref[...] = v.astype(BF16)

    seg = seg_ref[...]
    blk = blk_ref[...]
    kk = k * kk_ref[...]
    kk = kk * lax.rsqrt(jnp.maximum(_select_cols(kk * kk, seg, 2), 1e-24))
    tanh_wd = jnp.tanh(wd)

    k_sum = None
    for d in range(N_DIR):
        w_raw = w0_ref[d:d + 1, :] + _dot3(tanh_wd, w2_ref[d])
        lw = -jnp.exp(jnp.float32(-0.5)) * _sigmoid(w_raw)
        a = _sigmoid(a0_ref[d:d + 1, :] + _dot3(ad, a2_ref[d]))
        k_d = k * (1.0 + (a - 1.0) * ka_ref[...])
        b_d = kk * a
        k_sum = k_d if k_sum is None else k_sum + k_d
        cl = _select_rows(tri_ref[d], lw, 3)
        total = _select_rows(blk, lw, 3)
        e_out = jnp.exp(-cl)
        e_end = jnp.exp(total - cl)
        ops = (r * jnp.exp(cl),
               kk * jnp.exp(cl - lw),
               k_d * e_out, b_d * e_out,
               k_d * e_end, b_d * e_end)
        for i, op in enumerate(ops):
            ops_ref[d, :, i * W:(i + 1) * W] = op.astype(BF16)
        for q in range(ROW_TILE // CHUNK):
            gend_ref[d, q] = jnp.exp(total[q * CHUNK:q * CHUNK + HALO, :])

    bonus_ref[...] = _select_cols(r * k_sum * rk_ref[...], seg, 2) * v


def _rw_prep(p_rw, mup, mun, w0, w2p, a0, a2p, k_k, k_a, r_k, seg, tri, blk):
    bsz, seq, _ = p_rw.shape
    n_tiles = seq // ROW_TILE
    hb = ROW_TILE // HALO
    n_halo = seq // HALO
    n_chunks = seq // CHUNK
    cpt = ROW_TILE // CHUNK
    full = lambda a: pl.BlockSpec(a.shape, lambda b, j: (0,) * a.ndim)
    tile = lambda n: pl.BlockSpec((None, ROW_TILE, n), lambda b, j: (b, j, 0))
    return pl.pallas_call(
        functools.partial(_rw_prep_kernel, n_tiles=n_tiles),
        grid=(bsz, n_tiles),
        in_specs=[
            tile(RW_COLS),
            pl.BlockSpec((None, HALO, RW_COLS), lambda b, j: (b, jnp.maximum(j * hb - 1, 0), 0)),
            pl.BlockSpec((None, HALO, RW_COLS), lambda b, j: (b, jnp.minimum((j + 1) * hb, n_halo - 1), 0)),
            full(mup), full(mun), full(w0), full(w2p), full(a0), full(a2p),
            full(k_k), full(k_a), full(r_k), full(seg), full(tri), full(blk),
        ],
        out_specs=[
            pl.BlockSpec((N_DIR, None, ROW_TILE, RW_OPS * RW_WIDTH), lambda b, j: (0, b, j, 0)),
            tile(RW_WIDTH),
            pl.BlockSpec((N_DIR, None, cpt, HALO, RW_WIDTH), lambda b, j: (0, b, j, 0, 0)),
            tile(RW_WIDTH),
            tile(RW_WIDTH),
        ],
        out_shape=[
            jax.ShapeDtypeStruct((N_DIR, bsz, seq, RW_OPS * RW_WIDTH), BF16),
            jax.ShapeDtypeStruct((bsz, seq, RW_WIDTH), BF16),
            jax.ShapeDtypeStruct((N_DIR, bsz, n_chunks, HALO, RW_WIDTH), jnp.float32),
            jax.ShapeDtypeStruct((bsz, seq, RW_WIDTH), jnp.float32),
            jax.ShapeDtypeStruct((bsz, seq, RW_WIDTH), jnp.float32),
        ],
        compiler_params=pltpu.CompilerParams(
            dimension_semantics=("arbitrary", "arbitrary"), vmem_limit_bytes=VMEM_LIMIT),
        name="rwkv7_prep",
    )(p_rw, p_rw, p_rw, mup, mun, w0, w2p, a0, a2p, k_k, k_a, r_k, seg, tri, blk)


def _rw_scan_kernel(of_ref, ob_ref, vf_ref, vb_ref, gf_ref, gb_ref, yf_ref, yb_ref, s_ref):
    c = pl.program_id(1)

    @pl.when(c == 0)
    def _():
        s_ref[...] = jnp.zeros_like(s_ref)

    N, W = RW_HEAD_DIM, RW_WIDTH
    lane = lax.broadcasted_iota(jnp.int32, (CHUNK, LANES), 1)
    tt = lax.broadcasted_iota(jnp.int32, (CHUNK, LANES), 0)
    half = (lane < N, lane >= N)
    diag = ((lax.broadcasted_iota(jnp.int32, (LANES, LANES), 0) < N)
            == (lax.broadcasted_iota(jnp.int32, (LANES, LANES), 1) < N))
    dir_refs = ((of_ref, vf_ref, gf_ref, yf_ref), (ob_ref, vb_ref, gb_ref, yb_ref))

    pairs = [(q, d, j) for q in range(SCAN_BATCH) for d in range(N_DIR) for j in range(W // LANES)]
    probs = [(pi, par) for pi in range(len(pairs)) for par in range(2)]

    def op(pi, i):
        q, d, j = pairs[pi]
        return dir_refs[d][0][q, :, i * W + j * LANES:i * W + (j + 1) * LANES]

    r_p = [op(pi, 0) for pi in range(len(pairs))]
    a_p = [op(pi, 1) for pi in range(len(pairs))]
    k_p = [op(pi, 2) for pi in range(len(pairs))]
    b_p = [op(pi, 3) for pi in range(len(pairs))]
    ke_p = [op(pi, 4) for pi in range(len(pairs))]
    be_p = [op(pi, 5) for pi in range(len(pairs))]
    v_p = [dir_refs[d][1][q, :, j * LANES:(j + 1) * LANES] for q, d, j in pairs]
    s_p = [s_ref[q, d, j] for q, d, j in pairs]

    def lag(pi):
        d = pairs[pi][1]
        delta = tt - (lane & (N - 1))
        return delta if d == 0 else -delta

    gram = []
    for pi, par in probs:
        lhs = jnp.concatenate([a_p[pi], r_p[pi]], axis=0)
        lhs = jnp.where(jnp.concatenate([half[par], half[par]], axis=0), lhs, jnp.zeros_like(lhs))
        rhs = jnp.concatenate([k_p[pi], b_p[pi]] if par == 0 else [b_p[pi], k_p[pi]], axis=0)
        gram.append(lax.dot_general(lhs, rhs, _NT, preferred_element_type=jnp.float32))

    za, zb, bot = [], [], []
    for n, (pi, par) in enumerate(probs):
        top = gram[n][:CHUNK]
        lg = lag(pi)
        m0 = jnp.where(jnp.where(half[1 - par], lg, 0) > 0, -top, 0.0)
        l_ak = jnp.where(jnp.where(half[par], lg, 0) > 0, top, 0.0)
        bot.append(jnp.where(lg >= 0, gram[n][CHUNK:], 0.0).astype(BF16))
        vv = jnp.concatenate([v_p[pi], v_p[pi]], axis=0)
        zb.append(jnp.dot(l_ak.astype(BF16), vv, preferred_element_type=jnp.float32))
        za.append(jnp.where(half[par], a_p[pi].astype(jnp.float32), m0))

    for level in range(6):
        new_za, new_zb = [], []
        for n, (pi, par) in enumerate(probs):
            m = jnp.where(half[1 - par], za[n], 0.0).astype(BF16)
            z2 = jnp.concatenate([za[n], zb[n]], axis=1).astype(BF16)
            p = jnp.dot(m, jnp.concatenate([z2, z2], axis=0), preferred_element_type=jnp.float32)
            new_za.append(p[:, :LANES] + jnp.where(half[par], za[n], 0.0))
            new_zb.append(p[:, LANES:] + zb[n])
        za, zb = new_za, new_zb

    n_pairs = range(len(pairs))
    s_bf = [s_p[pi].astype(BF16) for pi in n_pairs]
    wr = [jnp.concatenate([jnp.where(half[0], za[2 * pi], za[2 * pi + 1]).astype(BF16), r_p[pi]], axis=0)
          for pi in n_pairs]
    ws = [lax.dot_general(wr[pi], s_bf[pi], _NT, preferred_element_type=jnp.float32) for pi in n_pairs]
    u_bf = [(-(ws[pi][:CHUNK] + jnp.where(half[0], zb[2 * pi], zb[2 * pi + 1]))).astype(BF16)
            for pi in n_pairs]
    y_e = [jnp.dot(bot[2 * pi], jnp.concatenate([v_p[pi], u_bf[pi]], axis=0),
                   preferred_element_type=jnp.float32) for pi in n_pairs]
    y_o = [jnp.dot(bot[2 * pi + 1], jnp.concatenate([u_bf[pi], v_p[pi]], axis=0),
                   preferred_element_type=jnp.float32) for pi in n_pairs]
    upd = [lax.dot_general(jnp.concatenate([v_p[pi], u_bf[pi]], axis=0),
                           jnp.concatenate([ke_p[pi], be_p[pi]], axis=0), _TN,
                           preferred_element_type=jnp.float32) for pi in n_pairs]
    for pi, (q, d, j) in enumerate(pairs):
        dir_refs[d][3][q, :, j * LANES:(j + 1) * LANES] = ws[pi][CHUNK:] + jnp.where(half[0], y_e[pi], y_o[pi])
        g_end = dir_refs[d][2][q, 0:1, j * LANES:(j + 1) * LANES]
        s_ref[q, d, j] = s_p[pi] * g_end + jnp.where(diag, upd[pi], 0.0)


def _rw_scan(ops, v_bf, gend):
    _, bsz, seq, _ = ops.shape
    n_chunks = seq // CHUNK
    rev = lambda c: n_chunks - 1 - c
    return pl.pallas_call(
        _rw_scan_kernel,
        grid=(bsz // SCAN_BATCH, n_chunks),
        in_specs=[
            pl.BlockSpec((None, SCAN_BATCH, CHUNK, RW_OPS * RW_WIDTH), lambda b, c: (0, b, c, 0)),
            pl.BlockSpec((None, SCAN_BATCH, CHUNK, RW_OPS * RW_WIDTH), lambda b, c: (1, b, rev(c), 0)),
            pl.BlockSpec((SCAN_BATCH, CHUNK, RW_WIDTH), lambda b, c: (b, c, 0)),
            pl.BlockSpec((SCAN_BATCH, CHUNK, RW_WIDTH), lambda b, c: (b, rev(c), 0)),
            pl.BlockSpec((None, SCAN_BATCH, None, HALO, RW_WIDTH), lambda b, c: (0, b, c, 0, 0)),
            pl.BlockSpec((None, SCAN_BATCH, None, HALO, RW_WIDTH), lambda b, c: (1, b, rev(c), 0, 0)),
        ],
        out_specs=[
            pl.BlockSpec((SCAN_BATCH, CHUNK, RW_WIDTH), lambda b, c: (b, c, 0)),
            pl.BlockSpec((SCAN_BATCH, CHUNK, RW_WIDTH), lambda b, c: (b, rev(c), 0)),
        ],
        out_shape=[jax.ShapeDtypeStruct((bsz, seq, RW_WIDTH), jnp.float32)] * 2,
        scratch_shapes=[pltpu.VMEM((SCAN_BATCH, N_DIR, RW_WIDTH // LANES, LANES, LANES), jnp.float32)],
        compiler_params=pltpu.CompilerParams(
            dimension_semantics=("arbitrary", "arbitrary"), vmem_limit_bytes=VMEM_LIMIT),
        name="rwkv7_scan",
    )(ops, ops, v_bf, v_bf, gend, gend)


def _hg_prep_kernel(q_ref, ff_ref, fb_ref, i_ref, lbl_ref, tri_ref, blk_ref,
                    hq_ref, hk_ref, hv_ref, gend_ref, oi_ref, cs_ref, vs_ref, *, layer):
    lg = lbl_ref[...]
    e = jnp.exp(lg - jnp.max(lg, axis=0, keepdims=True))
    lb = jnp.sum(e[0:layer + 1, :], axis=0, keepdims=True) / jnp.sum(e, axis=0, keepdims=True)

    q = q_ref[...]
    v = i_ref[...]
    v_bf = v.astype(BF16)
    hv_ref[...] = v_bf
    blk = blk_ref[...]
    heads = range(HG_HEADS)
    sls = [slice(h * HG_DK, (h + 1) * HG_DK) for h in heads]
    n_chunks = HG_TILE // CHUNK
    trow =lax.broadcasted_iota(jnp.int32, (SUB, HG_DK), 0)
    for h in heads:
        vs_ref[h] = v[:, sls[h]]
    oi_ref[...] = jnp.zeros_like(oi_ref)

    for d, f_ref in enumerate((ff_ref, fb_ref)):
        fx = f_ref[...]
        f = lb + (1.0 - lb) * _sigmoid(fx)
        kf = (1.0 - lb) * _sigmoid(-fx)
        gl = jnp.log(f)
        b = _select_rows(tri_ref[d], gl, 3)
        total = _select_rows(blk, gl, 3)
        hq_ref[d] = (q * jnp.exp(b)).astype(BF16)
        hk_ref[d] = (kf * jnp.exp(total - b)).astype(BF16)
        for c in range(n_chunks):
            gend_ref[d, c] = jnp.exp(total[c * CHUNK:c * CHUNK + HALO, :])
        b2 = b * LOG2E
        cs = b2 - jnp.log2(kf)
        for h in heads:
            cs_ref[d, h] = cs[:, sls[h]]

        def rows(c, p_lo, p_hi):
            lo = c * CHUNK + (p_lo if d == 0 else CHUNK - p_hi)
            return slice(lo, lo + (p_hi - p_lo))

        pending = []
        for c in range(n_chunks):
            size = CHUNK // 2
            while size >= SUB:
                for p in range(0, CHUNK, 2 * size):
                    s_sl = rows(c, p, p + size)
                    t_sl = rows(c, p + size, p + 2 * size)
                    ref = b2[rows(c, p + size - 1, p + size)]
                    q_t = (q[t_sl] * jnp.exp2(b2[t_sl] - ref)).astype(BF16)
                    k_s = (kf[s_sl] * jnp.exp2(ref - b2[s_sl])).astype(BF16)
                    scores = [lax.dot_general(q_t[:, sls[h]], k_s[:, sls[h]], _NT,
                                              preferred_element_type=jnp.float32).astype(BF16) for h in heads]
                    pending.append((t_sl, s_sl, scores))
                size //= 2

        for lo in range(0, HG_TILE, SUB):
            q_i, b_i = q[lo:lo + SUB], b2[lo:lo + SUB]
            acc = [oi_ref[lo:lo + SUB, sls[h]] for h in heads]
            for s in range(SUB):
                row_s = pl.ds(lo + s, SUB, stride=0)
                m = (trow >= s) if d == 0 else (trow <= s)
                for h in heads:
                    xs = jnp.where(m, q_i[:, sls[h]] * jnp.exp2(b_i[:, sls[h]] - cs_ref[d, h, row_s, :]), 0.0)
                    w = jnp.sum(xs, axis=-1, keepdims=True)
                    acc[h] = acc[h] + w * vs_ref[h, row_s, :]
            for h in heads:
                oi_ref[lo:lo + SUB, sls[h]] = acc[h]

        for t_sl, s_sl, scores in pending:
            for h in heads:
                oi_ref[t_sl, sls[h]] += jnp.dot(scores[h], v_bf[s_sl, sls[h]], preferred_element_type=jnp.float32)


def _hg_prep(p_hg, lb_logits, layer, tri, blk):
    bsz, seq, _ = p_hg.shape
    n_chunks = seq // CHUNK
    cpt = HG_TILE // CHUNK
    col = lambda j: pl.BlockSpec((None, HG_TILE, HG_WIDTH), lambda b, t: (b, t, j))
    full = lambda a: pl.BlockSpec(a.shape, lambda b, t: (0,) * a.ndim)
    dir_tile = pl.BlockSpec((N_DIR, None, HG_TILE, HG_WIDTH), lambda b, t: (0, b, t, 0))
    dir_sds = jax.ShapeDtypeStruct((N_DIR, bsz, seq, HG_WIDTH), BF16)
    return pl.pallas_call(
        functools.partial(_hg_prep_kernel, layer=layer),
        grid=(bsz, seq // HG_TILE),
        in_specs=[col(0), col(1), col(2), col(3), full(lb_logits), full(tri), full(blk)],
        out_specs=[
            dir_tile, dir_tile, col(0),
            pl.BlockSpec((N_DIR, None, cpt, HALO, HG_WIDTH), lambda b, t: (0, b, t, 0, 0)),
            col(0),
        ],
        out_shape=[
            dir_sds, dir_sds,
            jax.ShapeDtypeStruct((bsz, seq, HG_WIDTH), BF16),
            jax.ShapeDtypeStruct((N_DIR, bsz, n_chunks, HALO, HG_WIDTH), jnp.float32),
            jax.ShapeDtypeStruct((bsz, seq, HG_WIDTH), jnp.float32),
        ],
        scratch_shapes=[pltpu.VMEM((N_DIR, HG_HEADS, HG_TILE, HG_DK), jnp.float32),
                        pltpu.VMEM((HG_HEADS, HG_TILE, HG_DK), jnp.float32)],
        compiler_params=pltpu.CompilerParams(
            dimension_semantics=("arbitrary", "arbitrary"), vmem_limit_bytes=VMEM_LIMIT),
        name="hgrn2_prep",
    )(p_hg, p_hg, p_hg, p_hg, lb_logits, tri, blk)


def _hg_scan_kernel(qf_ref, qb_ref, kf_ref, kb_ref, vf_ref, vb_ref, gf_ref, gb_ref, of_ref, ob_ref, s_ref):
    c = pl.program_id(1)

    @pl.when(c == 0)
    def _():
        s_ref[...] = jnp.zeros_like(s_ref)

    dir_refs = ((qf_ref, kf_ref, vf_ref, gf_ref, of_ref), (qb_ref, kb_ref, vb_ref, gb_ref, ob_ref))
    probs = [(n, d, h) for n in range(SCAN_BATCH) for d in range(N_DIR) for h in range(HG_HEADS)]
    sl = lambda h: slice(h * HG_DK, (h + 1) * HG_DK)
    st = [s_ref[n, d, h] for n, d, h in probs]
    o = [lax.dot_general(dir_refs[d][0][n, :, sl(h)], st[i].astype(BF16), _NT,
                         preferred_element_type=jnp.float32) for i, (n, d, h) in enumerate(probs)]
    upd = [lax.dot_general(dir_refs[d][2][n, :, sl(h)], dir_refs[d][1][n, :, sl(h)], _TN,
                           preferred_element_type=jnp.float32) for n, d, h in probs]
    for i, (n, d, h) in enumerate(probs):
        dir_refs[d][4][n, :, sl(h)] = o[i]
        s_ref[n, d, h] = st[i] * dir_refs[d][3][n, 0:1, sl(h)] + upd[i]


def _hg_scan(hq, hk, hv, gend):
    _, bsz, seq, _ = hq.shape
    n_chunks = seq // CHUNK
    rev = lambda c: n_chunks - 1 - c
    fwd = lambda b, c: (0, b, c, 0)
    bwd = lambda b, c: (1, b, rev(c), 0)
    dir_blk = (None, SCAN_BATCH, CHUNK, HG_WIDTH)
    seq_blk = (SCAN_BATCH, CHUNK, HG_WIDTH)
    g_blk = (None, SCAN_BATCH, None, HALO, HG_WIDTH)
    return pl.pallas_call(
        _hg_scan_kernel,
        grid=(bsz // SCAN_BATCH, n_chunks),
        in_specs=[
            pl.BlockSpec(dir_blk, fwd), pl.BlockSpec(dir_blk, bwd),
            pl.BlockSpec(dir_blk, fwd), pl.BlockSpec(dir_blk, bwd),
            pl.BlockSpec(seq_blk, lambda b, c: (b, c, 0)), pl.BlockSpec(seq_blk, lambda b, c: (b, rev(c), 0)),
            pl.BlockSpec(g_blk, lambda b, c: (0, b, c, 0, 0)), pl.BlockSpec(g_blk, lambda b, c: (1, b, rev(c), 0, 0)),
        ],
        out_specs=[
            pl.BlockSpec(seq_blk, lambda b, c: (b, c, 0)),
            pl.BlockSpec(seq_blk, lambda b, c: (b, rev(c), 0)),
        ],
        out_shape=[jax.ShapeDtypeStruct((bsz, seq, HG_WIDTH), jnp.float32)] * 2,
        scratch_shapes=[pltpu.VMEM((SCAN_BATCH, N_DIR, HG_HEADS, HG_DK, HG_DK), jnp.float32)],
        compiler_params=pltpu.CompilerParams(
            dimension_semantics=("arbitrary", "arbitrary"), vmem_limit_bytes=VMEM_LIMIT),
        name="hgrn2_scan",
    )(hq, hq, hk, hk, hv, hv, gend, gend)


def _out_kernel(x_ref, yf_ref, yb_ref, bonus_ref, gate_ref, oi_ref, of_ref, ob_ref, g_ref,
                lnw_ref, lnb_ref, hgn_ref, seg_ref, w_ref, post_ref, out_ref):
    y = yf_ref[...] + yb_ref[...]
    seg = seg_ref[...]
    mu = _select_cols(y, seg, 3) * (1.0 / RW_HEAD_DIM)
    yc = y - mu
    var = _select_cols(yc * yc, seg, 2) * (1.0 / RW_HEAD_DIM)
    y = yc * lax.rsqrt(var + GN_EPS) * lnw_ref[...] + lnb_ref[...]
    rw = (y + bonus_ref[...]) * gate_ref[...]

    o = oi_ref[...] + of_ref[...] + ob_ref[...]
    g = g_ref[...]
    parts = []
    for h in range(HG_HEADS):
        oh = o[:, h * HG_DK:(h + 1) * HG_DK]
        parts.append(oh * lax.rsqrt(jnp.mean(oh * oh, axis=-1, keepdims=True) + NORM_EPS))
    hg = jnp.concatenate(parts, axis=1) * hgn_ref[...] * (g * _sigmoid(g))

    cat = jnp.concatenate([rw, hg], axis=1).astype(BF16)
    yo = jnp.dot(cat, w_ref[...], preferred_element_type=jnp.float32)
    ms = jnp.mean(yo * yo, axis=-1, keepdims=True)
    out_ref[...] = x_ref[...] + yo * lax.rsqrt(ms + NORM_EPS) * post_ref[...]


def _out_proj(x2, y_f, y_b, bonus, gate, o_i, o_f, o_b, p_hg2, ln_w, ln_b, hg_norm, seg, w_bf16, post_g):
    rows = x2.shape[0]
    tile = lambda n: pl.BlockSpec((ROW_TILE, n), lambda i: (i, 0))
    row_spec = lambda n: pl.BlockSpec((1, n), lambda i: (0, 0))
    return pl.pallas_call(
        _out_kernel,
        grid=(rows // ROW_TILE,),
        in_specs=[
            tile(D_MODEL),
            tile(RW_WIDTH), tile(RW_WIDTH), tile(RW_WIDTH), tile(RW_WIDTH),
            tile(HG_WIDTH), tile(HG_WIDTH), tile(HG_WIDTH),
            pl.BlockSpec((ROW_TILE, HG_WIDTH), lambda i: (i, 4)),
            row_spec(RW_WIDTH), row_spec(RW_WIDTH), row_spec(HG_WIDTH),
            pl.BlockSpec((RW_WIDTH, RW_WIDTH), lambda i: (0, 0)),
            pl.BlockSpec((D_MODEL, D_MODEL), lambda i: (0, 0)),
            row_spec(D_MODEL),
        ],
        out_specs=tile(D_MODEL),
        out_shape=jax.ShapeDtypeStruct((rows, D_MODEL), jnp.float32),
        compiler_params=pltpu.CompilerParams(
            dimension_semantics=("arbitrary",), vmem_limit_bytes=VMEM_LIMIT),
        name="out_proj",
    )(x2, y_f, y_b, bonus, gate, o_i, o_f, o_b, p_hg2, ln_w, ln_b, hg_norm, seg, w_bf16, post_g)


def _pad_lora(w):
    z = jnp.zeros_like(w[0])
    return jnp.stack([jnp.concatenate([w[0], z], axis=0), jnp.concatenate([z, w[1]], axis=0)])


def _chunk_selectors(tile):
    t = jnp.arange(tile)
    same = (t[:, None] // CHUNK) == (t[None, :] // CHUNK)
    fwd = same & (t[None, :] <= t[:, None])
    bwd = same & (t[None, :] >= t[:, None])
    return jnp.stack([fwd, bwd]).astype(BF16), same.astype(BF16)


def kernel(x, pre_norm_g, w_in, rw_shift_prev, rw_shift_next, rw_w0, rw_w2, rw_a0, rw_a2, rw_k_k, rw_k_a,
           rw_r_k, rw_ln_w, rw_ln_b, hg_lb_logits, hg_norm_g, w_out, post_norm_g):
    bsz, seq, dm = x.shape
    depth = w_in.shape[0]
    rows = bsz * seq
    lane = jnp.arange(RW_WIDTH) // RW_HEAD_DIM
    seg = (lane[:, None] == lane[None, :]).astype(BF16)
    tri, blk = _chunk_selectors(ROW_TILE)
    hg_tri, hg_blk = _chunk_selectors(HG_TILE)
    row = lambda t: t.reshape(1, -1)
    flat = lambda t: t.reshape(rows, t.shape[-1])
    for l in range(depth):
        x2 = x.reshape(rows, dm)
        p_rw, p_hg = _in_proj(x2, row(pre_norm_g[l]), w_in[l].astype(BF16))
        ops, v_bf, gend, bonus, gate = _rw_prep(
            p_rw.reshape(bsz, seq, RW_COLS), row(rw_shift_prev[l]), row(rw_shift_next[l]),
            rw_w0[l], _pad_lora(rw_w2[l]), rw_a0[l], _pad_lora(rw_a2[l]),
            row(rw_k_k[l]), row(rw_k_a[l]), row(rw_r_k[l]), seg, tri, blk)
        y_f, y_b = _rw_scan(ops, v_bf, gend)
        hq, hk, hv, hg_gend, o_i = _hg_prep(p_hg.reshape(bsz, seq, HG_COLS), hg_lb_logits, l, hg_tri, hg_blk)
        o_f, o_b = _hg_scan(hq, hk, hv, hg_gend)
        out = _out_proj(
            x2, flat(y_f), flat(y_b), flat(bonus), flat(gate), flat(o_i), flat(o_f), flat(o_b), p_hg,
            row(rw_ln_w[l]), row(rw_ln_b[l]), row(jnp.tile(hg_norm_g[l], HG_HEADS)), seg,
            w_out[l].astype(BF16), row(post_norm_g[l]))
        x = out.reshape(bsz, seq, dm)
    return x
```

```python
import functools

import jax
import jax.numpy as jnp
from jax import lax
from jax.experimental import pallas as pl
from jax.experimental.pallas import tpu as pltpu

D_MODEL = 1024
RW_HEAD_DIM = 64
RW_WIDTH = 512
RW_HEADS = 8
HG_DK = 128
HG_WIDTH = 512
HG_HEADS = 4
LORA = 64
NORM_EPS = 1e-6
GN_EPS = 64e-5
RW_COLS = 4 * RW_WIDTH + 4 * LORA
HG_COLS = 5 * HG_WIDTH
IN_COLS = RW_COLS + HG_COLS
N_DIR = 2
RW_OPS = 6

LANES = 128
CHUNK = 64
SUB = 8
ROW_TILE = 256
HG_TILE = 128
HALO = 8
SCAN_BATCH = 4
VMEM_LIMIT = 48 * 1024 * 1024

_NN = (((1,), (0,)), ((), ()))
_NT = (((1,), (1,)), ((), ()))
_TN = (((0,), (0,)), ((), ()))
BF16 = jnp.bfloat16
LOG2E = 1.4426950408889634


def _split_bf16(x, terms=2):
    pieces = []
    for _ in range(terms - 1):
        hi = x.astype(BF16)
        pieces.append(hi)
        x = x - hi.astype(jnp.float32)
    pieces.append(x.astype(BF16))
    return pieces


def _select_rows(sel, x, terms):
    acc = None
    for piece in _split_bf16(x, terms):
        t = jnp.dot(sel, piece, preferred_element_type=jnp.float32)
        acc = t if acc is None else acc + t
    return acc


def _select_cols(x, sel, terms):
    acc = None
    for piece in _split_bf16(x, terms):
        t = jnp.dot(piece, sel, preferred_element_type=jnp.float32)
        acc = t if acc is None else acc + t
    return acc


def _dot3(a, b_hi, b_lo):
    dg = functools.partial(jnp.dot, preferred_element_type=jnp.float32)
    a_hi, a_lo = _split_bf16(a)
    return dg(a_hi, b_hi) + (dg(a_hi, b_lo) + dg(a_lo, b_hi))


def _sigmoid(x):
    return 1.0 / (1.0 + jnp.exp(-x))


def _chunk_totals(cum, d):
    last = CHUNK - 1 if d == 0 else 0
    return [cum[c * CHUNK + last:c * CHUNK + last + 1] for c in range(cum.shape[0] // CHUNK)]


def _expand_chunks(rows):
    return jnp.concatenate([jnp.broadcast_to(r, (CHUNK, r.shape[1])) for r in rows], axis=0)


def _in_proj_kernel(x_ref, g_ref, w_ref, prw_ref, phg_ref):
    x = x_ref[...]
    ms = jnp.mean(x * x, axis=-1, keepdims=True)
    h = (x * lax.rsqrt(ms + NORM_EPS) * g_ref[...]).astype(BF16)
    p = jnp.dot(h, w_ref[...], preferred_element_type=jnp.float32)
    prw_ref[...] = p[:, :RW_COLS]
    phg_ref[...] = p[:, RW_COLS:]


def _in_proj(x2, g, w_bf16):
    rows = x2.shape[0]
    return pl.pallas_call(
        _in_proj_kernel,
        grid=(rows // ROW_TILE,),
        in_specs=[
            pl.BlockSpec((ROW_TILE, D_MODEL), lambda i: (i, 0)),
            pl.BlockSpec((1, D_MODEL), lambda i: (0, 0)),
            pl.BlockSpec((D_MODEL, IN_COLS), lambda i: (0, 0)),
        ],
        out_specs=[
            pl.BlockSpec((ROW_TILE, RW_COLS), lambda i: (i, 0)),
            pl.BlockSpec((ROW_TILE, HG_COLS), lambda i: (i, 0)),
        ],
        out_shape=[
            jax.ShapeDtypeStruct((rows, RW_COLS), jnp.float32),
            jax.ShapeDtypeStruct((rows, HG_COLS), jnp.float32),
        ],
        compiler_params=pltpu.CompilerParams(
            dimension_semantics=("arbitrary",), vmem_limit_bytes=VMEM_LIMIT),
        name="in_proj",
    )(x2, g, w_bf16)


def _rw_prep_kernel(pc_ref, pp_ref, pn_ref, mup_ref, mun_ref, w0_ref, w2h_ref, w2l_ref, a0_ref, a2h_ref, a2l_ref,
                    kk_ref, ka_ref, rk_ref, seg_ref, tri_ref,
                    ops_ref, v_ref, gend_ref, bonus_ref, gate_ref, *, n_tiles):
    j = pl.program_id(1)

    p = pc_ref[...]
    row = lax.broadcasted_iota(jnp.int32, (HALO, RW_COLS), 0)
    prev_row = jnp.where(j > 0, pp_ref[HALO - 1:HALO, :], 0.0)
    next_row = jnp.where(j < n_tiles - 1, pn_ref[0:1, :], 0.0)
    p_prev = pltpu.roll(p, 1, 0)
    p_next = pltpu.roll(p, ROW_TILE - 1, 0)
    p_prev = jnp.concatenate([jnp.where(row == 0, prev_row, p_prev[:HALO]), p_prev[HALO:]], axis=0)
    p_next = jnp.concatenate([p_next[:-HALO], jnp.where(row == HALO - 1, next_row, p_next[-HALO:])], axis=0)
    mup, mun = mup_ref[...], mun_ref[...]
    z = p * (1.0 - mup - mun) + mup * p_prev + mun * p_next

    W = RW_WIDTH
    r, k, v, g = z[:, 0:W], z[:, W:2 * W], z[:, 2 * W:3 * W], z[:, 3 * W:4 * W]
    wd = z[:, 4 * W:4 * W + 2 * LORA]
    ad = z[:, 4 * W + 2 * LORA:4 * W + 4 * LORA]
    gate_ref[...] = g * _sigmoid(g)
    v_ref[...] = v.astype(BF16)

    seg = seg_ref[...]
    kk = k * kk_ref[...]
    kk = kk * lax.rsqrt(jnp.maximum(_select_cols(kk * kk, seg, 1), 1e-24))
    tanh_wd = jnp.tanh(wd)

    k_sum = None
    for d in range(N_DIR):
        w_raw = w0_ref[d:d + 1, :] + _dot3(tanh_wd, w2h_ref[d], w2l_ref[d])
        lw = -jnp.exp(jnp.float32(-0.5)) * _sigmoid(w_raw)
        a = _sigmoid(a0_ref[d:d + 1, :] + _dot3(ad, a2h_ref[d], a2l_ref[d]))
        k_d = k * (1.0 + (a - 1.0) * ka_ref[...])
        b_d = kk * a
        k_sum = k_d if k_sum is None else k_sum + k_d
        cl = _select_rows(tri_ref[d], lw, 3)
        totals = _chunk_totals(cl, d)
        e_out = jnp.exp(-cl)
        e_end = jnp.exp(_expand_chunks(totals) - cl)
        ops = (r * jnp.exp(cl),
               kk * jnp.exp(cl - lw),
               k_d * e_out, b_d * e_out,
               k_d * e_end, b_d * e_end)
        for i, op in enumerate(ops):
            ops_ref[d, :, i * W:(i + 1) * W] = op.astype(BF16)
        for q, tot in enumerate(totals):
            gend_ref[d, q] = jnp.broadcast_to(jnp.exp(tot), (HALO, W))

    bonus_ref[...] = _select_cols(r * k_sum * rk_ref[...], seg, 1) * v


def _rw_prep(p_rw, mup, mun, w0, w2, a0, a2, k_k, k_a, r_k, seg, tri):
    bsz, seq, _ = p_rw.shape
    n_tiles = seq // ROW_TILE
    hb = ROW_TILE // HALO
    n_halo = seq // HALO
    n_chunks = seq // CHUNK
    cpt = ROW_TILE // CHUNK
    full = lambda a: pl.BlockSpec(a.shape, lambda b, j: (0,) * a.ndim)
    tile = lambda n: pl.BlockSpec((None, ROW_TILE, n), lambda b, j: (b, j, 0))
    consts = (mup, mun, w0, *w2, a0, *a2, k_k, k_a, r_k, seg, tri)
    return pl.pallas_call(
        functools.partial(_rw_prep_kernel, n_tiles=n_tiles),
        grid=(bsz, n_tiles),
        in_specs=[
            tile(RW_COLS),
            pl.BlockSpec((None, HALO, RW_COLS), lambda b, j: (b, jnp.maximum(j * hb - 1, 0), 0)),
            pl.BlockSpec((None, HALO, RW_COLS), lambda b, j: (b, jnp.minimum((j + 1) * hb, n_halo - 1), 0)),
            *[full(a) for a in consts],
        ],
        out_specs=[
            pl.BlockSpec((N_DIR, None, ROW_TILE, RW_OPS * RW_WIDTH), lambda b, j: (0, b, j, 0)),
            tile(RW_WIDTH),
            pl.BlockSpec((N_DIR, None, cpt, HALO, RW_WIDTH), lambda b, j: (0, b, j, 0, 0)),
            tile(RW_WIDTH),
            tile(RW_WIDTH),
        ],
        out_shape=[
            jax.ShapeDtypeStruct((N_DIR, bsz, seq, RW_OPS * RW_WIDTH), BF16),
            jax.ShapeDtypeStruct((bsz, seq, RW_WIDTH), BF16),
            jax.ShapeDtypeStruct((N_DIR, bsz, n_chunks, HALO, RW_WIDTH), jnp.float32),
            jax.ShapeDtypeStruct((bsz, seq, RW_WIDTH), jnp.float32),
            jax.ShapeDtypeStruct((bsz, seq, RW_WIDTH), jnp.float32),
        ],
        compiler_params=pltpu.CompilerParams(
            dimension_semantics=("arbitrary", "arbitrary"), vmem_limit_bytes=VMEM_LIMIT),
        name="rwkv7_prep",
    )(p_rw, p_rw, p_rw, *consts)


def _rw_scan_kernel(of_ref, ob_ref, vf_ref, vb_ref, gf_ref, gb_ref, yf_ref, yb_ref, s_ref):
    c = pl.program_id(1)

    @pl.when(c == 0)
    def _():
        s_ref[...] = jnp.zeros_like(s_ref)

    N, W = RW_HEAD_DIM, RW_WIDTH
    lane = lax.broadcasted_iota(jnp.int32, (CHUNK, LANES), 1)
    tt = lax.broadcasted_iota(jnp.int32, (CHUNK, LANES), 0)
    half = (lane < N, lane >= N)
    diag = ((lax.broadcasted_iota(jnp.int32, (LANES, LANES), 0) < N)
            == (lax.broadcasted_iota(jnp.int32, (LANES, LANES), 1) < N))
    dir_refs = ((of_ref, vf_ref, gf_ref, yf_ref), (ob_ref, vb_ref, gb_ref, yb_ref))

    pairs = [(q, d, j) for q in range(SCAN_BATCH) for d in range(N_DIR) for j in range(W // LANES)]
    probs = [(pi, par) for pi in range(len(pairs)) for par in range(2)]

    def op(pi, i):
        q, d, j = pairs[pi]
        return dir_refs[d][0][q, :, i * W + j * LANES:i * W + (j + 1) * LANES]

    r_p = [op(pi, 0) for pi in range(len(pairs))]
    a_p = [op(pi, 1) for pi in range(len(pairs))]
    k_p = [op(pi, 2) for pi in range(len(pairs))]
    b_p = [op(pi, 3) for pi in range(len(pairs))]
    ke_p = [op(pi, 4) for pi in range(len(pairs))]
    be_p = [op(pi, 5) for pi in range(len(pairs))]
    v_p = [dir_refs[d][1][q, :, j * LANES:(j + 1) * LANES] for q, d, j in pairs]
    s_p = [s_ref[q, d, j] for q, d, j in pairs]

    def lag(pi):
        d = pairs[pi][1]
        delta = tt - (lane & (N - 1))
        return delta if d == 0 else -delta

    gram = []
    for pi, par in probs:
        lhs = jnp.concatenate([a_p[pi], r_p[pi]], axis=0)
        lhs = jnp.where(jnp.concatenate([half[par], half[par]], axis=0), lhs, jnp.zeros_like(lhs))
        rhs = jnp.concatenate([k_p[pi], b_p[pi]] if par == 0 else [b_p[pi], k_p[pi]], axis=0)
        gram.append(lax.dot_general(lhs, rhs, _NT, preferred_element_type=jnp.float32))

    za, zb, bot = [], [], []
    for n, (pi, par) in enumerate(probs):
        top = gram[n][:CHUNK]
        lg = lag(pi)
        m0 = jnp.where(jnp.where(half[1 - par], lg, 0) > 0, -top, 0.0)
        l_ak = jnp.where(jnp.where(half[par], lg, 0) > 0, top, 0.0)
        bot.append(jnp.where(lg >= 0, gram[n][CHUNK:], 0.0).astype(BF16))
        vv = jnp.concatenate([v_p[pi], v_p[pi]], axis=0)
        zb.append(jnp.dot(l_ak.astype(BF16), vv, preferred_element_type=jnp.float32))
        za.append(jnp.where(half[par], a_p[pi].astype(jnp.float32), m0))

    for level in range(6):
        new_za, new_zb = [], []
        for n, (pi, par) in enumerate(probs):
            m = jnp.where(half[1 - par], za[n], 0.0).astype(BF16)
            z2 = jnp.concatenate([za[n], zb[n]], axis=1).astype(BF16)
            p = jnp.dot(m, jnp.concatenate([z2, z2], axis=0), preferred_element_type=jnp.float32)
            new_za.append(p[:, :LANES] + jnp.where(half[par], za[n], 0.0))
            new_zb.append(p[:, LANES:] + zb[n])
        za, zb = new_za, new_zb

    n_pairs = range(len(pairs))
    s_bf = [s_p[pi].astype(BF16) for pi in n_pairs]
    wr = [jnp.concatenate([jnp.where(half[0], za[2 * pi], za[2 * pi + 1]).astype(BF16), r_p[pi]], axis=0)
          for pi in n_pairs]
    ws = [lax.dot_general(wr[pi], s_bf[pi], _NT, preferred_element_type=jnp.float32) for pi in n_pairs]
    u_bf = [(-(ws[pi][:CHUNK] + jnp.where(half[0], zb[2 * pi], zb[2 * pi + 1]))).astype(BF16)
            for pi in n_pairs]
    y_e = [jnp.dot(bot[2 * pi], jnp.concatenate([v_p[pi], u_bf[pi]], axis=0),
                   preferred_element_type=jnp.float32) for pi in n_pairs]
    y_o = [jnp.dot(bot[2 * pi + 1], jnp.concatenate([u_bf[pi], v_p[pi]], axis=0),
                   preferred_element_type=jnp.float32) for pi in n_pairs]
    upd = [lax.dot_general(jnp.concatenate([v_p[pi], u_bf[pi]], axis=0),
                           jnp.concatenate([ke_p[pi], be_p[pi]], axis=0), _TN,
                           preferred_element_type=jnp.float32) for pi in n_pairs]
    for pi, (q, d, j) in enumerate(pairs):
        dir_refs[d][3][q, :, j * LANES:(j + 1) * LANES] = ws[pi][CHUNK:] + jnp.where(half[0], y_e[pi], y_o[pi])
        g_end = dir_refs[d][2][q, 0:1, j * LANES:(j + 1) * LANES]
        s_ref[q, d, j] = s_p[pi] * g_end + jnp.where(diag, upd[pi], 0.0)


def _rw_scan(ops, v_bf, gend):
    _, bsz, seq, _ = ops.shape
    n_chunks = seq // CHUNK
    rev = lambda c: n_chunks - 1 - c
    return pl.pallas_call(
        _rw_scan_kernel,
        grid=(bsz // SCAN_BATCH, n_chunks),
        in_specs=[
            pl.BlockSpec((None, SCAN_BATCH, CHUNK, RW_OPS * RW_WIDTH), lambda b, c: (0, b, c, 0)),
            pl.BlockSpec((None, SCAN_BATCH, CHUNK, RW_OPS * RW_WIDTH), lambda b, c: (1, b, rev(c), 0)),
            pl.BlockSpec((SCAN_BATCH, CHUNK, RW_WIDTH), lambda b, c: (b, c, 0)),
            pl.BlockSpec((SCAN_BATCH, CHUNK, RW_WIDTH), lambda b, c: (b, rev(c), 0)),
            pl.BlockSpec((None, SCAN_BATCH, None, HALO, RW_WIDTH), lambda b, c: (0, b, c, 0, 0)),
            pl.BlockSpec((None, SCAN_BATCH, None, HALO, RW_WIDTH), lambda b, c: (1, b, rev(c), 0, 0)),
        ],
        out_specs=[
            pl.BlockSpec((SCAN_BATCH, CHUNK, RW_WIDTH), lambda b, c: (b, c, 0)),
            pl.BlockSpec((SCAN_BATCH, CHUNK, RW_WIDTH), lambda b, c: (b, rev(c), 0)),
        ],
        out_shape=[jax.ShapeDtypeStruct((bsz, seq, RW_WIDTH), jnp.float32)] * 2,
        scratch_shapes=[pltpu.VMEM((SCAN_BATCH, N_DIR, RW_WIDTH // LANES, LANES, LANES), jnp.float32)],
        compiler_params=pltpu.CompilerParams(
            dimension_semantics=("arbitrary", "arbitrary"), vmem_limit_bytes=VMEM_LIMIT),
        name="rwkv7_scan",
    )(ops, ops, v_bf, v_bf, gend, gend)


def _hg_prep_kernel(q_ref, ff_ref, fb_ref, i_ref, lbl_ref, tri_ref,
                    hq_ref, hk_ref, hv_ref, gend_ref, oi_ref, cs_ref, vs_ref, *, layer):
    lg = lbl_ref[...]
    e = jnp.exp(lg - jnp.max(lg, axis=0, keepdims=True))
    lb = jnp.sum(e[0:layer + 1, :], axis=0, keepdims=True) / jnp.sum(e, axis=0, keepdims=True)

    q = q_ref[...]
    v = i_ref[...]
    v_bf = v.astype(BF16)
    hv_ref[...] = v_bf
    heads = range(HG_HEADS)
    sls = [slice(h * HG_DK, (h + 1) * HG_DK) for h in heads]
    n_chunks = HG_TILE // CHUNK
    trow = lax.broadcasted_iota(jnp.int32, (SUB, HG_DK), 0)
    for h in heads:
        vs_ref[h] = v[:, sls[h]]
    oi_ref[...] = jnp.zeros_like(oi_ref)

    for d, f_ref in enumerate((ff_ref, fb_ref)):
        f = lb + (1.0 - lb) * _sigmoid(f_ref[...])
        kf = 1.0 - f
        b = _select_rows(tri_ref[d], jnp.log(f), 3)
        totals = _chunk_totals(b, d)
        hq_ref[d] = (q * jnp.exp(b)).astype(BF16)
        hk_ref[d] = (kf * jnp.exp(_expand_chunks(totals) - b)).astype(BF16)
        for c, tot in enumerate(totals):
            gend_ref[d, c] = jnp.broadcast_to(jnp.exp(tot), (HALO, HG_WIDTH))
        b2 = b * LOG2E
        cs = b2 - jnp.log2(kf)
        for h in heads:
            cs_ref[d, h] = cs[:, sls[h]]

        def rows(c, p_lo, p_hi):
            lo = c * CHUNK + (p_lo if d == 0 else CHUNK - p_hi)
            return slice(lo, lo + (p_hi - p_lo))

        pending = []
        for c in range(n_chunks):
            size = CHUNK // 2
            while size >= SUB:
                for p in range(0, CHUNK, 2 * size):
                    s_sl = rows(c, p, p + size)
                    t_sl = rows(c, p + size, p + 2 * size)
                    ref = b2[rows(c, p + size - 1, p + size)]
                    q_t = (q[t_sl] * jnp.exp2(b2[t_sl] - ref)).astype(BF16)
                    k_s = (kf[s_sl] * jnp.exp2(ref - b2[s_sl])).astype(BF16)
                    scores = [lax.dot_general(q_t[:, sls[h]], k_s[:, sls[h]], _NT,
                                              preferred_element_type=jnp.float32).astype(BF16) for h in heads]
                    pending.append((t_sl, s_sl, scores))
                size //= 2

        for lo in range(0, HG_TILE, SUB):
            q_i, b_i = q[lo:lo + SUB], b2[lo:lo + SUB]
            acc = [oi_ref[lo:lo + SUB, sls[h]] for h in heads]
            for s in range(SUB):
                row_s = pl.ds(lo + s, SUB, stride=0)
                m = (trow >= s) if d == 0 else (trow <= s)
                for h in heads:
                    xs = jnp.where(m, q_i[:, sls[h]] * jnp.exp2(b_i[:, sls[h]] - cs_ref[d, h, row_s, :]), 0.0)
                    w = jnp.sum(xs, axis=-1, keepdims=True)
                    acc[h] = acc[h] + w * vs_ref[h, row_s, :]
            for h in heads:
                oi_ref[lo:lo + SUB, sls[h]] = acc[h]

        for t_sl, s_sl, scores in pending:
            for h in heads:
                oi_ref[t_sl, sls[h]] += jnp.dot(scores[h], v_bf[s_sl, sls[h]], preferred_element_type=jnp.float32)


def _hg_prep(p_hg, lb_logits, layer, tri):
    bsz, seq, _ = p_hg.shape
    n_chunks = seq // CHUNK
    cpt = HG_TILE // CHUNK
    col = lambda j: pl.BlockSpec((None, HG_TILE, HG_WIDTH), lambda b, t: (b, t, j))
    full = lambda a: pl.BlockSpec(a.shape, lambda b, t: (0,) * a.ndim)
    dir_tile = pl.BlockSpec((N_DIR, None, HG_TILE, HG_WIDTH), lambda b, t: (0, b, t, 0))
    dir_sds = jax.ShapeDtypeStruct((N_DIR, bsz, seq, HG_WIDTH), BF16)
    return pl.pallas_call(
        functools.partial(_hg_prep_kernel, layer=layer),
        grid=(bsz, seq // HG_TILE),
        in_specs=[col(0), col(1), col(2), col(3), full(lb_logits), full(tri)],
        out_specs=[
            dir_tile, dir_tile, col(0),
            pl.BlockSpec((N_DIR, None, cpt, HALO, HG_WIDTH), lambda b, t: (0, b, t, 0, 0)),
            col(0),
        ],
        out_shape=[
            dir_sds, dir_sds,
            jax.ShapeDtypeStruct((bsz, seq, HG_WIDTH), BF16),
            jax.ShapeDtypeStruct((N_DIR, bsz, n_chunks, HALO, HG_WIDTH), jnp.float32),
            jax.ShapeDtypeStruct((bsz, seq, HG_WIDTH), jnp.float32),
        ],
        scratch_shapes=[pltpu.VMEM((N_DIR, HG_HEADS, HG_TILE, HG_DK), jnp.float32),
                        pltpu.VMEM((HG_HEADS, HG_TILE, HG_DK), jnp.float32)],
        compiler_params=pltpu.CompilerParams(
            dimension_semantics=("arbitrary", "arbitrary"), vmem_limit_bytes=VMEM_LIMIT),
        name="hgrn2_prep",
    )(p_hg, p_hg, p_hg, p_hg, lb_logits, tri)


def _hg_scan_kernel(qf_ref, qb_ref, kf_ref, kb_ref, vf_ref, vb_ref, gf_ref, gb_ref, of_ref, ob_ref, s_ref):
    c = pl.program_id(1)

    @pl.when(c == 0)
    def _():
        s_ref[...] = jnp.zeros_like(s_ref)

    dir_refs = ((qf_ref, kf_ref, vf_ref, gf_ref, of_ref), (qb_ref, kb_ref, vb_ref, gb_ref, ob_ref))
    probs = [(n, d, h) for n in range(SCAN_BATCH) for d in range(N_DIR) for h in range(HG_HEADS)]
    sl = lambda h: slice(h * HG_DK, (h + 1) * HG_DK)
    st = [s_ref[n, d, h] for n, d, h in probs]
    o = [lax.dot_general(dir_refs[d][0][n, :, sl(h)], st[i].astype(BF16), _NT,
                         preferred_element_type=jnp.float32) for i, (n, d, h) in enumerate(probs)]
    upd = [lax.dot_general(dir_refs[d][2][n, :, sl(h)], dir_refs[d][1][n, :, sl(h)], _TN,
                           preferred_element_type=jnp.float32) for n, d, h in probs]
    for i, (n, d, h) in enumerate(probs):
        dir_refs[d][4][n, :, sl(h)] = o[i]
        s_ref[n, d, h] = st[i] * dir_refs[d][3][n, 0:1, sl(h)] + upd[i]


def _hg_scan(hq, hk, hv, gend):
    _, bsz, seq, _ = hq.shape
    n_chunks = seq // CHUNK
    rev = lambda c: n_chunks - 1 - c
    fwd = lambda b, c: (0, b, c, 0)
    bwd = lambda b, c: (1, b, rev(c), 0)
    dir_blk = (None, SCAN_BATCH, CHUNK, HG_WIDTH)
    seq_blk = (SCAN_BATCH, CHUNK, HG_WIDTH)
    g_blk = (None, SCAN_BATCH, None, HALO, HG_WIDTH)
    return pl.pallas_call(
        _hg_scan_kernel,
        grid=(bsz // SCAN_BATCH, n_chunks),
        in_specs=[
            pl.BlockSpec(dir_blk, fwd), pl.BlockSpec(dir_blk, bwd),
            pl.BlockSpec(dir_blk, fwd), pl.BlockSpec(dir_blk, bwd),
            pl.BlockSpec(seq_blk, lambda b, c: (b, c, 0)), pl.BlockSpec(seq_blk, lambda b, c: (b, rev(c), 0)),
            pl.BlockSpec(g_blk, lambda b, c: (0, b, c, 0, 0)), pl.BlockSpec(g_blk, lambda b, c: (1, b, rev(c), 0, 0)),
        ],
        out_specs=[
            pl.BlockSpec(seq_blk, lambda b, c: (b, c, 0)),
            pl.BlockSpec(seq_blk, lambda b, c: (b, rev(c), 0)),
        ],
        out_shape=[jax.ShapeDtypeStruct((bsz, seq, HG_WIDTH), jnp.float32)] * 2,
        scratch_shapes=[pltpu.VMEM((SCAN_BATCH, N_DIR, HG_HEADS, HG_DK, HG_DK), jnp.float32)],
        compiler_params=pltpu.CompilerParams(
            dimension_semantics=("arbitrary", "arbitrary"), vmem_limit_bytes=VMEM_LIMIT),
        name="hgrn2_scan",
    )(hq, hq, hk, hk, hv, hv, gend, gend)


def _out_kernel(x_ref, yf_ref, yb_ref, bonus_ref, gate_ref, oi_ref, of_ref, ob_ref, g_ref,
                lnw_ref, lnb_ref, hgn_ref, seg_ref, w_ref, post_ref, out_ref):
    y = yf_ref[...] + yb_ref[...]
    seg = seg_ref[...]
    mu = _select_cols(y, seg, 1) * (1.0 / RW_HEAD_DIM)
    yc = y - mu
    var = _select_cols(yc * yc, seg, 1) * (1.0 / RW_HEAD_DIM)
    y = yc * lax.rsqrt(var + GN_EPS) * lnw_ref[...] + lnb_ref[...]
    rw = (y + bonus_ref[...]) * gate_ref[...]

    o = oi_ref[...] + of_ref[...] + ob_ref[...]
    g = g_ref[...]
    parts = []
    for h in range(HG_HEADS):
        oh = o[:, h * HG_DK:(h + 1) * HG_DK]
        parts.append(oh * lax.rsqrt(jnp.mean(oh * oh, axis=-1, keepdims=True) + NORM_EPS))
    hg = jnp.concatenate(parts, axis=1) * hgn_ref[...] * (g * _sigmoid(g))

    cat = jnp.concatenate([rw, hg], axis=1).astype(BF16)
    yo = jnp.dot(cat, w_ref[...], preferred_element_type=jnp.float32)
    ms = jnp.mean(yo * yo, axis=-1, keepdims=True)
    out_ref[...] = x_ref[...] + yo * lax.rsqrt(ms + NORM_EPS) * post_ref[...]


def _out_proj(x2, y_f, y_b, bonus, gate, o_i, o_f, o_b, p_hg2, ln_w, ln_b, hg_norm, seg, w_bf16, post_g):
    rows = x2.shape[0]
    tile = lambda n: pl.BlockSpec((ROW_TILE, n), lambda i: (i, 0))
    row_spec = lambda n: pl.BlockSpec((1, n), lambda i: (0, 0))
    return pl.pallas_call(
        _out_kernel,
        grid=(rows // ROW_TILE,),
        in_specs=[
            tile(D_MODEL),
            tile(RW_WIDTH), tile(RW_WIDTH), tile(RW_WIDTH), tile(RW_WIDTH),
            tile(HG_WIDTH), tile(HG_WIDTH), tile(HG_WIDTH),
            pl.BlockSpec((ROW_TILE, HG_WIDTH), lambda i: (i, 4)),
            row_spec(RW_WIDTH), row_spec(RW_WIDTH), row_spec(HG_WIDTH),
            pl.BlockSpec((RW_WIDTH, RW_WIDTH), lambda i: (0, 0)),
            pl.BlockSpec((D_MODEL, D_MODEL), lambda i: (0, 0)),
            row_spec(D_MODEL),
        ],
        out_specs=tile(D_MODEL),
        out_shape=jax.ShapeDtypeStruct((rows, D_MODEL), jnp.float32),
        compiler_params=pltpu.CompilerParams(
            dimension_semantics=("arbitrary",), vmem_limit_bytes=VMEM_LIMIT),
        name="out_proj",
    )(x2, y_f, y_b, bonus, gate, o_i, o_f, o_b, p_hg2, ln_w, ln_b, hg_norm, seg, w_bf16, post_g)


def _lora_weights(w):
    z = jnp.zeros_like(w[0])
    wp = jnp.stack([jnp.concatenate([w[0], z], axis=0), jnp.concatenate([z, w[1]], axis=0)])
    hi = wp.astype(BF16)
    return hi, (wp - hi.astype(jnp.float32)).astype(BF16)


def _scan_order_selectors(tile):
    t = jnp.arange(tile)
    same = (t[:, None] // CHUNK) == (t[None, :] // CHUNK)
    fwd = same & (t[None, :] <= t[:, None])
    bwd = same & (t[None, :] >= t[:, None])
    return jnp.stack([fwd, bwd]).astype(BF16)


def kernel(x, pre_norm_g, w_in, rw_shift_prev, rw_shift_next, rw_w0, rw_w2, rw_a0, rw_a2, rw_k_k, rw_k_a,
           rw_r_k, rw_ln_w, rw_ln_b, hg_lb_logits, hg_norm_g, w_out, post_norm_g):
    bsz, seq, dm = x.shape
    depth = w_in.shape[0]
    rows = bsz * seq
    lane = jnp.arange(RW_WIDTH) // RW_HEAD_DIM
    seg = (lane[:, None] == lane[None, :]).astype(BF16)
    tri = _scan_order_selectors(ROW_TILE)
    hg_tri = _scan_order_selectors(HG_TILE)
    row = lambda t: t.reshape(1, -1)
    flat = lambda t: t.reshape(rows, t.shape[-1])
    for l in range(depth):
        x2 = x.reshape(rows, dm)
        p_rw, p_hg = _in_proj(x2, row(pre_norm_g[l]), w_in[l].astype(BF16))
        ops, v_bf, gend, bonus, gate = _rw_prep(
            p_rw.reshape(bsz, seq, RW_COLS), row(rw_shift_prev[l]), row(rw_shift_next[l]),
            rw_w0[l], _lora_weights(rw_w2[l]), rw_a0[l], _lora_weights(rw_a2[l]),
            row(rw_k_k[l]), row(rw_k_a[l]), row(rw_r_k[l]), seg, tri)
        y_f, y_b = _rw_scan(ops, v_bf, gend)
        hq, hk, hv, hg_gend, o_i = _hg_prep(p_hg.reshape(bsz, seq, HG_COLS), hg_lb_logits, l, hg_tri)
        o_f, o_b = _hg_scan(hq, hk, hv, hg_gend)
        out = _out_proj(
            x2, flat(y_f), flat(y_b), flat(bonus), flat(gate), flat(o_i), flat(o_f), flat(o_b), p_hg,
            row(rw_ln_w[l]), row(rw_ln_b[l]), row(jnp.tile(hg_norm_g[l], HG_HEADS)), seg,
            w_out[l].astype(BF16), row(post_norm_g[l]))
        x = out.reshape(bsz, seq, dm)
    return x
```

```python
import functools

import jax
import jax.numpy as jnp
from jax import lax
from jax.experimental import pallas as pl
from jax.experimental.pallas import tpu as pltpu

D_MODEL = 1024
RW_HEAD_DIM = 64
RW_WIDTH = 512
RW_HEADS = 8
HG_DK = 128
HG_WIDTH = 512
HG_HEADS = 4
LORA = 64
NORM_EPS = 1e-6
GN_EPS = 64e-5
RW_COLS = 4 * RW_WIDTH + 4 * LORA
HG_COLS = 5 * HG_WIDTH
IN_COLS = RW_COLS + HG_COLS
N_DIR = 2
RW_OPS = 6

LANES = 128
CHUNK = 64
SUB = 8
ROW_TILE = 256
HALO = 8
SCAN_BATCH = 4
VMEM_LIMIT = 48 * 1024 * 1024

_NN = (((1,), (0,)), ((), ()))
_NT = (((1,), (1,)), ((), ()))
_TN = (((0,), (0,)), ((), ()))
BF16 = jnp.bfloat16
LOG2E = 1.4426950408889634


def _split_bf16(x, terms=2):
    pieces = []
    for _ in range(terms - 1):
        hi = x.astype(BF16)
        pieces.append(hi)
        x = x - hi.astype(jnp.float32)
    pieces.append(x.astype(BF16))
    return pieces


def _select_rows(sel, x, terms):
    acc = None
    for piece in _split_bf16(x, terms):
        t = jnp.dot(sel, piece, preferred_element_type=jnp.float32)
        acc = t if acc is None else acc + t
    return acc


def _select_cols(x, sel, terms):
    acc = None
    for piece in _split_bf16(x, terms):
        t = jnp.dot(piece, sel, preferred_element_type=jnp.float32)
        acc = t if acc is None else acc + t
    return acc


def _dot3(a, b_hi, b_lo):
    dg = functools.partial(jnp.dot, preferred_element_type=jnp.float32)
    a_hi, a_lo = _split_bf16(a)
    return dg(a_hi, b_hi) + (dg(a_hi, b_lo) + dg(a_lo, b_hi))


def _sigmoid(x):
    return 1.0 / (1.0 + jnp.exp(-x))


def _chunk_totals(cum, d):
    last = CHUNK - 1 if d == 0 else 0
    return [cum[c * CHUNK + last:c * CHUNK + last + 1] for c in range(cum.shape[0] // CHUNK)]


def _expand_chunks(rows):
    return jnp.concatenate([jnp.broadcast_to(r, (CHUNK, r.shape[1])) for r in rows], axis=0)


def _in_proj_kernel(x_ref, g_ref, w_ref, prw_ref, phg_ref):
    x = x_ref[...]
    ms = jnp.mean(x * x, axis=-1, keepdims=True)
    h = (x * lax.rsqrt(ms + NORM_EPS) * g_ref[...]).astype(BF16)
    p = jnp.dot(h, w_ref[...], preferred_element_type=jnp.float32)
    prw_ref[...] = p[:, :RW_COLS]
    phg_ref[...] = p[:, RW_COLS:]


def _in_proj(x2, g, w_bf16):
    rows = x2.shape[0]
    return pl.pallas_call(
        _in_proj_kernel,
        grid=(rows // ROW_TILE,),
        in_specs=[
            pl.BlockSpec((ROW_TILE, D_MODEL), lambda i: (i, 0)),
            pl.BlockSpec((1, D_MODEL), lambda i: (0, 0)),
            pl.BlockSpec((D_MODEL, IN_COLS), lambda i: (0, 0)),
        ],
        out_specs=[
            pl.BlockSpec((ROW_TILE, RW_COLS), lambda i: (i, 0)),
            pl.BlockSpec((ROW_TILE, HG_COLS), lambda i: (i, 0)),
        ],
        out_shape=[
            jax.ShapeDtypeStruct((rows, RW_COLS), jnp.float32),
            jax.ShapeDtypeStruct((rows, HG_COLS), jnp.float32),
        ],
        compiler_params=pltpu.CompilerParams(
            dimension_semantics=("arbitrary",), vmem_limit_bytes=VMEM_LIMIT),
        name="in_proj",
    )(x2, g, w_bf16)


def _rw_prep_kernel(pc_ref, pp_ref, pn_ref, mup_ref, mun_ref, w0_ref, w2h_ref, w2l_ref, a0_ref, a2h_ref, a2l_ref,
                    kk_ref, ka_ref, rk_ref, seg_ref, tri_ref,
                    ops_ref, v_ref, gend_ref, bonus_ref, gate_ref, *, n_tiles):
    j = pl.program_id(1)

    p = pc_ref[...]
    row = lax.broadcasted_iota(jnp.int32, (HALO, RW_COLS), 0)
    prev_row = jnp.where(j > 0, pp_ref[HALO - 1:HALO, :], 0.0)
    next_row = jnp.where(j < n_tiles - 1, pn_ref[0:1, :], 0.0)
    p_prev = pltpu.roll(p, 1, 0)
    p_next = pltpu.roll(p, ROW_TILE - 1, 0)
    p_prev = jnp.concatenate([jnp.where(row == 0, prev_row, p_prev[:HALO]), p_prev[HALO:]], axis=0)
    p_next = jnp.concatenate([p_next[:-HALO], jnp.where(row == HALO - 1, next_row, p_next[-HALO:])], axis=0)
    mup, mun = mup_ref[...], mun_ref[...]
    z = p * (1.0 - mup - mun) + mup * p_prev + mun * p_next

    W = RW_WIDTH
    r, k, v, g = z[:, 0:W], z[:, W:2 * W], z[:, 2 * W:3 * W], z[:, 3 * W:4 * W]
    wd = z[:, 4 * W:4 * W + 2 * LORA]
    ad = z[:, 4 * W + 2 * LORA:4 * W + 4 * LORA]
    gate_ref[...] = g * _sigmoid(g)
    v_ref[...] = v.astype(BF16)

    seg = seg_ref[...]
    kk = k * kk_ref[...]
    kk = kk * lax.rsqrt(jnp.maximum(_select_cols(kk * kk, seg, 1), 1e-24))
    tanh_wd = jnp.tanh(wd)

    k_sum = None
    for d in range(N_DIR):
        w_raw = w0_ref[d:d + 1, :] + _dot3(tanh_wd, w2h_ref[d], w2l_ref[d])
        lw = -jnp.exp(jnp.float32(-0.5)) * _sigmoid(w_raw)
        a = _sigmoid(a0_ref[d:d + 1, :] + _dot3(ad, a2h_ref[d], a2l_ref[d]))
        k_d = k * (1.0 + (a - 1.0) * ka_ref[...])
        b_d = kk * a
        k_sum = k_d if k_sum is None else k_sum + k_d
        cl = _select_rows(tri_ref[d], lw, 3)
        totals = _chunk_totals(cl, d)
        e_out = jnp.exp(-cl)
        e_end = jnp.exp(_expand_chunks(totals) - cl)
        ops = (r * jnp.exp(cl),
               kk * jnp.exp(cl - lw),
               k_d * e_out, b_d * e_out,
               k_d * e_end, b_d * e_end)
        for i, op in enumerate(ops):
            ops_ref[d, :, i * W:(i + 1) * W] = op.astype(BF16)
        for q, tot in enumerate(totals):
            gend_ref[d, q] = jnp.broadcast_to(jnp.exp(tot), (HALO, W))

    bonus_ref[...] = _select_cols(r * k_sum * rk_ref[...], seg, 1) * v


def _rw_prep(p_rw, mup, mun, w0, w2, a0, a2, k_k, k_a, r_k, seg, tri):
    bsz, seq, _ = p_rw.shape
    n_tiles = seq // ROW_TILE
    hb = ROW_TILE // HALO
    n_halo = seq // HALO
    n_chunks = seq // CHUNK
    cpt = ROW_TILE // CHUNK
    full = lambda a: pl.BlockSpec(a.shape, lambda b, j: (0,) * a.ndim)
    tile = lambda n: pl.BlockSpec((None, ROW_TILE, n), lambda b, j: (b, j, 0))
    consts = (mup, mun, w0, *w2, a0, *a2, k_k, k_a, r_k, seg, tri)
    return pl.pallas_call(
        functools.partial(_rw_prep_kernel, n_tiles=n_tiles),
        grid=(bsz, n_tiles),
        in_specs=[
            tile(RW_COLS),
            pl.BlockSpec((None, HALO, RW_COLS), lambda b, j: (b, jnp.maximum(j * hb - 1, 0), 0)),
            pl.BlockSpec((None, HALO, RW_COLS), lambda b, j: (b, jnp.minimum((j + 1) * hb, n_halo - 1), 0)),
            *[full(a) for a in consts],
        ],
        out_specs=[
            pl.BlockSpec((N_DIR, None, ROW_TILE, RW_OPS * RW_WIDTH), lambda b, j: (0, b, j, 0)),
            tile(RW_WIDTH),
            pl.BlockSpec((N_DIR, None, cpt, HALO, RW_WIDTH), lambda b, j: (0, b, j, 0, 0)),
            tile(RW_WIDTH),
            tile(RW_WIDTH),
        ],
        out_shape=[
            jax.ShapeDtypeStruct((N_DIR, bsz, seq, RW_OPS * RW_WIDTH), BF16),
            jax.ShapeDtypeStruct((bsz, seq, RW_WIDTH), BF16),
            jax.ShapeDtypeStruct((N_DIR, bsz, n_chunks, HALO, RW_WIDTH), jnp.float32),
            jax.ShapeDtypeStruct((bsz, seq, RW_WIDTH), jnp.float32),
            jax.ShapeDtypeStruct((bsz, seq, RW_WIDTH), jnp.float32),
        ],
        compiler_params=pltpu.CompilerParams(
            dimension_semantics=("arbitrary", "arbitrary"), vmem_limit_bytes=VMEM_LIMIT),
        name="rwkv7_prep",
    )(p_rw, p_rw, p_rw, *consts)


def _rw_scan_kernel(of_ref, ob_ref, vf_ref, vb_ref, gf_ref, gb_ref, yf_ref, yb_ref, s_ref):
    c = pl.program_id(1)

    @pl.when(c == 0)
    def _():
        s_ref[...] = jnp.zeros_like(s_ref)

    N, W = RW_HEAD_DIM, RW_WIDTH
    lane = lax.broadcasted_iota(jnp.int32, (CHUNK, LANES), 1)
    tt = lax.broadcasted_iota(jnp.int32, (CHUNK, LANES), 0)
    half = (lane < N, lane >= N)
    diag = ((lax.broadcasted_iota(jnp.int32, (LANES, LANES), 0) < N)
            == (lax.broadcasted_iota(jnp.int32, (LANES, LANES), 1) < N))
    dir_refs = ((of_ref, vf_ref, gf_ref, yf_ref), (ob_ref, vb_ref, gb_ref, yb_ref))

    pairs = [(q, d, j) for q in range(SCAN_BATCH) for d in range(N_DIR) for j in range(W // LANES)]
    probs = [(pi, par) for pi in range(len(pairs)) for par in range(2)]

    def op(pi, i):
        q, d, j = pairs[pi]
        return dir_refs[d][0][q, :, i * W + j * LANES:i * W + (j + 1) * LANES]

    r_p = [op(pi, 0) for pi in range(len(pairs))]
    a_p = [op(pi, 1) for pi in range(len(pairs))]
    k_p = [op(pi, 2) for pi in range(len(pairs))]
    b_p = [op(pi, 3) for pi in range(len(pairs))]
    ke_p = [op(pi, 4) for pi in range(len(pairs))]
    be_p = [op(pi, 5) for pi in range(len(pairs))]
    v_p = [dir_refs[d][1][q, :, j * LANES:(j + 1) * LANES] for q, d, j in pairs]
    s_p = [s_ref[q, d, j] for q, d, j in pairs]

    def lag(pi):
        d = pairs[pi][1]
        delta = tt - (lane & (N - 1))
        return delta if d == 0 else -delta

    gram = []
    for pi, par in probs:
        lhs = jnp.concatenate([a_p[pi], r_p[pi]], axis=0)
        lhs = jnp.where(jnp.concatenate([half[par], half[par]], axis=0), lhs, jnp.zeros_like(lhs))
        rhs = jnp.concatenate([k_p[pi], b_p[pi]] if par == 0 else [b_p[pi], k_p[pi]], axis=0)
        gram.append(lax.dot_general(lhs, rhs, _NT, preferred_element_type=jnp.float32))

    za, zb, bot = [], [], []
    for n, (pi, par) in enumerate(probs):
        top = gram[n][:CHUNK]
        lg = lag(pi)
        m0 = jnp.where(jnp.where(half[1 - par], lg, 0) > 0, -top, 0.0)
        l_ak = jnp.where(jnp.where(half[par], lg, 0) > 0, top, 0.0)
        bot.append(jnp.where(lg >= 0, gram[n][CHUNK:], 0.0).astype(BF16))
        vv = jnp.concatenate([v_p[pi], v_p[pi]], axis=0)
        zb.append(jnp.dot(l_ak.astype(BF16), vv, preferred_element_type=jnp.float32))
        za.append(jnp.where(half[par], a_p[pi].astype(jnp.float32), m0))

    for level in range(6):
        new_za, new_zb = [], []
        for n, (pi, par) in enumerate(probs):
            m = jnp.where(half[1 - par], za[n], 0.0).astype(BF16)
            z2 = jnp.concatenate([za[n], zb[n]], axis=1).astype(BF16)
            p = jnp.dot(m, jnp.concatenate([z2, z2], axis=0), preferred_element_type=jnp.float32)
            new_za.append(p[:, :LANES] + jnp.where(half[par], za[n], 0.0))
            new_zb.append(p[:, LANES:] + zb[n])
        za, zb = new_za, new_zb

    n_pairs = range(len(pairs))
    s_bf = [s_p[pi].astype(BF16) for pi in n_pairs]
    wr = [jnp.concatenate([jnp.where(half[0], za[2 * pi], za[2 * pi + 1]).astype(BF16), r_p[pi]], axis=0)
          for pi in n_pairs]
    ws = [lax.dot_general(wr[pi], s_bf[pi], _NT, preferred_element_type=jnp.float32) for pi in n_pairs]
    u_bf = [(-(ws[pi][:CHUNK] + jnp.where(half[0], zb[2 * pi], zb[2 * pi + 1]))).astype(BF16)
            for pi in n_pairs]
    y_e = [jnp.dot(bot[2 * pi], jnp.concatenate([v_p[pi], u_bf[pi]], axis=0),
                   preferred_element_type=jnp.float32) for pi in n_pairs]
    y_o = [jnp.dot(bot[2 * pi + 1], jnp.concatenate([u_bf[pi], v_p[pi]], axis=0),
                   preferred_element_type=jnp.float32) for pi in n_pairs]
    upd = [lax.dot_general(jnp.concatenate([v_p[pi], u_bf[pi]], axis=0),
                           jnp.concatenate([ke_p[pi], be_p[pi]], axis=0), _TN,
                           preferred_element_type=jnp.float32) for pi in n_pairs]
    for pi, (q, d, j) in enumerate(pairs):
        dir_refs[d][3][q, :, j * LANES:(j + 1) * LANES] = ws[pi][CHUNK:] + jnp.where(half[0], y_e[pi], y_o[pi])
        g_end = dir_refs[d][2][q, 0:1, j * LANES:(j + 1) * LANES]
        s_ref[q, d, j] = s_p[pi] * g_end + jnp.where(diag, upd[pi], 0.0)


def _rw_scan(ops, v_bf, gend):
    _, bsz, seq, _ = ops.shape
    n_chunks = seq // CHUNK
    rev = lambda c: n_chunks - 1 - c
    return pl.pallas_call(
        _rw_scan_kernel,
        grid=(bsz // SCAN_BATCH, n_chunks),
        in_specs=[
            pl.BlockSpec((None, SCAN_BATCH, CHUNK, RW_OPS * RW_WIDTH), lambda b, c: (0, b, c, 0)),
            pl.BlockSpec((None, SCAN_BATCH, CHUNK, RW_OPS * RW_WIDTH), lambda b, c: (1, b, rev(c), 0)),
            pl.BlockSpec((SCAN_BATCH, CHUNK, RW_WIDTH), lambda b, c: (b, c, 0)),
            pl.BlockSpec((SCAN_BATCH, CHUNK, RW_WIDTH), lambda b, c: (b, rev(c), 0)),
            pl.BlockSpec((None, SCAN_BATCH, None, HALO, RW_WIDTH), lambda b, c: (0, b, c, 0, 0)),
            pl.BlockSpec((None, SCAN_BATCH, None, HALO, RW_WIDTH), lambda b, c: (1, b, rev(c), 0, 0)),
        ],
        out_specs=[
            pl.BlockSpec((SCAN_BATCH, CHUNK, RW_WIDTH), lambda b, c: (b, c, 0)),
            pl.BlockSpec((SCAN_BATCH, CHUNK, RW_WIDTH), lambda b, c: (b, rev(c), 0)),
        ],
        out_shape=[jax.ShapeDtypeStruct((bsz, seq, RW_WIDTH), jnp.float32)] * 2,
        scratch_shapes=[pltpu.VMEM((SCAN_BATCH, N_DIR, RW_WIDTH // LANES, LANES, LANES), jnp.float32)],
        compiler_params=pltpu.CompilerParams(
            dimension_semantics=("arbitrary", "arbitrary"), vmem_limit_bytes=VMEM_LIMIT),
        name="rwkv7_scan",
    )(ops, ops, v_bf, v_bf, gend, gend)


def _hg_kernel(qf_ref, ff_ref, if_ref, qb_ref, fb_ref, ib_ref, lbl_ref, tri_ref,
               of_ref, ob_ref, s_ref, cs_ref, vs_ref, *, layer):
    c = pl.program_id(1)

    @pl.when(c == 0)
    def _():
        s_ref[...] = jnp.zeros_like(s_ref)

    lg = lbl_ref[...]
    e = jnp.exp(lg - jnp.max(lg, axis=0, keepdims=True))
    lb = jnp.sum(e[0:layer + 1, :], axis=0, keepdims=True) / jnp.sum(e, axis=0, keepdims=True)

    heads = range(HG_HEADS)
    sls = [slice(h * HG_DK, (h + 1) * HG_DK) for h in heads]
    trow = lax.broadcasted_iota(jnp.int32, (SUB, HG_DK), 0)
    dir_refs = ((qf_ref, ff_ref, if_ref, of_ref), (qb_ref, fb_ref, ib_ref, ob_ref))
    probs = [(n, d) for n in range(SCAN_BATCH) for d in range(N_DIR)]

    def matmul_part(k):
        n, d = probs[k]
        q_ref, f_ref, i_ref, _ = dir_refs[d]
        q, v = q_ref[n], i_ref[n]
        v_bf = v.astype(BF16)
        f = lb + (1.0 - lb) * _sigmoid(f_ref[n])
        kf = 1.0 - f
        b = _select_rows(tri_ref[d], jnp.log(f), 3)
        (total,) = _chunk_totals(b, d)
        q_in = (q * jnp.exp(b)).astype(BF16)
        k_end = (kf * jnp.exp(total - b)).astype(BF16)
        g_end = jnp.exp(total)
        b2 = b * LOG2E
        cs = b2 - jnp.log2(kf)
        for h in heads:
            cs_ref[k, h] = cs[:, sls[h]]
            vs_ref[k, h] = v[:, sls[h]]

        st = [s_ref[n, d, h] for h in heads]
        inter = [lax.dot_general(q_in[:, sls[h]], st[h].astype(BF16), _NT, preferred_element_type=jnp.float32)
                 for h in heads]
        for h in heads:
            s_ref[n, d, h] = st[h] * g_end[:, sls[h]] + lax.dot_general(
                v_bf[:, sls[h]], k_end[:, sls[h]], _TN, preferred_element_type=jnp.float32)

        def rows(p_lo, p_hi):
            lo = p_lo if d == 0 else CHUNK - p_hi
            return slice(lo, lo + (p_hi - p_lo))

        pending = []
        size = CHUNK // 2
        while size >= SUB:
            for p in range(0, CHUNK, 2 * size):
                s_sl, t_sl = rows(p, p + size), rows(p + size, p + 2 * size)
                ref = b2[rows(p + size - 1, p + size)]
                q_t = (q[t_sl] * jnp.exp2(b2[t_sl] - ref)).astype(BF16)
                k_s = (kf[s_sl] * jnp.exp2(ref - b2[s_sl])).astype(BF16)
                scores = [lax.dot_general(q_t[:, sls[h]], k_s[:, sls[h]], _NT,
                                          preferred_element_type=jnp.float32).astype(BF16) for h in heads]
                pending.append((t_sl, s_sl, scores))
            size //= 2
        return q, b2, v_bf, inter, pending

    def vpu_part(k, q, b2, v_bf, inter, pending):
        n, d = probs[k]
        o_ref = dir_refs[d][3]
        for lo in range(0, CHUNK, SUB):
            q_i, b_i = q[lo:lo + SUB], b2[lo:lo + SUB]
            acc = [inter[h][lo:lo + SUB] for h in heads]
            for s in range(SUB):
                row_s = pl.ds(lo + s, SUB, stride=0)
                m = (trow >= s) if d == 0 else (trow <= s)
                for h in heads:
                    xs = jnp.where(m, q_i[:, sls[h]] * jnp.exp2(b_i[:, sls[h]] - cs_ref[k, h, row_s, :]), 0.0)
                    w = jnp.sum(xs, axis=-1, keepdims=True)
                    acc[h] = acc[h] + w * vs_ref[k, h, row_s, :]
            for h in heads:
                o_ref[n, lo:lo + SUB, sls[h]] = acc[h]
        for t_sl, s_sl, scores in pending:
            for h in heads:
                o_ref[n, t_sl, sls[h]] += jnp.dot(scores[h], v_bf[s_sl, sls[h]], preferred_element_type=jnp.float32)

    ctx = matmul_part(0)
    for k in range(len(probs)):
        nxt = matmul_part(k + 1) if k + 1 < len(probs) else None
        vpu_part(k, *ctx)
        ctx = nxt


def _hg_mix(p_hg, lb_logits, layer, tri):
    bsz, seq, _ = p_hg.shape
    n_chunks = seq // CHUNK
    rev = lambda c: n_chunks - 1 - c
    blk = (SCAN_BATCH, CHUNK, HG_WIDTH)
    fwd = lambda j: pl.BlockSpec(blk, lambda g, c: (g, c, j))
    bwd = lambda j: pl.BlockSpec(blk, lambda g, c: (g, rev(c), j))
    full = lambda a: pl.BlockSpec(a.shape, lambda g, c: (0,) * a.ndim)
    n_prob = SCAN_BATCH * N_DIR
    return pl.pallas_call(
        functools.partial(_hg_kernel, layer=layer),
        grid=(bsz // SCAN_BATCH, n_chunks),
        in_specs=[fwd(0), fwd(1), fwd(3), bwd(0), bwd(2), bwd(3), full(lb_logits), full(tri)],
        out_specs=[fwd(0), bwd(0)],
        out_shape=[jax.ShapeDtypeStruct((bsz, seq, HG_WIDTH), jnp.float32)] * 2,
        scratch_shapes=[pltpu.VMEM((SCAN_BATCH, N_DIR, HG_HEADS, HG_DK, HG_DK), jnp.float32),
                        pltpu.VMEM((n_prob, HG_HEADS, CHUNK, HG_DK), jnp.float32),
                        pltpu.VMEM((n_prob, HG_HEADS, CHUNK, HG_DK), jnp.float32)],
        compiler_params=pltpu.CompilerParams(
            dimension_semantics=("arbitrary", "arbitrary"), vmem_limit_bytes=VMEM_LIMIT),
        name="hgrn2_mix",
    )(p_hg, p_hg, p_hg, p_hg, p_hg, p_hg, lb_logits, tri)


def _out_kernel(x_ref, yf_ref, yb_ref, bonus_ref, gate_ref, of_ref, ob_ref, g_ref,
                lnw_ref, lnb_ref, hgn_ref, seg_ref, w_ref, post_ref, out_ref):
    y = yf_ref[...] + yb_ref[...]
    seg = seg_ref[...]
    mu = _select_cols(y, seg, 1) * (1.0 / RW_HEAD_DIM)
    yc = y - mu
    var = _select_cols(yc * yc, seg, 1) * (1.0 / RW_HEAD_DIM)
    y = yc * lax.rsqrt(var + GN_EPS) * lnw_ref[...] + lnb_ref[...]
    rw = (y + bonus_ref[...]) * gate_ref[...]

    o = of_ref[...] + ob_ref[...]
    g = g_ref[...]
    parts = []
    for h in range(HG_HEADS):
        oh = o[:, h * HG_DK:(h + 1) * HG_DK]
        parts.append(oh * lax.rsqrt(jnp.mean(oh * oh, axis=-1, keepdims=True) + NORM_EPS))
    hg = jnp.concatenate(parts, axis=1) * hgn_ref[...] * (g * _sigmoid(g))

    cat = jnp.concatenate([rw, hg], axis=1).astype(BF16)
    yo = jnp.dot(cat, w_ref[...], preferred_element_type=jnp.float32)
    ms = jnp.mean(yo * yo, axis=-1, keepdims=True)
    out_ref[...] = x_ref[...] + yo * lax.rsqrt(ms + NORM_EPS) * post_ref[...]


def _out_proj(x2, y_f, y_b, bonus, gate, o_f, o_b, p_hg2, ln_w, ln_b, hg_norm, seg, w_bf16, post_g):
    rows = x2.shape[0]
    tile = lambda n: pl.BlockSpec((ROW_TILE, n), lambda i: (i, 0))
    row_spec = lambda n: pl.BlockSpec((1, n), lambda i: (0, 0))
    return pl.pallas_call(
        _out_kernel,
        grid=(rows // ROW_TILE,),
        in_specs=[
            tile(D_MODEL),
            tile(RW_WIDTH), tile(RW_WIDTH), tile(RW_WIDTH), tile(RW_WIDTH),
            tile(HG_WIDTH), tile(HG_WIDTH),
            pl.BlockSpec((ROW_TILE, HG_WIDTH), lambda i: (i, 4)),
            row_spec(RW_WIDTH), row_spec(RW_WIDTH), row_spec(HG_WIDTH),
            pl.BlockSpec((RW_WIDTH, RW_WIDTH), lambda i: (0, 0)),
            pl.BlockSpec((D_MODEL, D_MODEL), lambda i: (0, 0)),
            row_spec(D_MODEL),
        ],
        out_specs=tile(D_MODEL),
        out_shape=jax.ShapeDtypeStruct((rows, D_MODEL), jnp.float32),
        compiler_params=pltpu.CompilerParams(
            dimension_semantics=("arbitrary",), vmem_limit_bytes=VMEM_LIMIT),
        name="out_proj",
    )(x2, y_f, y_b, bonus, gate, o_f, o_b, p_hg2, ln_w, ln_b, hg_norm, seg, w_bf16, post_g)


def _lora_weights(w):
    z = jnp.zeros_like(w[0])
    wp = jnp.stack([jnp.concatenate([w[0], z], axis=0), jnp.concatenate([z, w[1]], axis=0)])
    hi = wp.astype(BF16)
    return hi, (wp - hi.astype(jnp.float32)).astype(BF16)


def _scan_order_selectors(tile):
    t = jnp.arange(tile)
    same = (t[:, None] // CHUNK) == (t[None, :] // CHUNK)
    fwd = same & (t[None, :] <= t[:, None])
    bwd = same & (t[None, :] >= t[:, None])
    return jnp.stack([fwd, bwd]).astype(BF16)


def kernel(x, pre_norm_g, w_in, rw_shift_prev, rw_shift_next, rw_w0, rw_w2, rw_a0, rw_a2, rw_k_k, rw_k_a,
           rw_r_k, rw_ln_w, rw_ln_b, hg_lb_logits, hg_norm_g, w_out, post_norm_g):
    bsz, seq, dm = x.shape
    depth = w_in.shape[0]
    rows = bsz * seq
    lane = jnp.arange(RW_WIDTH) // RW_HEAD_DIM
    seg = (lane[:, None] == lane[None, :]).astype(BF16)
    tri = _scan_order_selectors(ROW_TILE)
    hg_tri = _scan_order_selectors(CHUNK)
    row = lambda t: t.reshape(1, -1)
    flat = lambda t: t.reshape(rows, t.shape[-1])
    for l in range(depth):
        x2 = x.reshape(rows, dm)
        p_rw, p_hg = _in_proj(x2, row(pre_norm_g[l]), w_in[l].astype(BF16))
        ops, v_bf, gend, bonus, gate = _rw_prep(
            p_rw.reshape(bsz, seq, RW_COLS), row(rw_shift_prev[l]), row(rw_shift_next[l]),
            rw_w0[l], _lora_weights(rw_w2[l]), rw_a0[l], _lora_weights(rw_a2[l]),
            row(rw_k_k[l]), row(rw_k_a[l]), row(rw_r_k[l]), seg, tri)
        y_f, y_b = _rw_scan(ops, v_bf, gend)
        o_f, o_b = _hg_mix(p_hg.reshape(bsz, seq, HG_COLS), hg_lb_logits, l, hg_tri)
        out = _out_proj(
            x2, flat(y_f), flat(y_b), flat(bonus), flat(gate), flat(o_f), flat(o_b), p_hg,
            row(rw_ln_w[l]), row(rw_ln_b[l]), row(jnp.tile(hg_norm_g[l], HG_HEADS)), seg,
            w_out[l].astype(BF16), row(post_norm_g[l]))
        x = out.reshape(bsz, seq, dm)
    return x
```

```python
import functools

import jax
import jax.numpy as jnp
from jax import lax
from jax.experimental import pallas as pl
from jax.experimental.pallas import tpu as pltpu

D_MODEL = 1024
RW_HEAD_DIM = 64
RW_WIDTH = 512
RW_HEADS = 8
HG_DK = 128
HG_WIDTH = 512
HG_HEADS = 4
LORA = 64
NORM_EPS = 1e-6
GN_EPS = 64e-5
RW_COLS = 4 * RW_WIDTH + 4 * LORA
HG_COLS = 5 * HG_WIDTH
IN_COLS = RW_COLS + HG_COLS
N_DIR = 2
RW_OPS = 6

LANES = 128
CHUNK = 64
SUB = 8
ROW_TILE = 256
HALO = 8
SCAN_BATCH = 4
VMEM_LIMIT = 56 * 1024 * 1024

_NN = (((1,), (0,)), ((), ()))
_NT = (((1,), (1,)), ((), ()))
_TN = (((0,), (0,)), ((), ()))
BF16 = jnp.bfloat16
LOG2E = 1.4426950408889634


def _split_bf16(x, terms=2):
    pieces = []
    for _ in range(terms - 1):
        hi = x.astype(BF16)
        pieces.append(hi)
        x = x - hi.astype(jnp.float32)
    pieces.append(x.astype(BF16))
    return pieces


def _select_rows(sel, x, terms):
    acc = None
    for piece in _split_bf16(x, terms):
        t = jnp.dot(sel, piece, preferred_element_type=jnp.float32)
        acc = t if acc is None else acc + t
    return acc


def _select_cols(x, sel, terms):
    acc = None
    for piece in _split_bf16(x, terms):
        t = jnp.dot(piece, sel, preferred_element_type=jnp.float32)
        acc = t if acc is None else acc + t
    return acc


def _sigmoid(x):
    return 1.0 / (1.0 + jnp.exp(-x))


def _chunk_totals(cum, d):
    last = CHUNK - 1 if d == 0 else 0
    return [cum[c * CHUNK + last:c * CHUNK + last + 1] for c in range(cum.shape[0] // CHUNK)]


def _expand_chunks(rows):
    return jnp.concatenate([jnp.broadcast_to(r, (CHUNK, r.shape[1])) for r in rows], axis=0)


def _in_rw_kernel(xc_ref, xp_ref, xn_ref, g_ref, wrw_ref, whg_ref,
                  mup_ref, mun_ref, w0_ref, w2_ref, a0_ref, a2_ref,
                  kk_ref, ka_ref, rk_ref, seg_ref, tri_ref,
                  phg_ref, ops_ref, v_ref, gend_ref, bonus_ref, gate_ref, *, n_tiles):
    j = pl.program_id(1)

    def pre_norm(x):
        return x * lax.rsqrt(jnp.mean(x * x, axis=-1, keepdims=True) + NORM_EPS) * g_ref[...]

    h_c = pre_norm(xc_ref[...])
    h_p = pre_norm(xp_ref[...]) * (j > 0).astype(jnp.float32)
    h_n = pre_norm(xn_ref[...]) * (j < n_tiles - 1).astype(jnp.float32)
    h = h_c.astype(BF16)
    h_ext = jnp.concatenate([h_p, h_c, h_n], axis=0).astype(BF16)

    hg_blocks = iter(range(HG_COLS // HG_WIDTH))

    def hg_block():
        i = next(hg_blocks, None)
        if i is not None:
            cols = slice(i * HG_WIDTH, (i + 1) * HG_WIDTH)
            phg_ref[:, cols] = jnp.dot(h, whg_ref[:, cols], preferred_element_type=jnp.float32)

    W = RW_WIDTH
    ext = ROW_TILE + 2 * HALO
    tile = slice(HALO, HALO + ROW_TILE)

    def shifted(cols):
        p_ext = jnp.dot(h_ext, wrw_ref[:, cols], preferred_element_type=jnp.float32)
        p, p_prev, p_next = p_ext[tile], pltpu.roll(p_ext, 1, 0)[tile], pltpu.roll(p_ext, ext - 1, 0)[tile]
        mup, mun = mup_ref[:, cols], mun_ref[:, cols]
        return p * (1.0 - mup - mun) + mup * p_prev + mun * p_next

    r, k, v, g = [shifted(slice(i * W, (i + 1) * W)) for i in range(4)]
    lora_in = shifted(slice(4 * W, RW_COLS))
    wd = lora_in[:, :2 * LORA]
    ad = lora_in[:, 2 * LORA:].astype(BF16)
    hg_block()
    gate_ref[...] = g * _sigmoid(g)
    v_ref[...] = v.astype(BF16)

    seg = seg_ref[...]
    kk = k * kk_ref[...]
    kk = kk * lax.rsqrt(jnp.maximum(_select_cols(kk * kk, seg, 1), 1e-24))
    tanh_wd = jnp.tanh(wd).astype(BF16)
    hg_block()

    k_sum = None
    for d in range(N_DIR):
        w_raw = w0_ref[d:d + 1, :] + jnp.dot(tanh_wd, w2_ref[d], preferred_element_type=jnp.float32)
        lw = (-LOG2E * jnp.exp(jnp.float32(-0.5))) * _sigmoid(w_raw)
        a = _sigmoid(a0_ref[d:d + 1, :] + jnp.dot(ad, a2_ref[d], preferred_element_type=jnp.float32))
        k_d = k * (1.0 + (a - 1.0) * ka_ref[...])
        b_d = kk * a
        k_sum = k_d if k_sum is None else k_sum + k_d
        cl = _select_rows(tri_ref[d], lw, 2)
        g_end = [jnp.exp2(tot) for tot in _chunk_totals(cl, d)]
        hg_block()
        e_out = jnp.exp2(-cl)
        e_end = e_out * _expand_chunks(g_end)
        ops = (r * jnp.exp2(cl),
               kk * jnp.exp2(cl - lw),
               k_d * e_out, b_d * e_out,
               k_d * e_end, b_d * e_end)
        for i, op in enumerate(ops):
            ops_ref[d, :, i * W:(i + 1) * W] = op.astype(BF16)
        for q, g_q in enumerate(g_end):
            gend_ref[d, q] = jnp.broadcast_to(g_q, (HALO, W))
        hg_block()

    bonus_ref[...] = _select_cols(r * k_sum * rk_ref[...], seg, 1) * v
    for _ in hg_blocks:
        raise AssertionError("HGRN2 column blocks left unissued")


def _in_rw(x, g, w_rw, w_hg, mup, mun, w0, w2, a0, a2, k_k, k_a, r_k, seg, tri):
    bsz, seq, _ = x.shape
    n_tiles = seq // ROW_TILE
    hb = ROW_TILE // HALO
    n_halo = seq // HALO
    n_chunks = seq // CHUNK
    cpt = ROW_TILE // CHUNK
    full = lambda a: pl.BlockSpec(a.shape, lambda b, j: (0,) * a.ndim)
    tile = lambda n: pl.BlockSpec((None, ROW_TILE, n), lambda b, j: (b, j, 0))
    consts = (g, w_rw, w_hg, mup, mun, w0, w2, a0, a2, k_k, k_a, r_k, seg, tri)
    return pl.pallas_call(
        functools.partial(_in_rw_kernel, n_tiles=n_tiles),
        grid=(bsz, n_tiles),
        in_specs=[
            tile(D_MODEL),
            pl.BlockSpec((None, HALO, D_MODEL), lambda b, j: (b, jnp.maximum(j * hb - 1, 0), 0)),
            pl.BlockSpec((None, HALO, D_MODEL), lambda b, j: (b, jnp.minimum((j + 1) * hb, n_halo - 1), 0)),
            *[full(a) for a in consts],
        ],
        out_specs=[
            tile(HG_COLS),
            pl.BlockSpec((N_DIR, None, ROW_TILE, RW_OPS * RW_WIDTH), lambda b, j: (0, b, j, 0)),
            tile(RW_WIDTH),
            pl.BlockSpec((N_DIR, None, cpt, HALO, RW_WIDTH), lambda b, j: (0, b, j, 0, 0)),
            tile(RW_WIDTH),
            tile(RW_WIDTH),
        ],
        out_shape=[
            jax.ShapeDtypeStruct((bsz, seq, HG_COLS), jnp.float32),
            jax.ShapeDtypeStruct((N_DIR, bsz, seq, RW_OPS * RW_WIDTH), BF16),
            jax.ShapeDtypeStruct((bsz, seq, RW_WIDTH), BF16),
            jax.ShapeDtypeStruct((N_DIR, bsz, n_chunks, HALO, RW_WIDTH), jnp.float32),
            jax.ShapeDtypeStruct((bsz, seq, RW_WIDTH), jnp.float32),
            jax.ShapeDtypeStruct((bsz, seq, RW_WIDTH), jnp.float32),
        ],
        compiler_params=pltpu.CompilerParams(
            dimension_semantics=("arbitrary", "arbitrary"), vmem_limit_bytes=VMEM_LIMIT),
        name="in_proj_rwkv7_prep",
    )(x, x, x, *consts)


def _rw_scan_kernel(of_ref, ob_ref, vf_ref, vb_ref, gf_ref, gb_ref, yf_ref, yb_ref, s_ref):
    c = pl.program_id(1)

    @pl.when(c == 0)
    def _():
        s_ref[...] = jnp.zeros_like(s_ref)

    N, W = RW_HEAD_DIM, RW_WIDTH
    lane = lax.broadcasted_iota(jnp.int32, (CHUNK, LANES), 1)
    tt = lax.broadcasted_iota(jnp.int32, (CHUNK, LANES), 0)
    half = (lane < N, lane >= N)
    diag = ((lax.broadcasted_iota(jnp.int32, (LANES, LANES), 0) < N)
            == (lax.broadcasted_iota(jnp.int32, (LANES, LANES), 1) < N))
    dir_refs = ((of_ref, vf_ref, gf_ref, yf_ref), (ob_ref, vb_ref, gb_ref, yb_ref))

    pairs = [(q, d, j) for q in range(SCAN_BATCH) for d in range(N_DIR) for j in range(W // LANES)]
    probs = [(pi, par) for pi in range(len(pairs)) for par in range(2)]

    def op(pi, i):
        q, d, j = pairs[pi]
        return dir_refs[d][0][q, :, i * W + j * LANES:i * W + (j + 1) * LANES]

    r_p = [op(pi, 0) for pi in range(len(pairs))]
    a_p = [op(pi, 1) for pi in range(len(pairs))]
    k_p = [op(pi, 2) for pi in range(len(pairs))]
    b_p = [op(pi, 3) for pi in range(len(pairs))]
    ke_p = [op(pi, 4) for pi in range(len(pairs))]
    be_p = [op(pi, 5) for pi in range(len(pairs))]
    v_p = [dir_refs[d][1][q, :, j * LANES:(j + 1) * LANES] for q, d, j in pairs]
    s_p = [s_ref[q, d, j] for q, d, j in pairs]

    def lag(pi):
        d = pairs[pi][1]
        delta = tt - (lane & (N - 1))
        return delta if d == 0 else -delta

    gram = []
    for pi, par in probs:
        lhs = jnp.concatenate([a_p[pi], r_p[pi]], axis=0)
        lhs = jnp.where(jnp.concatenate([half[par], half[par]], axis=0), lhs, jnp.zeros_like(lhs))
        rhs = jnp.concatenate([k_p[pi], b_p[pi]] if par == 0 else [b_p[pi], k_p[pi]], axis=0)
        gram.append(lax.dot_general(lhs, rhs, _NT, preferred_element_type=jnp.float32))

    za, zb, bot = [], [], []
    for n, (pi, par) in enumerate(probs):
        top = gram[n][:CHUNK]
        lg = lag(pi)
        m0 = jnp.where(jnp.where(half[1 - par], lg, 0) > 0, -top, 0.0)
        l_ak = jnp.where(jnp.where(half[par], lg, 0) > 0, top, 0.0)
        bot.append(jnp.where(lg >= 0, gram[n][CHUNK:], 0.0).astype(BF16))
        vv = jnp.concatenate([v_p[pi], v_p[pi]], axis=0)
        zb.append(jnp.dot(l_ak.astype(BF16), vv, preferred_element_type=jnp.float32))
        za.append(jnp.where(half[par], a_p[pi].astype(jnp.float32), m0))

    for level in range(6):
        new_za, new_zb = [], []
        for n, (pi, par) in enumerate(probs):
            m = jnp.where(half[1 - par], za[n], 0.0).astype(BF16)
            z2 = jnp.concatenate([za[n], zb[n]], axis=1).astype(BF16)
            p = jnp.dot(m, jnp.concatenate([z2, z2], axis=0), preferred_element_type=jnp.float32)
            new_za.append(p[:, :LANES] + jnp.where(half[par], za[n], 0.0))
            new_zb.append(p[:, LANES:] + zb[n])
        za, zb = new_za, new_zb

    n_pairs = range(len(pairs))
    s_bf = [s_p[pi].astype(BF16) for pi in n_pairs]
    wr = [jnp.concatenate([jnp.where(half[0], za[2 * pi], za[2 * pi + 1]).astype(BF16), r_p[pi]], axis=0)
          for pi in n_pairs]
    ws = [lax.dot_general(wr[pi], s_bf[pi], _NT, preferred_element_type=jnp.float32) for pi in n_pairs]
    u_bf = [(-(ws[pi][:CHUNK] + jnp.where(half[0], zb[2 * pi], zb[2 * pi + 1]))).astype(BF16)
            for pi in n_pairs]
    y_e = [jnp.dot(bot[2 * pi], jnp.concatenate([v_p[pi], u_bf[pi]], axis=0),
                   preferred_element_type=jnp.float32) for pi in n_pairs]
    y_o = [jnp.dot(bot[2 * pi + 1], jnp.concatenate([u_bf[pi], v_p[pi]], axis=0),
                   preferred_element_type=jnp.float32) for pi in n_pairs]
    upd = [lax.dot_general(jnp.concatenate([v_p[pi], u_bf[pi]], axis=0),
                           jnp.concatenate([ke_p[pi], be_p[pi]], axis=0), _TN,
                           preferred_element_type=jnp.float32) for pi in n_pairs]
    for pi, (q, d, j) in enumerate(pairs):
        dir_refs[d][3][q, :, j * LANES:(j + 1) * LANES] = ws[pi][CHUNK:] + jnp.where(half[0], y_e[pi], y_o[pi])
        g_end = dir_refs[d][2][q, 0:1, j * LANES:(j + 1) * LANES]
        s_ref[q, d, j] = s_p[pi] * g_end + jnp.where(diag, upd[pi], 0.0)


def _rw_scan(ops, v_bf, gend):
    _, bsz, seq, _ = ops.shape
    n_chunks = seq // CHUNK
    rev = lambda c: n_chunks - 1 - c
    return pl.pallas_call(
        _rw_scan_kernel,
        grid=(bsz // SCAN_BATCH, n_chunks),
        in_specs=[
            pl.BlockSpec((None, SCAN_BATCH, CHUNK, RW_OPS * RW_WIDTH), lambda b, c: (0, b, c, 0)),
            pl.BlockSpec((None, SCAN_BATCH, CHUNK, RW_OPS * RW_WIDTH), lambda b, c: (1, b, rev(c), 0)),
            pl.BlockSpec((SCAN_BATCH, CHUNK, RW_WIDTH), lambda b, c: (b, c, 0)),
            pl.BlockSpec((SCAN_BATCH, CHUNK, RW_WIDTH), lambda b, c: (b, rev(c), 0)),
            pl.BlockSpec((None, SCAN_BATCH, None, HALO, RW_WIDTH), lambda b, c: (0, b, c, 0, 0)),
            pl.BlockSpec((None, SCAN_BATCH, None, HALO, RW_WIDTH), lambda b, c: (1, b, rev(c), 0, 0)),
        ],
        out_specs=[
            pl.BlockSpec((SCAN_BATCH, CHUNK, RW_WIDTH), lambda b, c: (b, c, 0)),
            pl.BlockSpec((SCAN_BATCH, CHUNK, RW_WIDTH), lambda b, c: (b, rev(c), 0)),
        ],
        out_shape=[jax.ShapeDtypeStruct((bsz, seq, RW_WIDTH), jnp.float32)] * 2,
        scratch_shapes=[pltpu.VMEM((SCAN_BATCH, N_DIR, RW_WIDTH // LANES, LANES, LANES), jnp.float32)],
        compiler_params=pltpu.CompilerParams(
            dimension_semantics=("arbitrary", "arbitrary"), vmem_limit_bytes=VMEM_LIMIT),
        name="rwkv7_scan",
    )(ops, ops, v_bf, v_bf, gend, gend)


def _hg_kernel(qf_ref, ff_ref, if_ref, qb_ref, fb_ref, ib_ref, lbl_ref, tri_ref,
               of_ref, ob_ref, s_ref, cs_ref, vs_ref, *, layer):
    c = pl.program_id(1)

    @pl.when(c == 0)
    def _():
        s_ref[...] = jnp.zeros_like(s_ref)

    lg = lbl_ref[...]
    e = jnp.exp(lg - jnp.max(lg, axis=0, keepdims=True))
    lb = jnp.sum(e[0:layer + 1, :], axis=0, keepdims=True) / jnp.sum(e, axis=0, keepdims=True)

    heads = range(HG_HEADS)
    sls = [slice(h * HG_DK, (h + 1) * HG_DK) for h in heads]
    trow = lax.broadcasted_iota(jnp.int32, (SUB, HG_DK), 0)
    dir_refs = ((qf_ref, ff_ref, if_ref, of_ref), (qb_ref, fb_ref, ib_ref, ob_ref))
    probs = [(n, d) for n in range(SCAN_BATCH) for d in range(N_DIR)]

    def matmul_part(k):
        n, d = probs[k]
        q_ref, f_ref, i_ref, _ = dir_refs[d]
        q, v = q_ref[n], i_ref[n]
        v_bf = v.astype(BF16)
        f = lb + (1.0 - lb) * _sigmoid(f_ref[n])
        kf = 1.0 - f
        b = _select_rows(tri_ref[d], jnp.log(f), 3)
        (total,) = _chunk_totals(b, d)
        q_in = (q * jnp.exp(b)).astype(BF16)
        k_end = (kf * jnp.exp(total - b)).astype(BF16)
        g_end = jnp.exp(total)
        b2 = b * LOG2E
        cs = b2 - jnp.log2(kf)
        for h in heads:
            cs_ref[k, h] = cs[:, sls[h]]
            vs_ref[k, h] = v[:, sls[h]]

        st = [s_ref[n, d, h] for h in heads]
        inter = [lax.dot_general(q_in[:, sls[h]], st[h].astype(BF16), _NT, preferred_element_type=jnp.float32)
                 for h in heads]
        for h in heads:
            s_ref[n, d, h] = st[h] * g_end[:, sls[h]] + lax.dot_general(
                v_bf[:, sls[h]], k_end[:, sls[h]], _TN, preferred_element_type=jnp.float32)

        def rows(p_lo, p_hi):
            lo = p_lo if d == 0 else CHUNK - p_hi
            return slice(lo, lo + (p_hi - p_lo))

        pending = []
        size = CHUNK // 2
        while size >= SUB:
            for p in range(0, CHUNK, 2 * size):
                s_sl, t_sl = rows(p, p + size), rows(p + size, p + 2 * size)
                ref = b2[rows(p + size - 1, p + size)]
                q_t = (q[t_sl] * jnp.exp2(b2[t_sl] - ref)).astype(BF16)
                k_s = (kf[s_sl] * jnp.exp2(ref - b2[s_sl])).astype(BF16)
                scores = [lax.dot_general(q_t[:, sls[h]], k_s[:, sls[h]], _NT,
                                          preferred_element_type=jnp.float32).astype(BF16) for h in heads]
                pending.append((t_sl, s_sl, scores))
            size //= 2
        return q, b2, v_bf, inter, pending

    def vpu_part(k, q, b2, v_bf, inter, pending):
        n, d = probs[k]
        o_ref = dir_refs[d][3]
        for lo in range(0, CHUNK, SUB):
            q_i, b_i = q[lo:lo + SUB], b2[lo:lo + SUB]
            acc = [inter[h][lo:lo + SUB] for h in heads]
            for s in range(SUB):
                row_s = pl.ds(lo + s, SUB, stride=0)
                m = (trow >= s) if d == 0 else (trow <= s)
                for h in heads:
                    xs = jnp.where(m, q_i[:, sls[h]] * jnp.exp2(b_i[:, sls[h]] - cs_ref[k, h, row_s, :]), 0.0)
                    w = jnp.sum(xs, axis=-1, keepdims=True)
                    acc[h] = acc[h] + w * vs_ref[k, h, row_s, :]
            for h in heads:
                o_ref[n, lo:lo + SUB, sls[h]] = acc[h]
        for t_sl, s_sl, scores in pending:
            for h in heads:
                o_ref[n, t_sl, sls[h]] += jnp.dot(scores[h], v_bf[s_sl, sls[h]], preferred_element_type=jnp.float32)

    ctx = matmul_part(0)
    for k in range(len(probs)):
        nxt = matmul_part(k + 1) if k + 1 < len(probs) else None
        vpu_part(k, *ctx)
        ctx = nxt


def _hg_mix(p_hg, lb_logits, layer, tri):
    bsz, seq, _ = p_hg.shape
    n_chunks = seq // CHUNK
    rev = lambda c: n_chunks - 1 - c
    blk = (SCAN_BATCH, CHUNK, HG_WIDTH)
    fwd = lambda j: pl.BlockSpec(blk, lambda g, c: (g, c, j))
    bwd = lambda j: pl.BlockSpec(blk, lambda g, c: (g, rev(c), j))
    full = lambda a: pl.BlockSpec(a.shape, lambda g, c: (0,) * a.ndim)
    n_prob = SCAN_BATCH * N_DIR
    return pl.pallas_call(
        functools.partial(_hg_kernel, layer=layer),
        grid=(bsz // SCAN_BATCH, n_chunks),
        in_specs=[fwd(0), fwd(1), fwd(3), bwd(0), bwd(2), bwd(3), full(lb_logits), full(tri)],
        out_specs=[fwd(0), bwd(0)],
        out_shape=[jax.ShapeDtypeStruct((bsz, seq, HG_WIDTH), jnp.float32)] * 2,
        scratch_shapes=[pltpu.VMEM((SCAN_BATCH, N_DIR, HG_HEADS, HG_DK, HG_DK), jnp.float32),
                        pltpu.VMEM((n_prob, HG_HEADS, CHUNK, HG_DK), jnp.float32),
                        pltpu.VMEM((n_prob, HG_HEADS, CHUNK, HG_DK), jnp.float32)],
        compiler_params=pltpu.CompilerParams(
            dimension_semantics=("arbitrary", "arbitrary"), vmem_limit_bytes=VMEM_LIMIT),
        name="hgrn2_mix",
    )(p_hg, p_hg, p_hg, p_hg, p_hg, p_hg, lb_logits, tri)


def _out_kernel(x_ref, yf_ref, yb_ref, bonus_ref, gate_ref, of_ref, ob_ref, g_ref,
                lnw_ref, lnb_ref, hgn_ref, seg_ref, w_ref, post_ref, out_ref):
    y = yf_ref[...] + yb_ref[...]
    seg = seg_ref[...]
    mu = _select_cols(y, seg, 1) * (1.0 / RW_HEAD_DIM)
    yc = y - mu
    var = _select_cols(yc * yc, seg, 1) * (1.0 / RW_HEAD_DIM)
    y = yc * lax.rsqrt(var + GN_EPS) * lnw_ref[...] + lnb_ref[...]
    rw = (y + bonus_ref[...]) * gate_ref[...]

    o = of_ref[...] + ob_ref[...]
    g = g_ref[...]
    parts = []
    for h in range(HG_HEADS):
        oh = o[:, h * HG_DK:(h + 1) * HG_DK]
        parts.append(oh * lax.rsqrt(jnp.mean(oh * oh, axis=-1, keepdims=True) + NORM_EPS))
    hg = jnp.concatenate(parts, axis=1) * hgn_ref[...] * (g * _sigmoid(g))

    cat = jnp.concatenate([rw, hg], axis=1).astype(BF16)
    yo = jnp.dot(cat, w_ref[...], preferred_element_type=jnp.float32)
    ms = jnp.mean(yo * yo, axis=-1, keepdims=True)
    out_ref[...] = x_ref[...] + yo * lax.rsqrt(ms + NORM_EPS) * post_ref[...]


def _out_proj(x2, y_f, y_b, bonus, gate, o_f, o_b, p_hg2, ln_w, ln_b, hg_norm, seg, w_bf16, post_g):
    rows = x2.shape[0]
    tile = lambda n: pl.BlockSpec((ROW_TILE, n), lambda i: (i, 0))
    row_spec = lambda n: pl.BlockSpec((1, n), lambda i: (0, 0))
    return pl.pallas_call(
        _out_kernel,
        grid=(rows // ROW_TILE,),
        in_specs=[
            tile(D_MODEL),
            tile(RW_WIDTH), tile(RW_WIDTH), tile(RW_WIDTH), tile(RW_WIDTH),
            tile(HG_WIDTH), tile(HG_WIDTH),
            pl.BlockSpec((ROW_TILE, HG_WIDTH), lambda i: (i, 4)),
            row_spec(RW_WIDTH), row_spec(RW_WIDTH), row_spec(HG_WIDTH),
            pl.BlockSpec((RW_WIDTH, RW_WIDTH), lambda i: (0, 0)),
            pl.BlockSpec((D_MODEL, D_MODEL), lambda i: (0, 0)),
            row_spec(D_MODEL),
        ],
        out_specs=tile(D_MODEL),
        out_shape=jax.ShapeDtypeStruct((rows, D_MODEL), jnp.float32),
        compiler_params=pltpu.CompilerParams(
            dimension_semantics=("arbitrary",), vmem_limit_bytes=VMEM_LIMIT),
        name="out_proj",
    )(x2, y_f, y_b, bonus, gate, o_f, o_b, p_hg2, ln_w, ln_b, hg_norm, seg, w_bf16, post_g)


def _lora_weights(w):
    z = jnp.zeros_like(w[0])
    wp = jnp.stack([jnp.concatenate([w[0], z], axis=0), jnp.concatenate([z, w[1]], axis=0)])
    return wp.astype(BF16)


def _scan_order_selectors(tile):
    t = jnp.arange(tile)
    same = (t[:, None] // CHUNK) == (t[None, :] // CHUNK)
    fwd = same & (t[None, :] <= t[:, None])
    bwd = same & (t[None, :] >= t[:, None])
    return jnp.stack([fwd, bwd]).astype(BF16)


def kernel(x, pre_norm_g, w_in, rw_shift_prev, rw_shift_next, rw_w0, rw_w2, rw_a0, rw_a2, rw_k_k, rw_k_a,
           rw_r_k, rw_ln_w, rw_ln_b, hg_lb_logits, hg_norm_g, w_out, post_norm_g):
    bsz, seq, dm = x.shape
    depth = w_in.shape[0]
    rows = bsz * seq
    lane = jnp.arange(RW_WIDTH) // RW_HEAD_DIM
    seg = (lane[:, None] == lane[None, :]).astype(BF16)
    tri = _scan_order_selectors(ROW_TILE)
    hg_tri = _scan_order_selectors(CHUNK)
    row = lambda t: t.reshape(1, -1)
    flat = lambda t: t.reshape(rows, t.shape[-1])
    for l in range(depth):
        x2 = x.reshape(rows, dm)
        w_bf = w_in[l].astype(BF16)
        p_hg, ops, v_bf, gend, bonus, gate = _in_rw(
            x, row(pre_norm_g[l]), w_bf[:, :RW_COLS], w_bf[:, RW_COLS:],
            row(rw_shift_prev[l]), row(rw_shift_next[l]),
            rw_w0[l], _lora_weights(rw_w2[l]), rw_a0[l], _lora_weights(rw_a2[l]),
            row(rw_k_k[l]), row(rw_k_a[l]), row(rw_r_k[l]), seg, tri)
        y_f, y_b = _rw_scan(ops, v_bf, gend)
        o_f, o_b = _hg_mix(p_hg, hg_lb_logits, l, hg_tri)
        out = _out_proj(
            x2, flat(y_f), flat(y_b), flat(bonus), flat(gate), flat(o_f), flat(o_b), flat(p_hg),
            row(rw_ln_w[l]), row(rw_ln_b[l]), row(jnp.tile(hg_norm_g[l], HG_HEADS)), seg,
            w_out[l].astype(BF16), row(post_norm_g[l]))
        x = out.reshape(bsz, seq, dm)
    return x
```

```python
import functools

import jax
import jax.numpy as jnp
from jax import lax
from jax.experimental import pallas as pl
from jax.experimental.pallas import tpu as pltpu

D_MODEL = 1024
RW_HEAD_DIM = 64
RW_WIDTH = 512
RW_HEADS = 8
HG_DK = 128
HG_WIDTH = 512
HG_HEADS = 4
LORA = 64
NORM_EPS = 1e-6
GN_EPS = 64e-5
RW_COLS = 4 * RW_WIDTH + 4 * LORA
HG_COLS = 5 * HG_WIDTH
IN_COLS = RW_COLS + HG_COLS
N_DIR = 2
RW_OPS = 6

LANES = 128
LHS_ROWS = 16
CHUNK = 64
SUB = 8
ROW_TILE = 256
HALO = 8
SCAN_BATCH = 4
VMEM_LIMIT = 56 * 1024 * 1024

_NN = (((1,), (0,)), ((), ()))
_NT = (((1,), (1,)), ((), ()))
_TN = (((0,), (0,)), ((), ()))
BF16 = jnp.bfloat16
LOG2E = 1.4426950408889634


def _split_bf16(x, terms=2):
    pieces = []
    for _ in range(terms - 1):
        hi = x.astype(BF16)
        pieces.append(hi)
        x = x - hi.astype(jnp.float32)
    pieces.append(x.astype(BF16))
    return pieces


def _select_rows(sel, x, terms):
    acc = None
    for piece in _split_bf16(x, terms):
        t = jnp.dot(sel, piece, preferred_element_type=jnp.float32)
        acc = t if acc is None else acc + t
    return acc


def _select_cols(x, sel, terms):
    acc = None
    for piece in _split_bf16(x, terms):
        t = jnp.dot(piece, sel, preferred_element_type=jnp.float32)
        acc = t if acc is None else acc + t
    return acc


def _sigmoid(x):
    return 1.0 / (1.0 + jnp.exp(-x))


def _chunk_totals(cum, d):
    last = CHUNK - 1 if d == 0 else 0
    return [cum[c * CHUNK + last:c * CHUNK + last + 1] for c in range(cum.shape[0] // CHUNK)]


def _expand_chunks(rows):
    return jnp.concatenate([jnp.broadcast_to(r, (CHUNK, r.shape[1])) for r in rows], axis=0)


def _in_rw_kernel(xc_ref, xp_ref, xn_ref, g_ref, wrw_ref, whg_ref,
                  mup_ref, mun_ref, w0_ref, w2_ref, a0_ref, a2_ref,
                  kk_ref, ka_ref, rk_ref, seg_ref, tri_ref,
                  phg_ref, ops_ref, v_ref, gend_ref, bonus_ref, gate_ref, *, n_tiles):
    j = pl.program_id(1)

    def pre_norm(x):
        return x * lax.rsqrt(jnp.mean(x * x, axis=-1, keepdims=True) + NORM_EPS) * g_ref[...]

    h_c = pre_norm(xc_ref[...])
    h_p = pre_norm(xp_ref[...]) * (j > 0).astype(jnp.float32)
    h_n = pre_norm(xn_ref[...]) * (j < n_tiles - 1).astype(jnp.float32)
    h = h_c.astype(BF16)
    h_ext = jnp.concatenate([h_p, h_c, h_n], axis=0).astype(BF16)

    hg_blocks = iter(range(HG_COLS // HG_WIDTH))

    def hg_block():
        i = next(hg_blocks, None)
        if i is not None:
            cols = slice(i * HG_WIDTH, (i + 1) * HG_WIDTH)
            phg_ref[:, cols] = jnp.dot(h, whg_ref[:, cols], preferred_element_type=jnp.float32)

    W = RW_WIDTH
    ext = ROW_TILE + 2 * HALO
    tile = slice(HALO, HALO + ROW_TILE)

    def shifted(cols):
        p_ext = jnp.dot(h_ext, wrw_ref[:, cols], preferred_element_type=jnp.float32)
        p, p_prev, p_next = p_ext[tile], pltpu.roll(p_ext, 1, 0)[tile], pltpu.roll(p_ext, ext - 1, 0)[tile]
        mup, mun = mup_ref[:, cols], mun_ref[:, cols]
        return p * (1.0 - mup - mun) + mup * p_prev + mun * p_next

    r, k, v, g = [shifted(slice(i * W, (i + 1) * W)) for i in range(4)]
    lora_in = shifted(slice(4 * W, RW_COLS))
    wd = lora_in[:, :2 * LORA]
    ad = lora_in[:, 2 * LORA:].astype(BF16)
    hg_block()
    gate_ref[...] = (g * _sigmoid(g)).astype(BF16)
    v_ref[...] = v.astype(BF16)

    seg = seg_ref[...]
    kk = k * kk_ref[...]
    kk = kk * lax.rsqrt(jnp.maximum(_select_cols(kk * kk, seg, 1), 1e-24))
    tanh_wd = jnp.tanh(wd).astype(BF16)
    hg_block()

    k_sum = None
    for d in range(N_DIR):
        w_raw = w0_ref[d:d + 1, :] + jnp.dot(tanh_wd, w2_ref[d], preferred_element_type=jnp.float32)
        lw = (-LOG2E * jnp.exp(jnp.float32(-0.5))) * _sigmoid(w_raw)
        a = _sigmoid(a0_ref[d:d + 1, :] + jnp.dot(ad, a2_ref[d], preferred_element_type=jnp.float32))
        k_d = k * (1.0 + (a - 1.0) * ka_ref[...])
        b_d = kk * a
        k_sum = k_d if k_sum is None else k_sum + k_d
        cl = _select_rows(tri_ref[d], lw, 2)
        g_end = [jnp.exp2(tot) for tot in _chunk_totals(cl, d)]
        hg_block()
        e_out = jnp.exp2(-cl)
        e_end = e_out * _expand_chunks(g_end)
        ops = (r * jnp.exp2(cl),
               kk * jnp.exp2(cl - lw),
               k_d * e_out, b_d * e_out,
               k_d * e_end, b_d * e_end)
        for i, op in enumerate(ops):
            ops_ref[d, :, i * W:(i + 1) * W] = op.astype(BF16)
        for q, g_q in enumerate(g_end):
            gend_ref[d, q] = jnp.broadcast_to(g_q, (HALO, W))
        hg_block()

    bonus_ref[...] = (_select_cols(r * k_sum * rk_ref[...], seg, 1) * v).astype(BF16)
    for _ in hg_blocks:
        raise AssertionError("HGRN2 column blocks left unissued")


def _in_rw(x, g, w_rw, w_hg, mup, mun, w0, w2, a0, a2, k_k, k_a, r_k, seg, tri):
    bsz, seq, _ = x.shape
    n_tiles = seq // ROW_TILE
    hb = ROW_TILE // HALO
    n_halo = seq // HALO
    n_chunks = seq // CHUNK
    cpt = ROW_TILE // CHUNK
    full = lambda a: pl.BlockSpec(a.shape, lambda b, j: (0,) * a.ndim)
    tile = lambda n: pl.BlockSpec((None, ROW_TILE, n), lambda b, j: (b, j, 0))
    consts = (g, w_rw, w_hg, mup, mun, w0, w2, a0, a2, k_k, k_a, r_k, seg, tri)
    return pl.pallas_call(
        functools.partial(_in_rw_kernel, n_tiles=n_tiles),
        grid=(bsz, n_tiles),
        in_specs=[
            tile(D_MODEL),
            pl.BlockSpec((None, HALO, D_MODEL), lambda b, j: (b, jnp.maximum(j * hb - 1, 0), 0)),
            pl.BlockSpec((None, HALO, D_MODEL), lambda b, j: (b, jnp.minimum((j + 1) * hb, n_halo - 1), 0)),
            *[full(a) for a in consts],
        ],
        out_specs=[
            tile(HG_COLS),
            pl.BlockSpec((N_DIR, None, ROW_TILE, RW_OPS * RW_WIDTH), lambda b, j: (0, b, j, 0)),
            tile(RW_WIDTH),
            pl.BlockSpec((N_DIR, None, cpt, HALO, RW_WIDTH), lambda b, j: (0, b, j, 0, 0)),
            tile(RW_WIDTH),
            tile(RW_WIDTH),
        ],
        out_shape=[
            jax.ShapeDtypeStruct((bsz, seq, HG_COLS), jnp.float32),
            jax.ShapeDtypeStruct((N_DIR, bsz, seq, RW_OPS * RW_WIDTH), BF16),
            jax.ShapeDtypeStruct((bsz, seq, RW_WIDTH), BF16),
            jax.ShapeDtypeStruct((N_DIR, bsz, n_chunks, HALO, RW_WIDTH), jnp.float32),
            jax.ShapeDtypeStruct((bsz, seq, RW_WIDTH), BF16),
            jax.ShapeDtypeStruct((bsz, seq, RW_WIDTH), BF16),
        ],
        compiler_params=pltpu.CompilerParams(
            dimension_semantics=("arbitrary", "arbitrary"), vmem_limit_bytes=VMEM_LIMIT),
        name="in_proj_rwkv7_prep",
    )(x, x, x, *consts)


def _rw_scan_kernel(of_ref, ob_ref, vf_ref, vb_ref, gf_ref, gb_ref, yf_ref, yb_ref, s_ref):
    c = pl.program_id(1)

    @pl.when(c == 0)
    def _():
        s_ref[...] = jnp.zeros_like(s_ref)

    N, W = RW_HEAD_DIM, RW_WIDTH
    lane = lax.broadcasted_iota(jnp.int32, (CHUNK, LANES), 1)
    tt = lax.broadcasted_iota(jnp.int32, (CHUNK, LANES), 0)
    half = (lane < N, lane >= N)
    diag = ((lax.broadcasted_iota(jnp.int32, (LANES, LANES), 0) < N)
            == (lax.broadcasted_iota(jnp.int32, (LANES, LANES), 1) < N))
    dir_refs = ((of_ref, vf_ref, gf_ref, yf_ref), (ob_ref, vb_ref, gb_ref, yb_ref))

    pairs = [(q, d, j) for q in range(SCAN_BATCH) for d in range(N_DIR) for j in range(W // LANES)]
    probs = [(pi, par) for pi in range(len(pairs)) for par in range(2)]

    def op(pi, i):
        q, d, j = pairs[pi]
        return dir_refs[d][0][q, :, i * W + j * LANES:i * W + (j + 1) * LANES]

    r_p = [op(pi, 0) for pi in range(len(pairs))]
    a_p = [op(pi, 1) for pi in range(len(pairs))]
    k_p = [op(pi, 2) for pi in range(len(pairs))]
    b_p = [op(pi, 3) for pi in range(len(pairs))]
    ke_p = [op(pi, 4) for pi in range(len(pairs))]
    be_p = [op(pi, 5) for pi in range(len(pairs))]
    v_p = [dir_refs[d][1][q, :, j * LANES:(j + 1) * LANES] for q, d, j in pairs]
    s_p = [s_ref[q, d, j] for q, d, j in pairs]

    def lag(pi):
        d = pairs[pi][1]
        delta = tt - (lane & (N - 1))
        return delta if d == 0 else -delta

    gram = []
    for pi, par in probs:
        lhs = jnp.concatenate([a_p[pi], r_p[pi]], axis=0)
        lhs = jnp.where(jnp.concatenate([half[par], half[par]], axis=0), lhs, jnp.zeros_like(lhs))
        rhs = jnp.concatenate([b_p[pi], k_p[pi]], axis=0)
        gram.append(lax.dot_general(lhs, rhs, _NT, preferred_element_type=jnp.float32))

    za, zb, bot = [], [], []
    for n, (pi, par) in enumerate(probs):
        top = gram[n][:CHUNK]
        lg = lag(pi)
        m0 = jnp.where(jnp.where(half[0], lg, 0) > 0, -top, 0.0)
        l_ak = jnp.where(jnp.where(half[1], lg, 0) > 0, top, 0.0)
        bot.append(jnp.where(lg >= 0, gram[n][CHUNK:], 0.0).astype(BF16))
        vv = jnp.concatenate([v_p[pi], v_p[pi]], axis=0)
        zb.append(jnp.dot(l_ak.astype(BF16), vv, preferred_element_type=jnp.float32))
        a_own = a_p[pi].astype(jnp.float32)
        za.append(jnp.where(half[1], a_own if par == 1 else pltpu.roll(a_own, N, 1), m0))

    for level in range(6):
        skip = (2 ** level) // LHS_ROWS * LHS_ROWS
        new_za, new_zb = [], []
        for n, (pi, par) in enumerate(probs):
            live = slice(skip, CHUNK) if pairs[pi][1] == 0 else slice(0, CHUNK - skip)
            m = za[n][live, :N].astype(BF16)
            z2 = jnp.concatenate([za[n], zb[n]], axis=1).astype(BF16)
            p = jnp.dot(m, z2, preferred_element_type=jnp.float32)
            keep_a = jnp.where(half[1], za[n], 0.0)
            pieces_a = [keep_a[:live.start], p[:, :LANES] + keep_a[live], keep_a[live.stop:]]
            pieces_b = [zb[n][:live.start], p[:, LANES:] + zb[n][live], zb[n][live.stop:]]
            new_za.append(jnp.concatenate([x for x in pieces_a if x.shape[0]], axis=0))
            new_zb.append(jnp.concatenate([x for x in pieces_b if x.shape[0]], axis=0))
        za, zb = new_za, new_zb

    n_pairs = range(len(pairs))
    s_bf = [s_p[pi].astype(BF16) for pi in n_pairs]
    wr = [jnp.concatenate([jnp.where(half[0], pltpu.roll(za[2 * pi], N, 1), za[2 * pi + 1]).astype(BF16),
                           r_p[pi]], axis=0) for pi in n_pairs]
    ws = [lax.dot_general(wr[pi], s_bf[pi], _NT, preferred_element_type=jnp.float32) for pi in n_pairs]
    u_bf = [(-(ws[pi][:CHUNK] + jnp.where(half[0], zb[2 * pi], zb[2 * pi + 1]))).astype(BF16)
            for pi in n_pairs]
    uv = [jnp.concatenate([u_bf[pi], v_p[pi]], axis=0) for pi in n_pairs]
    y_eo = [jnp.dot(jnp.concatenate([bot[2 * pi], bot[2 * pi + 1]], axis=0), uv[pi],
                    preferred_element_type=jnp.float32) for pi in n_pairs]
    upd = [lax.dot_general(uv[pi], jnp.concatenate([be_p[pi], ke_p[pi]], axis=0), _TN,
                           preferred_element_type=jnp.float32) for pi in n_pairs]
    for pi, (q, d, j) in enumerate(pairs):
        y = ws[pi][CHUNK:] + jnp.where(half[0], y_eo[pi][:CHUNK], y_eo[pi][CHUNK:])
        dir_refs[d][3][q, :, j * LANES:(j + 1) * LANES] = y.astype(BF16)
        g_end = dir_refs[d][2][q, 0:1, j * LANES:(j + 1) * LANES]
        s_ref[q, d, j] = s_p[pi] * g_end + jnp.where(diag, upd[pi], 0.0)


def _rw_scan(ops, v_bf, gend):
    _, bsz, seq, _ = ops.shape
    n_chunks = seq // CHUNK
    rev = lambda c: n_chunks - 1 - c
    return pl.pallas_call(
        _rw_scan_kernel,
        grid=(bsz // SCAN_BATCH, n_chunks),
        in_specs=[
            pl.BlockSpec((None, SCAN_BATCH, CHUNK, RW_OPS * RW_WIDTH), lambda b, c: (0, b, c, 0)),
            pl.BlockSpec((None, SCAN_BATCH, CHUNK, RW_OPS * RW_WIDTH), lambda b, c: (1, b, rev(c), 0)),
            pl.BlockSpec((SCAN_BATCH, CHUNK, RW_WIDTH), lambda b, c: (b, c, 0)),
            pl.BlockSpec((SCAN_BATCH, CHUNK, RW_WIDTH), lambda b, c: (b, rev(c), 0)),
            pl.BlockSpec((None, SCAN_BATCH, None, HALO, RW_WIDTH), lambda b, c: (0, b, c, 0, 0)),
            pl.BlockSpec((None, SCAN_BATCH, None, HALO, RW_WIDTH), lambda b, c: (1, b, rev(c), 0, 0)),
        ],
        out_specs=[
            pl.BlockSpec((SCAN_BATCH, CHUNK, RW_WIDTH), lambda b, c: (b, c, 0)),
            pl.BlockSpec((SCAN_BATCH, CHUNK, RW_WIDTH), lambda b, c: (b, rev(c), 0)),
        ],
        out_shape=[jax.ShapeDtypeStruct((bsz, seq, RW_WIDTH), BF16)] * 2,
        scratch_shapes=[pltpu.VMEM((SCAN_BATCH, N_DIR, RW_WIDTH // LANES, LANES, LANES), jnp.float32)],
        compiler_params=pltpu.CompilerParams(
            dimension_semantics=("arbitrary", "arbitrary"), vmem_limit_bytes=VMEM_LIMIT),
        name="rwkv7_scan",
    )(ops, ops, v_bf, v_bf, gend, gend)


def _hg_kernel(qf_ref, ff_ref, if_ref, qb_ref, fb_ref, ib_ref, lbl_ref, tri_ref,
               of_ref, ob_ref, s_ref, cs_ref, vs_ref, *, layer):
    c = pl.program_id(1)

    @pl.when(c == 0)
    def _():
        s_ref[...] = jnp.zeros_like(s_ref)

    lg = lbl_ref[...]
    e = jnp.exp(lg - jnp.max(lg, axis=0, keepdims=True))
    lb = jnp.sum(e[0:layer + 1, :], axis=0, keepdims=True) / jnp.sum(e, axis=0, keepdims=True)

    heads = range(HG_HEADS)
    sls = [slice(h * HG_DK, (h + 1) * HG_DK) for h in heads]
    trow = lax.broadcasted_iota(jnp.int32, (SUB, HG_DK), 0)
    dir_refs = ((qf_ref, ff_ref, if_ref, of_ref), (qb_ref, fb_ref, ib_ref, ob_ref))
    probs = [(n, d) for n in range(SCAN_BATCH) for d in range(N_DIR)]

    def matmul_part(k):
        n, d = probs[k]
        q_ref, f_ref, i_ref, _ = dir_refs[d]
        q, v = q_ref[n], i_ref[n]
        v_bf = v.astype(BF16)
        f = lb + (1.0 - lb) * _sigmoid(f_ref[n])
        kf = 1.0 - f
        b2 = _select_rows(tri_ref[d], jnp.log2(f), 3)
        (total,) = _chunk_totals(b2, d)
        q_in = (q * jnp.exp2(b2)).astype(BF16)
        k_end = (kf * jnp.exp2(total - b2)).astype(BF16)
        g_end = jnp.exp2(total)
        cs = b2 - jnp.log2(kf)
        for h in heads:
            cs_ref[k, h] = cs[:, sls[h]]
            vs_ref[k, h] = v[:, sls[h]]

        st = [s_ref[n, d, h] for h in heads]
        inter = [lax.dot_general(q_in[:, sls[h]], st[h].astype(BF16), _NT, preferred_element_type=jnp.float32)
                 for h in heads]
        for h in heads:
            s_ref[n, d, h] = st[h] * g_end[:, sls[h]] + lax.dot_general(
                v_bf[:, sls[h]], k_end[:, sls[h]], _TN, preferred_element_type=jnp.float32)

        def rows(p_lo, p_hi):
            lo = p_lo if d == 0 else CHUNK - p_hi
            return slice(lo, lo + (p_hi - p_lo))

        pending = []
        size = CHUNK // 2
        while size >= SUB:
            for p in range(0, CHUNK, 2 * size):
                s_sl, t_sl = rows(p, p + size), rows(p + size, p + 2 * size)
                ref = b2[rows(p + size - 1, p + size)]
                q_t = (q[t_sl] * jnp.exp2(b2[t_sl] - ref)).astype(BF16)
                k_s = (kf[s_sl] * jnp.exp2(ref - b2[s_sl])).astype(BF16)
                scores = [lax.dot_general(q_t[:, sls[h]], k_s[:, sls[h]], _NT,
                                          preferred_element_type=jnp.float32).astype(BF16) for h in heads]
                pending.append((t_sl, s_sl, scores))
            size //= 2
        return q, b2, v_bf, inter, pending

    def vpu_part(k, q, b2, v_bf, inter, pending):
        n, d = probs[k]
        o_ref = dir_refs[d][3]
        for lo in range(0, CHUNK, SUB):
            q_i, b_i = q[lo:lo + SUB], b2[lo:lo + SUB]
            acc = [inter[h][lo:lo + SUB] for h in heads]
            for s in range(SUB):
                row_s = pl.ds(lo + s, SUB, stride=0)
                m = (trow >= s) if d == 0 else (trow <= s)
                for h in heads:
                    xs = jnp.where(m, q_i[:, sls[h]] * jnp.exp2(b_i[:, sls[h]] - cs_ref[k, h, row_s, :]), 0.0)
                    w = jnp.sum(xs, axis=-1, keepdims=True)
                    acc[h] = acc[h] + w * vs_ref[k, h, row_s, :]
            for h in heads:
                o_ref[n, lo:lo + SUB, sls[h]] = acc[h]
        for t_sl, s_sl, scores in pending:
            for h in heads:
                o_ref[n, t_sl, sls[h]] += jnp.dot(scores[h], v_bf[s_sl, sls[h]], preferred_element_type=jnp.float32)

    ctx = matmul_part(0)
    for k in range(len(probs)):
        nxt = matmul_part(k + 1) if k + 1 < len(probs) else None
        vpu_part(k, *ctx)
        ctx = nxt


def _hg_mix(p_hg, lb_logits, layer, tri):
    bsz, seq, _ = p_hg.shape
    n_chunks = seq // CHUNK
    rev = lambda c: n_chunks - 1 - c
    blk = (SCAN_BATCH, CHUNK, HG_WIDTH)
    fwd = lambda j: pl.BlockSpec(blk, lambda g, c: (g, c, j))
    bwd = lambda j: pl.BlockSpec(blk, lambda g, c: (g, rev(c), j))
    full = lambda a: pl.BlockSpec(a.shape, lambda g, c: (0,) * a.ndim)
    n_prob = SCAN_BATCH * N_DIR
    return pl.pallas_call(
        functools.partial(_hg_kernel, layer=layer),
        grid=(bsz // SCAN_BATCH, n_chunks),
        in_specs=[fwd(0), fwd(1), fwd(3), bwd(0), bwd(2), bwd(3), full(lb_logits), full(tri)],
        out_specs=[fwd(0), bwd(0)],
        out_shape=[jax.ShapeDtypeStruct((bsz, seq, HG_WIDTH), jnp.float32)] * 2,
        scratch_shapes=[pltpu.VMEM((SCAN_BATCH, N_DIR, HG_HEADS, HG_DK, HG_DK), jnp.float32),
                        pltpu.VMEM((n_prob, HG_HEADS, CHUNK, HG_DK), jnp.float32),
                        pltpu.VMEM((n_prob, HG_HEADS, CHUNK, HG_DK), jnp.float32)],
        compiler_params=pltpu.CompilerParams(
            dimension_semantics=("arbitrary", "arbitrary"), vmem_limit_bytes=VMEM_LIMIT),
        name="hgrn2_mix",
    )(p_hg, p_hg, p_hg, p_hg, p_hg, p_hg, lb_logits, tri)


def _out_kernel(x_ref, yf_ref, yb_ref, bonus_ref, gate_ref, of_ref, ob_ref, g_ref,
                lnw_ref, lnb_ref, hgn_ref, seg_ref, w_ref, post_ref, out_ref):
    y = yf_ref[...].astype(jnp.float32) + yb_ref[...].astype(jnp.float32)
    seg = seg_ref[...]
    mu = _select_cols(y, seg, 1) * (1.0 / RW_HEAD_DIM)
    yc = y - mu
    var = _select_cols(yc * yc, seg, 1) * (1.0 / RW_HEAD_DIM)
    y = yc * lax.rsqrt(var + GN_EPS) * lnw_ref[...] + lnb_ref[...]
    rw = (y + bonus_ref[...].astype(jnp.float32)) * gate_ref[...].astype(jnp.float32)

    o = of_ref[...] + ob_ref[...]
    g = g_ref[...]
    parts = []
    for h in range(HG_HEADS):
        oh = o[:, h * HG_DK:(h + 1) * HG_DK]
        parts.append(oh * lax.rsqrt(jnp.mean(oh * oh, axis=-1, keepdims=True) + NORM_EPS))
    hg = jnp.concatenate(parts, axis=1) * hgn_ref[...] * (g * _sigmoid(g))

    cat = jnp.concatenate([rw, hg], axis=1).astype(BF16)
    yo = jnp.dot(cat, w_ref[...], preferred_element_type=jnp.float32)
    ms = jnp.mean(yo * yo, axis=-1, keepdims=True)
    out_ref[...] = x_ref[...] + yo * lax.rsqrt(ms + NORM_EPS) * post_ref[...]


def _out_proj(x2, y_f, y_b, bonus, gate, o_f, o_b, p_hg2, ln_w, ln_b, hg_norm, seg, w_bf16, post_g):
    rows = x2.shape[0]
    tile = lambda n: pl.BlockSpec((ROW_TILE, n), lambda i: (i, 0))
    row_spec = lambda n: pl.BlockSpec((1, n), lambda i: (0, 0))
    return pl.pallas_call(
        _out_kernel,
        grid=(rows // ROW_TILE,),
        in_specs=[
            tile(D_MODEL),
            tile(RW_WIDTH), tile(RW_WIDTH), tile(RW_WIDTH), tile(RW_WIDTH),
            tile(HG_WIDTH), tile(HG_WIDTH),
            pl.BlockSpec((ROW_TILE, HG_WIDTH), lambda i: (i, 4)),
            row_spec(RW_WIDTH), row_spec(RW_WIDTH), row_spec(HG_WIDTH),
            pl.BlockSpec((RW_WIDTH, RW_WIDTH), lambda i: (0, 0)),
            pl.BlockSpec((D_MODEL, D_MODEL), lambda i: (0, 0)),
            row_spec(D_MODEL),
        ],
        out_specs=tile(D_MODEL),
        out_shape=jax.ShapeDtypeStruct((rows, D_MODEL), jnp.float32),
        compiler_params=pltpu.CompilerParams(
            dimension_semantics=("arbitrary",), vmem_limit_bytes=VMEM_LIMIT),
        name="out_proj",
    )(x2, y_f, y_b, bonus, gate, o_f, o_b, p_hg2, ln_w, ln_b, hg_norm, seg, w_bf16, post_g)


def _lora_weights(w):
    z = jnp.zeros_like(w[0])
    wp = jnp.stack([jnp.concatenate([w[0], z], axis=0), jnp.concatenate([z, w[1]], axis=0)])
    return wp.astype(BF16)


def _scan_order_selectors(tile):
    t = jnp.arange(tile)
    same = (t[:, None] // CHUNK) == (t[None, :] // CHUNK)
    fwd = same & (t[None, :] <= t[:, None])
    bwd = same & (t[None, :] >= t[:, None])
    return jnp.stack([fwd, bwd]).astype(BF16)


def kernel(x, pre_norm_g, w_in, rw_shift_prev, rw_shift_next, rw_w0, rw_w2, rw_a0, rw_a2, rw_k_k, rw_k_a,
           rw_r_k, rw_ln_w, rw_ln_b, hg_lb_logits, hg_norm_g, w_out, post_norm_g):
    bsz, seq, dm = x.shape
    depth = w_in.shape[0]
    rows = bsz * seq
    lane = jnp.arange(RW_WIDTH) // RW_HEAD_DIM
    seg = (lane[:, None] == lane[None, :]).astype(BF16)
    tri = _scan_order_selectors(ROW_TILE)
    hg_tri = _scan_order_selectors(CHUNK)
    row = lambda t: t.reshape(1, -1)
    flat = lambda t: t.reshape(rows, t.shape[-1])
    for l in range(depth):
        x2 = x.reshape(rows, dm)
        w_bf = w_in[l].astype(BF16)
        p_hg, ops, v_bf, gend, bonus, gate = _in_rw(
            x, row(pre_norm_g[l]), w_bf[:, :RW_COLS], w_bf[:, RW_COLS:],
            row(rw_shift_prev[l]), row(rw_shift_next[l]),
            rw_w0[l], _lora_weights(rw_w2[l]), rw_a0[l], _lora_weights(rw_a2[l]),
            row(rw_k_k[l]), row(rw_k_a[l]), row(rw_r_k[l]), seg, tri)
        y_f, y_b = _rw_scan(ops, v_bf, gend)
        o_f, o_b = _hg_mix(p_hg, hg_lb_logits, l, hg_tri)
        out = _out_proj(
            x2, flat(y_f), flat(y_b), flat(bonus), flat(gate), flat(o_f), flat(o_b), flat(p_hg),
            row(rw_ln_w[l]), row(rw_ln_b[l]), row(jnp.tile(hg_norm_g[l], HG_HEADS)), seg,
            w_out[l].astype(BF16), row(post_norm_g[l]))
        x = out.reshape(bsz, seq, dm)
    return x
```

```python
import functools

import jax
import jax.numpy as jnp
from jax import lax
from jax.experimental import pallas as pl
from jax.experimental.pallas import tpu as pltpu

D_MODEL = 1024
RW_HEAD_DIM = 64
RW_WIDTH = 512
RW_HEADS = 8
HG_DK = 128
HG_WIDTH = 512
HG_HEADS = 4
LORA = 64
NORM_EPS = 1e-6
GN_EPS = 64e-5
RW_COLS = 4 * RW_WIDTH + 4 * LORA
HG_COLS = 5 * HG_WIDTH
HG_SCAN_COLS = 4 * HG_WIDTH
IN_COLS = RW_COLS + HG_COLS
N_DIR = 2
RW_OPS = 6

LANES = 128
LHS_ROWS = 16
CHUNK = 64
SUB = 8
ROW_TILE = 256
OUT_TILE = 512
HALO = 8
SCAN_BATCH = 4
VMEM_LIMIT = 56 * 1024 * 1024

_NN = (((1,), (0,)), ((), ()))
_NT = (((1,), (1,)), ((), ()))
_TN = (((0,), (0,)), ((), ()))
BF16 = jnp.bfloat16
LOG2E = 1.4426950408889634


def _split_bf16(x, terms=2):
    pieces = []
    for _ in range(terms - 1):
        hi = x.astype(BF16)
        pieces.append(hi)
        x = x - hi.astype(jnp.float32)
    pieces.append(x.astype(BF16))
    return pieces


def _select_rows(sel, x, terms):
    acc = None
    for piece in _split_bf16(x, terms):
        t = jnp.dot(sel, piece, preferred_element_type=jnp.float32)
        acc = t if acc is None else acc + t
    return acc


def _select_cols(x, sel, terms):
    acc = None
    for piece in _split_bf16(x, terms):
        t = jnp.dot(piece, sel, preferred_element_type=jnp.float32)
        acc = t if acc is None else acc + t
    return acc


def _sigmoid(x):
    return 1.0 / (1.0 + jnp.exp(-x))


def _chunk_totals(cum, d):
    last = CHUNK - 1 if d == 0 else 0
    return [cum[c * CHUNK + last:c * CHUNK + last + 1] for c in range(cum.shape[0] // CHUNK)]


def _expand_chunks(rows):
    return jnp.concatenate([jnp.broadcast_to(r, (CHUNK, r.shape[1])) for r in rows], axis=0)


def _in_rw_kernel(xc_ref, xp_ref, xn_ref, g_ref, wrw_ref, whg_ref,
                  mup_ref, mun_ref, w0_ref, w2_ref, a0_ref, a2_ref,
                  kk_ref, ka_ref, rk_ref, seg_ref, tri_ref,
                  phg_ref, hgate_ref, ops_ref, v_ref, gend_ref, bonus_ref, gate_ref, *, n_tiles):
    j = pl.program_id(1)

    def pre_norm(x):
        return x * lax.rsqrt(jnp.mean(x * x, axis=-1, keepdims=True) + NORM_EPS) * g_ref[...]

    h_c = pre_norm(xc_ref[...])
    h_p = pre_norm(xp_ref[...]) * (j > 0).astype(jnp.float32)
    h_n = pre_norm(xn_ref[...]) * (j < n_tiles - 1).astype(jnp.float32)
    h = h_c.astype(BF16)
    h_ext = jnp.concatenate([h_p, h_c, h_n], axis=0).astype(BF16)

    hg_blocks = iter(range(HG_COLS // HG_WIDTH))

    def hg_block():
        i = next(hg_blocks, None)
        if i is not None:
            cols = slice(i * HG_WIDTH, (i + 1) * HG_WIDTH)
            blk = jnp.dot(h, whg_ref[:, cols], preferred_element_type=jnp.float32)
            if i < HG_SCAN_COLS // HG_WIDTH:
                phg_ref[:, cols] = blk
            else:
                hgate_ref[...] = (blk * _sigmoid(blk)).astype(BF16)

    W = RW_WIDTH
    ext = ROW_TILE + 2 * HALO
    tile = slice(HALO, HALO + ROW_TILE)

    def shifted(cols):
        p_ext = jnp.dot(h_ext, wrw_ref[:, cols], preferred_element_type=jnp.float32)
        p, p_prev, p_next = p_ext[tile], pltpu.roll(p_ext, 1, 0)[tile], pltpu.roll(p_ext, ext - 1, 0)[tile]
        mup, mun = mup_ref[:, cols], mun_ref[:, cols]
        return p * (1.0 - mup - mun) + mup * p_prev + mun * p_next

    r, k, v, g = [shifted(slice(i * W, (i + 1) * W)) for i in range(4)]
    lora_in = shifted(slice(4 * W, RW_COLS))
    wd = lora_in[:, :2 * LORA]
    ad = lora_in[:, 2 * LORA:].astype(BF16)
    hg_block()
    gate_ref[...] = (g * _sigmoid(g)).astype(BF16)
    v_ref[...] = v.astype(BF16)

    seg = seg_ref[...]
    kk = k * kk_ref[...]
    kk = kk * lax.rsqrt(jnp.maximum(_select_cols(kk * kk, seg, 1), 1e-24))
    tanh_wd = jnp.tanh(wd).astype(BF16)
    hg_block()

    k_sum = None
    for d in range(N_DIR):
        w_raw = w0_ref[d:d + 1, :] + jnp.dot(tanh_wd, w2_ref[d], preferred_element_type=jnp.float32)
        lw = (-LOG2E * jnp.exp(jnp.float32(-0.5))) * _sigmoid(w_raw)
        a = _sigmoid(a0_ref[d:d + 1, :] + jnp.dot(ad, a2_ref[d], preferred_element_type=jnp.float32))
        k_d = k * (1.0 + (a - 1.0) * ka_ref[...])
        b_d = kk * a
        k_sum = k_d if k_sum is None else k_sum + k_d
        cl = _select_rows(tri_ref[d], lw, 2)
        g_end = [jnp.exp2(tot) for tot in _chunk_totals(cl, d)]
        hg_block()
        e_out = jnp.exp2(-cl)
        e_end = e_out * _expand_chunks(g_end)
        ops = (r * jnp.exp2(cl),
               kk * jnp.exp2(cl - lw),
               k_d * e_out, b_d * e_out,
               k_d * e_end, b_d * e_end)
        for i, op in enumerate(ops):
            ops_ref[d, :, i * W:(i + 1) * W] = op.astype(BF16)
        for q, g_q in enumerate(g_end):
            gend_ref[d, q] = jnp.broadcast_to(g_q, (HALO, W))
        hg_block()

    bonus_ref[...] = (_select_cols(r * k_sum * rk_ref[...], seg, 1) * v).astype(BF16)
    for _ in hg_blocks:
        raise AssertionError("HGRN2 column blocks left unissued")


def _in_rw(x, g, w_rw, w_hg, mup, mun, w0, w2, a0, a2, k_k, k_a, r_k, seg, tri):
    bsz, seq, _ = x.shape
    n_tiles = seq // ROW_TILE
    hb = ROW_TILE // HALO
    n_halo = seq // HALO
    n_chunks = seq // CHUNK
    cpt = ROW_TILE // CHUNK
    full = lambda a: pl.BlockSpec(a.shape, lambda b, j: (0,) * a.ndim)
    tile = lambda n: pl.BlockSpec((None, ROW_TILE, n), lambda b, j: (b, j, 0))
    consts = (g, w_rw, w_hg, mup, mun, w0, w2, a0, a2, k_k, k_a, r_k, seg, tri)
    return pl.pallas_call(
        functools.partial(_in_rw_kernel, n_tiles=n_tiles),
        grid=(bsz, n_tiles),
        in_specs=[
            tile(D_MODEL),
            pl.BlockSpec((None, HALO, D_MODEL), lambda b, j: (b, jnp.maximum(j * hb - 1, 0), 0)),
            pl.BlockSpec((None, HALO, D_MODEL), lambda b, j: (b, jnp.minimum((j + 1) * hb, n_halo - 1), 0)),
            *[full(a) for a in consts],
        ],
        out_specs=[
            tile(HG_SCAN_COLS),
            tile(HG_WIDTH),
            pl.BlockSpec((N_DIR, None, ROW_TILE, RW_OPS * RW_WIDTH), lambda b, j: (0, b, j, 0)),
            tile(RW_WIDTH),
            pl.BlockSpec((N_DIR, None, cpt, HALO, RW_WIDTH), lambda b, j: (0, b, j, 0, 0)),
            tile(RW_WIDTH),
            tile(RW_WIDTH),
        ],
        out_shape=[
            jax.ShapeDtypeStruct((bsz, seq, HG_SCAN_COLS), jnp.float32),
            jax.ShapeDtypeStruct((bsz, seq, HG_WIDTH), BF16),
            jax.ShapeDtypeStruct((N_DIR, bsz, seq, RW_OPS * RW_WIDTH), BF16),
            jax.ShapeDtypeStruct((bsz, seq, RW_WIDTH), BF16),
            jax.ShapeDtypeStruct((N_DIR, bsz, n_chunks, HALO, RW_WIDTH), jnp.float32),
            jax.ShapeDtypeStruct((bsz, seq, RW_WIDTH), BF16),
            jax.ShapeDtypeStruct((bsz, seq, RW_WIDTH), BF16),
        ],
        compiler_params=pltpu.CompilerParams(
            dimension_semantics=("arbitrary", "arbitrary"), vmem_limit_bytes=VMEM_LIMIT),
        name="in_proj_rwkv7_prep",
    )(x, x, x, *consts)


def _rw_scan_kernel(of_ref, ob_ref, vf_ref, vb_ref, gf_ref, gb_ref, yf_ref, yb_ref, s_ref):
    c = pl.program_id(1)

    @pl.when(c == 0)
    def _():
        s_ref[...] = jnp.zeros_like(s_ref)

    N, W = RW_HEAD_DIM, RW_WIDTH
    lane = lax.broadcasted_iota(jnp.int32, (CHUNK, LANES), 1)
    tt = lax.broadcasted_iota(jnp.int32, (CHUNK, LANES), 0)
    half = (lane < N, lane >= N)
    diag = ((lax.broadcasted_iota(jnp.int32, (LANES, LANES), 0) < N)
            == (lax.broadcasted_iota(jnp.int32, (LANES, LANES), 1) < N))
    dir_refs = ((of_ref, vf_ref, gf_ref, yf_ref), (ob_ref, vb_ref, gb_ref, yb_ref))

    pairs = [(q, d, j) for q in range(SCAN_BATCH) for d in range(N_DIR) for j in range(W // LANES)]
    probs = [(pi, par) for pi in range(len(pairs)) for par in range(2)]

    def op(pi, i):
        q, d, j = pairs[pi]
        return dir_refs[d][0][q, :, i * W + j * LANES:i * W + (j + 1) * LANES]

    r_p = [op(pi, 0) for pi in range(len(pairs))]
    a_p = [op(pi, 1) for pi in range(len(pairs))]
    k_p = [op(pi, 2) for pi in range(len(pairs))]
    b_p = [op(pi, 3) for pi in range(len(pairs))]
    ke_p = [op(pi, 4) for pi in range(len(pairs))]
    be_p = [op(pi, 5) for pi in range(len(pairs))]
    v_p = [dir_refs[d][1][q, :, j * LANES:(j + 1) * LANES] for q, d, j in pairs]
    s_p = [s_ref[q, d, j] for q, d, j in pairs]

    def lag(pi):
        d = pairs[pi][1]
        delta = tt - (lane & (N - 1))
        return delta if d == 0 else -delta

    gram = []
    for pi, par in probs:
        lhs = jnp.concatenate([a_p[pi], r_p[pi]], axis=0)
        lhs = jnp.where(jnp.concatenate([half[par], half[par]], axis=0), lhs, jnp.zeros_like(lhs))
        rhs = jnp.concatenate([b_p[pi], k_p[pi]], axis=0)
        gram.append(lax.dot_general(lhs, rhs, _NT, preferred_element_type=jnp.float32))

    za, zb, bot = [], [], []
    for n, (pi, par) in enumerate(probs):
        top = gram[n][:CHUNK]
        lg = lag(pi)
        m0 = jnp.where(jnp.where(half[0], lg, 0) > 0, -top, 0.0)
        l_ak = jnp.where(jnp.where(half[1], lg, 0) > 0, top, 0.0)
        bot.append(jnp.where(lg >= 0, gram[n][CHUNK:], 0.0).astype(BF16))
        vv = jnp.concatenate([v_p[pi], v_p[pi]], axis=0)
        zb.append(jnp.dot(l_ak.astype(BF16), vv, preferred_element_type=jnp.float32))
        a_own = a_p[pi].astype(jnp.float32)
        za.append(jnp.where(half[1], a_own if par == 1 else pltpu.roll(a_own, N, 1), m0))

    for level in range(6):
        skip = (2 ** level) // LHS_ROWS * LHS_ROWS
        new_za, new_zb = [], []
        for n, (pi, par) in enumerate(probs):
            live = slice(skip, CHUNK) if pairs[pi][1] == 0 else slice(0, CHUNK - skip)
            m = za[n][live, :N].astype(BF16)
            z2 = jnp.concatenate([za[n], zb[n]], axis=1).astype(BF16)
            p = jnp.dot(m, z2, preferred_element_type=jnp.float32)
            keep_a = jnp.where(half[1], za[n], 0.0)
            pieces_a = [keep_a[:live.start], p[:, :LANES] + keep_a[live], keep_a[live.stop:]]
            pieces_b = [zb[n][:live.start], p[:, LANES:] + zb[n][live], zb[n][live.stop:]]
            new_za.append(jnp.concatenate([x for x in pieces_a if x.shape[0]], axis=0))
            new_zb.append(jnp.concatenate([x for x in pieces_b if x.shape[0]], axis=0))
        za, zb = new_za, new_zb

    n_pairs = range(len(pairs))
    s_bf = [s_p[pi].astype(BF16) for pi in n_pairs]
    wr = [jnp.concatenate([jnp.where(half[0], pltpu.roll(za[2 * pi], N, 1), za[2 * pi + 1]).astype(BF16),
                           r_p[pi]], axis=0) for pi in n_pairs]
    ws = [lax.dot_general(wr[pi], s_bf[pi], _NT, preferred_element_type=jnp.float32) for pi in n_pairs]
    u_bf = [(-(ws[pi][:CHUNK] + jnp.where(half[0], zb[2 * pi], zb[2 * pi + 1]))).astype(BF16)
            for pi in n_pairs]
    uv = [jnp.concatenate([u_bf[pi], v_p[pi]], axis=0) for pi in n_pairs]
    y_eo = [jnp.dot(jnp.concatenate([bot[2 * pi], bot[2 * pi + 1]], axis=0), uv[pi],
                    preferred_element_type=jnp.float32) for pi in n_pairs]
    upd = [lax.dot_general(uv[pi], jnp.concatenate([be_p[pi], ke_p[pi]], axis=0), _TN,
                           preferred_element_type=jnp.float32) for pi in n_pairs]
    for pi, (q, d, j) in enumerate(pairs):
        y = ws[pi][CHUNK:] + jnp.where(half[0], y_eo[pi][:CHUNK], y_eo[pi][CHUNK:])
        dir_refs[d][3][q, :, j * LANES:(j + 1) * LANES] = y.astype(BF16)
        g_end = dir_refs[d][2][q, 0:1, j * LANES:(j + 1) * LANES]
        s_ref[q, d, j] = s_p[pi] * g_end + jnp.where(diag, upd[pi], 0.0)


def _rw_scan(ops, v_bf, gend):
    _, bsz, seq, _ = ops.shape
    n_chunks = seq // CHUNK
    rev = lambda c: n_chunks - 1 - c
    return pl.pallas_call(
        _rw_scan_kernel,
        grid=(bsz // SCAN_BATCH, n_chunks),
        in_specs=[
            pl.BlockSpec((None, SCAN_BATCH, CHUNK, RW_OPS * RW_WIDTH), lambda b, c: (0, b, c, 0)),
            pl.BlockSpec((None, SCAN_BATCH, CHUNK, RW_OPS * RW_WIDTH), lambda b, c: (1, b, rev(c), 0)),
            pl.BlockSpec((SCAN_BATCH, CHUNK, RW_WIDTH), lambda b, c: (b, c, 0)),
            pl.BlockSpec((SCAN_BATCH, CHUNK, RW_WIDTH), lambda b, c: (b, rev(c), 0)),
            pl.BlockSpec((None, SCAN_BATCH, None, HALO, RW_WIDTH), lambda b, c: (0, b, c, 0, 0)),
            pl.BlockSpec((None, SCAN_BATCH, None, HALO, RW_WIDTH), lambda b, c: (1, b, rev(c), 0, 0)),
        ],
        out_specs=[
            pl.BlockSpec((SCAN_BATCH, CHUNK, RW_WIDTH), lambda b, c: (b, c, 0)),
            pl.BlockSpec((SCAN_BATCH, CHUNK, RW_WIDTH), lambda b, c: (b, rev(c), 0)),
        ],
        out_shape=[jax.ShapeDtypeStruct((bsz, seq, RW_WIDTH), BF16)] * 2,
        scratch_shapes=[pltpu.VMEM((SCAN_BATCH, N_DIR, RW_WIDTH // LANES, LANES, LANES), jnp.float32)],
        compiler_params=pltpu.CompilerParams(
            dimension_semantics=("arbitrary", "arbitrary"), vmem_limit_bytes=VMEM_LIMIT),
        name="rwkv7_scan",
    )(ops, ops, v_bf, v_bf, gend, gend)


def _hg_kernel(qf_ref, ff_ref, if_ref, qb_ref, fb_ref, ib_ref, lbl_ref, tri_ref,
               of_ref, ob_ref, s_ref, cs_ref, vs_ref, *, layer):
    c = pl.program_id(1)

    @pl.when(c == 0)
    def _():
        s_ref[...] = jnp.zeros_like(s_ref)

    lg = lbl_ref[...]
    e = jnp.exp(lg - jnp.max(lg, axis=0, keepdims=True))
    lb = jnp.sum(e[0:layer + 1, :], axis=0, keepdims=True) / jnp.sum(e, axis=0, keepdims=True)

    heads = range(HG_HEADS)
    sls = [slice(h * HG_DK, (h + 1) * HG_DK) for h in heads]
    trow = lax.broadcasted_iota(jnp.int32, (SUB, HG_DK), 0)
    dir_refs = ((qf_ref, ff_ref, if_ref, of_ref), (qb_ref, fb_ref, ib_ref, ob_ref))
    probs = [(n, d) for n in range(SCAN_BATCH) for d in range(N_DIR)]

    def matmul_part(k):
        n, d = probs[k]
        q_ref, f_ref, i_ref, _ = dir_refs[d]
        q, v = q_ref[n], i_ref[n]
        v_bf = v.astype(BF16)
        f = lb + (1.0 - lb) * _sigmoid(f_ref[n])
        kf = 1.0 - f
        b2 = _select_rows(tri_ref[d], jnp.log2(f), 3)
        (total,) = _chunk_totals(b2, d)
        q_in = (q * jnp.exp2(b2)).astype(BF16)
        k_end = (kf * jnp.exp2(total - b2)).astype(BF16)
        g_end = jnp.exp2(total)
        cs = b2 - jnp.log2(kf)
        for h in heads:
            cs_ref[k, h] = cs[:, sls[h]]
            vs_ref[k, h] = v[:, sls[h]]

        st = [s_ref[n, d, h] for h in heads]
        inter = [lax.dot_general(q_in[:, sls[h]], st[h].astype(BF16), _NT, preferred_element_type=jnp.float32)
                 for h in heads]
        for h in heads:
            s_ref[n, d, h] = st[h] * g_end[:, sls[h]] + lax.dot_general(
                v_bf[:, sls[h]], k_end[:, sls[h]], _TN, preferred_element_type=jnp.float32)

        def rows(p_lo, p_hi):
            lo = p_lo if d == 0 else CHUNK - p_hi
            return slice(lo, lo + (p_hi - p_lo))

        pending = []
        size = CHUNK // 2
        while size >= SUB:
            for p in range(0, CHUNK, 2 * size):
                s_sl, t_sl = rows(p, p + size), rows(p + size, p + 2 * size)
                ref = b2[rows(p + size - 1, p + size)]
                q_t = (q[t_sl] * jnp.exp2(b2[t_sl] - ref)).astype(BF16)
                k_s = (kf[s_sl] * jnp.exp2(ref - b2[s_sl])).astype(BF16)
                scores = [lax.dot_general(q_t[:, sls[h]], k_s[:, sls[h]], _NT,
                                          preferred_element_type=jnp.float32).astype(BF16) for h in heads]
                pending.append((t_sl, s_sl, scores))
            size //= 2
        return q, b2, v_bf, inter, pending

    def vpu_part(k, q, b2, v_bf, inter, pending):
        n, d = probs[k]
        o_ref = dir_refs[d][3]
        for lo in range(0, CHUNK, SUB):
            q_i, b_i = q[lo:lo + SUB], b2[lo:lo + SUB]
            acc = [inter[h][lo:lo + SUB] for h in heads]
            for s in range(SUB):
                row_s = pl.ds(lo + s, SUB, stride=0)
                m = (trow >= s) if d == 0 else (trow <= s)
                for h in heads:
                    xs = jnp.where(m, q_i[:, sls[h]] * jnp.exp2(b_i[:, sls[h]] - cs_ref[k, h, row_s, :]), 0.0)
                    w = jnp.sum(xs, axis=-1, keepdims=True)
                    acc[h] = acc[h] + w * vs_ref[k, h, row_s, :]
            for h in heads:
                o_ref[n, lo:lo + SUB, sls[h]] = acc[h]
        for t_sl, s_sl, scores in pending:
            for h in heads:
                o_ref[n, t_sl, sls[h]] += jnp.dot(scores[h], v_bf[s_sl, sls[h]], preferred_element_type=jnp.float32)

    ctx = matmul_part(0)
    for k in range(len(probs)):
        nxt = matmul_part(k + 1) if k + 1 < len(probs) else None
        vpu_part(k, *ctx)
        ctx = nxt


def _hg_mix(p_hg, lb_logits, layer, tri):
    bsz, seq, _ = p_hg.shape
    n_chunks = seq // CHUNK
    rev = lambda c: n_chunks - 1 - c
    blk = (SCAN_BATCH, CHUNK, HG_WIDTH)
    fwd = lambda j: pl.BlockSpec(blk, lambda g, c: (g, c, j))
    bwd = lambda j: pl.BlockSpec(blk, lambda g, c: (g, rev(c), j))
    full = lambda a: pl.BlockSpec(a.shape, lambda g, c: (0,) * a.ndim)
    n_prob = SCAN_BATCH * N_DIR
    return pl.pallas_call(
        functools.partial(_hg_kernel, layer=layer),
        grid=(bsz // SCAN_BATCH, n_chunks),
        in_specs=[fwd(0), fwd(1), fwd(3), bwd(0), bwd(2), bwd(3), full(lb_logits), full(tri)],
        out_specs=[fwd(0), bwd(0)],
        out_shape=[jax.ShapeDtypeStruct((bsz, seq, HG_WIDTH), jnp.float32)] * 2,
        scratch_shapes=[pltpu.VMEM((SCAN_BATCH, N_DIR, HG_HEADS, HG_DK, HG_DK), jnp.float32),
                        pltpu.VMEM((n_prob, HG_HEADS, CHUNK, HG_DK), jnp.float32),
                        pltpu.VMEM((n_prob, HG_HEADS, CHUNK, HG_DK), jnp.float32)],
        compiler_params=pltpu.CompilerParams(
            dimension_semantics=("arbitrary", "arbitrary"), vmem_limit_bytes=VMEM_LIMIT),
        name="hgrn2_mix",
    )(p_hg, p_hg, p_hg, p_hg, p_hg, p_hg, lb_logits, tri)


def _out_kernel(x_ref, yf_ref, yb_ref, bonus_ref, gate_ref, of_ref, ob_ref, hgate_ref,
                lnw_ref, lnb_ref, hgn_ref, seg_ref, w_ref, post_ref, out_ref):
    y = yf_ref[...].astype(jnp.float32) + yb_ref[...].astype(jnp.float32)
    seg = seg_ref[...]
    mu = _select_cols(y, seg, 1) * (1.0 / RW_HEAD_DIM)
    yc = y - mu
    var = _select_cols(yc * yc, seg, 1) * (1.0 / RW_HEAD_DIM)
    y = yc * lax.rsqrt(var + GN_EPS) * lnw_ref[...] + lnb_ref[...]
    rw = (y + bonus_ref[...].astype(jnp.float32)) * gate_ref[...].astype(jnp.float32)

    o = of_ref[...] + ob_ref[...]
    parts = []
    for h in range(HG_HEADS):
        oh = o[:, h * HG_DK:(h + 1) * HG_DK]
        parts.append(oh * lax.rsqrt(jnp.mean(oh * oh, axis=-1, keepdims=True) + NORM_EPS))
    hg = jnp.concatenate(parts, axis=1) * hgn_ref[...] * hgate_ref[...].astype(jnp.float32)

    cat = jnp.concatenate([rw, hg], axis=1).astype(BF16)
    yo = jnp.dot(cat, w_ref[...], preferred_element_type=jnp.float32)
    ms = jnp.mean(yo * yo, axis=-1, keepdims=True)
    out_ref[...] = x_ref[...] + yo * lax.rsqrt(ms + NORM_EPS) * post_ref[...]


def _out_proj(x2, y_f, y_b, bonus, gate, o_f, o_b, hg_gate, ln_w, ln_b, hg_norm, seg, w_bf16, post_g):
    rows = x2.shape[0]
    tile = lambda n: pl.BlockSpec((OUT_TILE, n), lambda i: (i, 0))
    row_spec = lambda n: pl.BlockSpec((1, n), lambda i: (0, 0))
    return pl.pallas_call(
        _out_kernel,
        grid=(rows // OUT_TILE,),
        in_specs=[
            tile(D_MODEL),
            tile(RW_WIDTH), tile(RW_WIDTH), tile(RW_WIDTH), tile(RW_WIDTH),
            tile(HG_WIDTH), tile(HG_WIDTH), tile(HG_WIDTH),
            row_spec(RW_WIDTH), row_spec(RW_WIDTH), row_spec(HG_WIDTH),
            pl.BlockSpec((RW_WIDTH, RW_WIDTH), lambda i: (0, 0)),
            pl.BlockSpec((D_MODEL, D_MODEL), lambda i: (0, 0)),
            row_spec(D_MODEL),
        ],
        out_specs=tile(D_MODEL),
        out_shape=jax.ShapeDtypeStruct((rows, D_MODEL), jnp.float32),
        compiler_params=pltpu.CompilerParams(
            dimension_semantics=("arbitrary",), vmem_limit_bytes=VMEM_LIMIT),
        name="out_proj",
    )(x2, y_f, y_b, bonus, gate, o_f, o_b, hg_gate, ln_w, ln_b, hg_norm, seg, w_bf16, post_g)


def _lora_weights(w):
    z = jnp.zeros_like(w[0])
    wp = jnp.stack([jnp.concatenate([w[0], z], axis=0), jnp.concatenate([z, w[1]], axis=0)])
    return wp.astype(BF16)


def _scan_order_selectors(tile):
    t = jnp.arange(tile)
    same = (t[:, None] // CHUNK) == (t[None, :] // CHUNK)
    fwd = same & (t[None, :] <= t[:, None])
    bwd = same & (t[None, :] >= t[:, None])
    return jnp.stack([fwd, bwd]).astype(BF16)


def kernel(x, pre_norm_g, w_in, rw_shift_prev, rw_shift_next, rw_w0, rw_w2, rw_a0, rw_a2, rw_k_k, rw_k_a,
           rw_r_k, rw_ln_w, rw_ln_b, hg_lb_logits, hg_norm_g, w_out, post_norm_g):
    bsz, seq, dm = x.shape
    depth = w_in.shape[0]
    rows = bsz * seq
    lane = jnp.arange(RW_WIDTH) // RW_HEAD_DIM
    seg = (lane[:, None] == lane[None, :]).astype(BF16)
    tri = _scan_order_selectors(ROW_TILE)
    hg_tri = _scan_order_selectors(CHUNK)
    row = lambda t: t.reshape(1, -1)
    flat = lambda t: t.reshape(rows, t.shape[-1])
    for l in range(depth):
        x2 = x.reshape(rows, dm)
        w_bf = w_in[l].astype(BF16)
        p_hg, hg_gate, ops, v_bf, gend, bonus, gate = _in_rw(
            x, row(pre_norm_g[l]), w_bf[:, :RW_COLS], w_bf[:, RW_COLS:],
            row(rw_shift_prev[l]), row(rw_shift_next[l]),
            rw_w0[l], _lora_weights(rw_w2[l]), rw_a0[l], _lora_weights(rw_a2[l]),
            row(rw_k_k[l]), row(rw_k_a[l]), row(rw_r_k[l]), seg, tri)
        y_f, y_b = _rw_scan(ops, v_bf, gend)
        o_f, o_b = _hg_mix(p_hg, hg_lb_logits, l, hg_tri)
        out = _out_proj(
            x2, flat(y_f), flat(y_b), flat(bonus), flat(gate), flat(o_f), flat(o_b), flat(hg_gate),
            row(rw_ln_w[l]), row(rw_ln_b[l]), row(jnp.tile(hg_norm_g[l], HG_HEADS)), seg,
            w_out[l].astype(BF16), row(post_norm_g[l]))
        x = out.reshape(bsz, seq, dm)
    return x
```

```python
import functools

import jax
import jax.numpy as jnp
from jax import lax
from jax.experimental import pallas as pl
from jax.experimental.pallas import tpu as pltpu

D_MODEL = 1024
RW_HEAD_DIM = 64
RW_WIDTH = 512
RW_HEADS = 8
HG_DK = 128
HG_WIDTH = 512
HG_HEADS = 4
LORA = 64
NORM_EPS = 1e-6
GN_EPS = 64e-5
RW_COLS = 4 * RW_WIDTH + 4 * LORA
HG_COLS = 5 * HG_WIDTH
HG_SCAN_COLS = 4 * HG_WIDTH
IN_COLS = RW_COLS + HG_COLS
N_DIR = 2
RW_OPS = 6

LANES = 128
LHS_ROWS = 16
MXU_TILE = 256
CHUNK = 64
SUB = 8
ROW_TILE = 256
OUT_TILE = 512
HALO = 8
SCAN_BATCH = 4
VMEM_LIMIT = 56 * 1024 * 1024

_NN = (((1,), (0,)), ((), ()))
_NT = (((1,), (1,)), ((), ()))
_TN = (((0,), (0,)), ((), ()))
BF16 = jnp.bfloat16
LOG2E = 1.4426950408889634


def _split_bf16(x, terms=2):
    pieces = []
    for _ in range(terms - 1):
        hi = x.astype(BF16)
        pieces.append(hi)
        x = x - hi.astype(jnp.float32)
    pieces.append(x.astype(BF16))
    return pieces


def _select_rows(sel, x, terms):
    acc = None
    for piece in _split_bf16(x, terms):
        t = jnp.dot(sel, piece, preferred_element_type=jnp.float32)
        acc = t if acc is None else acc + t
    return acc


def _head_sums(x, seg):
    xb = x.astype(BF16)
    return jnp.concatenate([jnp.dot(xb[:, c:c + MXU_TILE], seg, preferred_element_type=jnp.float32)
                            for c in range(0, x.shape[1], MXU_TILE)], axis=1)


def _sigmoid(x):
    return 1.0 / (1.0 + jnp.exp(-x))


def _chunk_totals(cum, d):
    last = CHUNK - 1 if d == 0 else 0
    return [cum[c * CHUNK + last:c * CHUNK + last + 1] for c in range(cum.shape[0] // CHUNK)]


def _expand_chunks(rows):
    return jnp.concatenate([jnp.broadcast_to(r, (CHUNK, r.shape[1])) for r in rows], axis=0)


def _in_rw_kernel(xc_ref, xp_ref, xn_ref, g_ref, w_ref,
                  mup_ref, mun_ref, w0_ref, w2_ref, a0_ref, a2_ref,
                  kk_ref, ka_ref, rk_ref, seg_ref, tri_ref,
                  phg_ref, hgate_ref, ops_ref, v_ref, gend_ref, bonus_ref, gate_ref, *, n_tiles):
    j = pl.program_id(1)

    def pre_norm(x):
        return x * lax.rsqrt(jnp.mean(x * x, axis=-1, keepdims=True) + NORM_EPS) * g_ref[...]

    h_c = pre_norm(xc_ref[...])
    h_p = pre_norm(xp_ref[...]) * (j > 0).astype(jnp.float32)
    h_n = pre_norm(xn_ref[...]) * (j < n_tiles - 1).astype(jnp.float32)
    h = h_c.astype(BF16)
    h_ext = jnp.concatenate([h_p, h_c, h_n], axis=0).astype(BF16)

    hg_blocks = iter(range(HG_COLS // HG_WIDTH))

    def hg_block():
        i = next(hg_blocks, None)
        if i is not None:
            cols = slice(i * HG_WIDTH, (i + 1) * HG_WIDTH)
            blk = jnp.dot(h, w_ref[:, RW_COLS + cols.start:RW_COLS + cols.stop], preferred_element_type=jnp.float32)
            if i < HG_SCAN_COLS // HG_WIDTH:
                phg_ref[:, cols] = blk
            else:
                hgate_ref[...] = (blk * _sigmoid(blk)).astype(BF16)

    W = RW_WIDTH
    ext = ROW_TILE + 2 * HALO
    tile = slice(HALO, HALO + ROW_TILE)

    def shifted(cols):
        p_ext = jnp.dot(h_ext, w_ref[:, cols], preferred_element_type=jnp.float32)
        p, p_prev, p_next = p_ext[tile], pltpu.roll(p_ext, 1, 0)[tile], pltpu.roll(p_ext, ext - 1, 0)[tile]
        mup, mun = mup_ref[:, cols], mun_ref[:, cols]
        return p * (1.0 - mup - mun) + mup * p_prev + mun * p_next

    r, k, v, g = [shifted(slice(i * W, (i + 1) * W)) for i in range(4)]
    lora_in = shifted(slice(4 * W, RW_COLS))
    wd = lora_in[:, :2 * LORA]
    ad = lora_in[:, 2 * LORA:].astype(BF16)
    hg_block()
    gate_ref[...] = (g * _sigmoid(g)).astype(BF16)
    v_ref[...] = v.astype(BF16)

    seg = seg_ref[...]
    kk = k * kk_ref[...]
    kk = kk * lax.rsqrt(jnp.maximum(_head_sums(kk * kk, seg), 1e-24))
    tanh_wd = jnp.tanh(wd).astype(BF16)
    hg_block()

    k_sum = None
    for d in range(N_DIR):
        w_raw = w0_ref[d:d + 1, :] + jnp.dot(tanh_wd, w2_ref[d], preferred_element_type=jnp.float32)
        lw = (-LOG2E * jnp.exp(jnp.float32(-0.5))) * _sigmoid(w_raw)
        a = _sigmoid(a0_ref[d:d + 1, :] + jnp.dot(ad, a2_ref[d], preferred_element_type=jnp.float32))
        k_d = k * (1.0 + (a - 1.0) * ka_ref[...])
        b_d = kk * a
        k_sum = k_d if k_sum is None else k_sum + k_d
        cl = _select_rows(tri_ref[d], lw, 2)
        g_end = [jnp.exp2(tot) for tot in _chunk_totals(cl, d)]
        hg_block()
        e_out = jnp.exp2(-cl)
        e_end = e_out * _expand_chunks(g_end)
        ops = (r * jnp.exp2(cl),
               kk * jnp.exp2(cl - lw),
               k_d * e_out, b_d * e_out,
               k_d * e_end, b_d * e_end)
        for i, op in enumerate(ops):
            ops_ref[d, :, i * W:(i + 1) * W] = op.astype(BF16)
        for q, g_q in enumerate(g_end):
            gend_ref[d, q] = jnp.broadcast_to(g_q, (HALO, W))
        hg_block()

    bonus_ref[...] = (_head_sums(r * k_sum * rk_ref[...], seg) * v).astype(BF16)
    for _ in hg_blocks:
        raise AssertionError("HGRN2 column blocks left unissued")


def _in_rw(x, g, w_in, mup, mun, w0, w2, a0, a2, k_k, k_a, r_k, seg, tri):
    bsz, seq, _ = x.shape
    n_tiles = seq // ROW_TILE
    hb = ROW_TILE // HALO
    n_halo = seq // HALO
    n_chunks = seq // CHUNK
    cpt = ROW_TILE // CHUNK
    full = lambda a: pl.BlockSpec(a.shape, lambda b, j: (0,) * a.ndim)
    tile = lambda n: pl.BlockSpec((None, ROW_TILE, n), lambda b, j: (b, j, 0))
    consts = (g, w_in, mup, mun, w0, w2, a0, a2, k_k, k_a, r_k, seg, tri)
    return pl.pallas_call(
        functools.partial(_in_rw_kernel, n_tiles=n_tiles),
        grid=(bsz, n_tiles),
        in_specs=[
            tile(D_MODEL),
            pl.BlockSpec((None, HALO, D_MODEL), lambda b, j: (b, jnp.maximum(j * hb - 1, 0), 0)),
            pl.BlockSpec((None, HALO, D_MODEL), lambda b, j: (b, jnp.minimum((j + 1) * hb, n_halo - 1), 0)),
            *[full(a) for a in consts],
        ],
        out_specs=[
            tile(HG_SCAN_COLS),
            tile(HG_WIDTH),
            pl.BlockSpec((N_DIR, None, ROW_TILE, RW_OPS * RW_WIDTH), lambda b, j: (0, b, j, 0)),
            tile(RW_WIDTH),
            pl.BlockSpec((N_DIR, None, cpt, HALO, RW_WIDTH), lambda b, j: (0, b, j, 0, 0)),
            tile(RW_WIDTH),
            tile(RW_WIDTH),
        ],
        out_shape=[
            jax.ShapeDtypeStruct((bsz, seq, HG_SCAN_COLS), jnp.float32),
            jax.ShapeDtypeStruct((bsz, seq, HG_WIDTH), BF16),
            jax.ShapeDtypeStruct((N_DIR, bsz, seq, RW_OPS * RW_WIDTH), BF16),
            jax.ShapeDtypeStruct((bsz, seq, RW_WIDTH), BF16),
            jax.ShapeDtypeStruct((N_DIR, bsz, n_chunks, HALO, RW_WIDTH), jnp.float32),
            jax.ShapeDtypeStruct((bsz, seq, RW_WIDTH), BF16),
            jax.ShapeDtypeStruct((bsz, seq, RW_WIDTH), BF16),
        ],
        compiler_params=pltpu.CompilerParams(
            dimension_semantics=("arbitrary", "arbitrary"), vmem_limit_bytes=VMEM_LIMIT),
        name="in_proj_rwkv7_prep",
    )(x, x, x, *consts)


def _rw_scan_kernel(of_ref, ob_ref, vf_ref, vb_ref, gf_ref, gb_ref, yf_ref, yb_ref, s_ref):
    c = pl.program_id(1)

    @pl.when(c == 0)
    def _():
        s_ref[...] = jnp.zeros_like(s_ref)

    N, W = RW_HEAD_DIM, RW_WIDTH
    lane = lax.broadcasted_iota(jnp.int32, (CHUNK, LANES), 1)
    tt = lax.broadcasted_iota(jnp.int32, (CHUNK, LANES), 0)
    half = (lane < N, lane >= N)
    diag = ((lax.broadcasted_iota(jnp.int32, (LANES, LANES), 0) < N)
            == (lax.broadcasted_iota(jnp.int32, (LANES, LANES), 1) < N))
    dir_refs = ((of_ref, vf_ref, gf_ref, yf_ref), (ob_ref, vb_ref, gb_ref, yb_ref))

    pairs = [(q, d, j) for q in range(SCAN_BATCH) for d in range(N_DIR) for j in range(W // LANES)]
    probs = [(pi, par) for pi in range(len(pairs)) for par in range(2)]

    def op(pi, i):
        q, d, j = pairs[pi]
        return dir_refs[d][0][q, :, i * W + j * LANES:i * W + (j + 1) * LANES]

    r_p = [op(pi, 0) for pi in range(len(pairs))]
    a_p = [op(pi, 1) for pi in range(len(pairs))]
    k_p = [op(pi, 2) for pi in range(len(pairs))]
    b_p = [op(pi, 3) for pi in range(len(pairs))]
    ke_p = [op(pi, 4) for pi in range(len(pairs))]
    be_p = [op(pi, 5) for pi in range(len(pairs))]
    v_p = [dir_refs[d][1][q, :, j * LANES:(j + 1) * LANES] for q, d, j in pairs]
    s_p = [s_ref[q, d, j] for q, d, j in pairs]

    def lag(pi):
        d = pairs[pi][1]
        delta = tt - (lane & (N - 1))
        return delta if d == 0 else -delta

    gram = []
    for pi, par in probs:
        lhs = jnp.concatenate([a_p[pi], r_p[pi]], axis=0)
        lhs = jnp.where(jnp.concatenate([half[par], half[par]], axis=0), lhs, jnp.zeros_like(lhs))
        rhs = jnp.concatenate([b_p[pi], k_p[pi]], axis=0)
        gram.append(lax.dot_general(lhs, rhs, _NT, preferred_element_type=jnp.float32))

    za, zb, bot = [], [], []
    for n, (pi, par) in enumerate(probs):
        top = gram[n][:CHUNK]
        lg = lag(pi)
        m0 = jnp.where(jnp.where(half[0], lg, 0) > 0, -top, 0.0)
        l_ak = jnp.where(jnp.where(half[1], lg, 0) > 0, top, 0.0)
        bot.append(jnp.where(lg >= 0, gram[n][CHUNK:], 0.0).astype(BF16))
        vv = jnp.concatenate([v_p[pi], v_p[pi]], axis=0)
        zb.append(jnp.dot(l_ak.astype(BF16), vv, preferred_element_type=jnp.float32))
        a_own = a_p[pi].astype(jnp.float32)
        za.append(jnp.where(half[1], a_own if par == 1 else pltpu.roll(a_own, N, 1), m0))

    for level in range(6):
        skip = (2 ** level) // LHS_ROWS * LHS_ROWS
        new_za, new_zb = [], []
        for n, (pi, par) in enumerate(probs):
            live = slice(skip, CHUNK) if pairs[pi][1] == 0 else slice(0, CHUNK - skip)
            m = za[n][live, :N].astype(BF16)
            z2 = jnp.concatenate([za[n], zb[n]], axis=1).astype(BF16)
            p = jnp.dot(m, z2, preferred_element_type=jnp.float32)
            keep_a = jnp.where(half[1], za[n], 0.0)
            pieces_a = [keep_a[:live.start], p[:, :LANES] + keep_a[live], keep_a[live.stop:]]
            pieces_b = [zb[n][:live.start], p[:, LANES:] + zb[n][live], zb[n][live.stop:]]
            new_za.append(jnp.concatenate([x for x in pieces_a if x.shape[0]], axis=0))
            new_zb.append(jnp.concatenate([x for x in pieces_b if x.shape[0]], axis=0))
        za, zb = new_za, new_zb

    n_pairs = range(len(pairs))
    s_bf = [s_p[pi].astype(BF16) for pi in n_pairs]
    wr = [jnp.concatenate([jnp.where(half[0], pltpu.roll(za[2 * pi], N, 1), za[2 * pi + 1]).astype(BF16),
                           r_p[pi]], axis=0) for pi in n_pairs]
    ws = [lax.dot_general(wr[pi], s_bf[pi], _NT, preferred_element_type=jnp.float32) for pi in n_pairs]
    u_bf = [(-(ws[pi][:CHUNK] + jnp.where(half[0], zb[2 * pi], zb[2 * pi + 1]))).astype(BF16)
            for pi in n_pairs]
    uv = [jnp.concatenate([u_bf[pi], v_p[pi]], axis=0) for pi in n_pairs]
    y_eo = [jnp.dot(jnp.concatenate([bot[2 * pi], bot[2 * pi + 1]], axis=0), uv[pi],
                    preferred_element_type=jnp.float32) for pi in n_pairs]
    upd = [lax.dot_general(uv[pi], jnp.concatenate([be_p[pi], ke_p[pi]], axis=0), _TN,
                           preferred_element_type=jnp.float32) for pi in n_pairs]
    for pi, (q, d, j) in enumerate(pairs):
        y = ws[pi][CHUNK:] + jnp.where(half[0], y_eo[pi][:CHUNK], y_eo[pi][CHUNK:])
        dir_refs[d][3][q, :, j * LANES:(j + 1) * LANES] = y.astype(BF16)
        g_end = dir_refs[d][2][q, 0:1, j * LANES:(j + 1) * LANES]
        s_ref[q, d, j] = s_p[pi] * g_end + jnp.where(diag, upd[pi], 0.0)


def _rw_scan(ops, v_bf, gend):
    _, bsz, seq, _ = ops.shape
    n_chunks = seq // CHUNK
    rev = lambda c: n_chunks - 1 - c
    return pl.pallas_call(
        _rw_scan_kernel,
        grid=(bsz // SCAN_BATCH, n_chunks),
        in_specs=[
            pl.BlockSpec((None, SCAN_BATCH, CHUNK, RW_OPS * RW_WIDTH), lambda b, c: (0, b, c, 0)),
            pl.BlockSpec((None, SCAN_BATCH, CHUNK, RW_OPS * RW_WIDTH), lambda b, c: (1, b, rev(c), 0)),
            pl.BlockSpec((SCAN_BATCH, CHUNK, RW_WIDTH), lambda b, c: (b, c, 0)),
            pl.BlockSpec((SCAN_BATCH, CHUNK, RW_WIDTH), lambda b, c: (b, rev(c), 0)),
            pl.BlockSpec((None, SCAN_BATCH, None, HALO, RW_WIDTH), lambda b, c: (0, b, c, 0, 0)),
            pl.BlockSpec((None, SCAN_BATCH, None, HALO, RW_WIDTH), lambda b, c: (1, b, rev(c), 0, 0)),
        ],
        out_specs=[
            pl.BlockSpec((SCAN_BATCH, CHUNK, RW_WIDTH), lambda b, c: (b, c, 0)),
            pl.BlockSpec((SCAN_BATCH, CHUNK, RW_WIDTH), lambda b, c: (b, rev(c), 0)),
        ],
        out_shape=[jax.ShapeDtypeStruct((bsz, seq, RW_WIDTH), BF16)] * 2,
        scratch_shapes=[pltpu.VMEM((SCAN_BATCH, N_DIR, RW_WIDTH // LANES, LANES, LANES), jnp.float32)],
        compiler_params=pltpu.CompilerParams(
            dimension_semantics=("arbitrary", "arbitrary"), vmem_limit_bytes=VMEM_LIMIT),
        name="rwkv7_scan",
    )(ops, ops, v_bf, v_bf, gend, gend)


def _hg_kernel(qf_ref, ff_ref, if_ref, qb_ref, fb_ref, ib_ref, lbl_ref, tri_ref,
               of_ref, ob_ref, s_ref, cs_ref, vs_ref, *, layer):
    c = pl.program_id(1)

    @pl.when(c == 0)
    def _():
        s_ref[...] = jnp.zeros_like(s_ref)

    lg = lbl_ref[...]
    e = jnp.exp(lg - jnp.max(lg, axis=0, keepdims=True))
    lb = jnp.sum(e[0:layer + 1, :], axis=0, keepdims=True) / jnp.sum(e, axis=0, keepdims=True)

    heads = range(HG_HEADS)
    sls = [slice(h * HG_DK, (h + 1) * HG_DK) for h in heads]
    trow = lax.broadcasted_iota(jnp.int32, (SUB, HG_DK), 0)
    dir_refs = ((qf_ref, ff_ref, if_ref, of_ref), (qb_ref, fb_ref, ib_ref, ob_ref))
    probs = [(n, d) for n in range(SCAN_BATCH) for d in range(N_DIR)]

    def matmul_part(k):
        n, d = probs[k]
        q_ref, f_ref, i_ref, _ = dir_refs[d]
        q, v = q_ref[n], i_ref[n]
        v_bf = v.astype(BF16)
        f = lb + (1.0 - lb) * _sigmoid(f_ref[n])
        kf = 1.0 - f
        b2 = _select_rows(tri_ref[d], jnp.log2(f), 2)
        (total,) = _chunk_totals(b2, d)
        q_in = (q * jnp.exp2(b2)).astype(BF16)
        k_end = (kf * jnp.exp2(total - b2)).astype(BF16)
        g_end = jnp.exp2(total)
        cs = b2 - jnp.log2(kf)
        for h in heads:
            cs_ref[k, h] = cs[:, sls[h]]
            vs_ref[k, h] = v[:, sls[h]]

        st = [s_ref[n, d, h] for h in heads]
        inter = [lax.dot_general(q_in[:, sls[h]], st[h].astype(BF16), _NT, preferred_element_type=jnp.float32)
                 for h in heads]
        for h in heads:
            s_ref[n, d, h] = st[h] * g_end[:, sls[h]] + lax.dot_general(
                v_bf[:, sls[h]], k_end[:, sls[h]], _TN, preferred_element_type=jnp.float32)

        def rows(p_lo, p_hi):
            lo = p_lo if d == 0 else CHUNK - p_hi
            return slice(lo, lo + (p_hi - p_lo))

        pending = []
        size = CHUNK // 2
        while size >= SUB:
            for p in range(0, CHUNK, 2 * size):
                s_sl, t_sl = rows(p, p + size), rows(p + size, p + 2 * size)
                ref = b2[rows(p + size - 1, p + size)]
                q_t = (q[t_sl] * jnp.exp2(b2[t_sl] - ref)).astype(BF16)
                k_s = (kf[s_sl] * jnp.exp2(ref - b2[s_sl])).astype(BF16)
                scores = [lax.dot_general(q_t[:, sls[h]], k_s[:, sls[h]], _NT,
                                          preferred_element_type=jnp.float32).astype(BF16) for h in heads]
                pending.append((t_sl, s_sl, scores))
            size //= 2
        return q, b2, v_bf, inter, pending

    def vpu_part(k, q, b2, v_bf, inter, pending):
        n, d = probs[k]
        o_ref = dir_refs[d][3]
        for lo in range(0, CHUNK, SUB):
            q_i, b_i = q[lo:lo + SUB], b2[lo:lo + SUB]
            acc = [inter[h][lo:lo + SUB] for h in heads]
            for s in range(SUB):
                row_s = pl.ds(lo + s, SUB, stride=0)
                m = (trow >= s) if d == 0 else (trow <= s)
                for h in heads:
                    xs = jnp.where(m, q_i[:, sls[h]] * jnp.exp2(b_i[:, sls[h]] - cs_ref[k, h, row_s, :]), 0.0)
                    w = jnp.sum(xs, axis=-1, keepdims=True)
                    acc[h] = acc[h] + w * vs_ref[k, h, row_s, :]
            for h in heads:
                o_ref[n, lo:lo + SUB, sls[h]] = acc[h]
        for t_sl, s_sl, scores in pending:
            for h in heads:
                o_ref[n, t_sl, sls[h]] += jnp.dot(scores[h], v_bf[s_sl, sls[h]], preferred_element_type=jnp.float32)

    ctx = matmul_part(0)
    for k in range(len(probs)):
        nxt = matmul_part(k + 1) if k + 1 < len(probs) else None
        vpu_part(k, *ctx)
        ctx = nxt


def _hg_mix(p_hg, lb_logits, layer, tri):
    bsz, seq, _ = p_hg.shape
    n_chunks = seq // CHUNK
    rev = lambda c: n_chunks - 1 - c
    blk = (SCAN_BATCH, CHUNK, HG_WIDTH)
    fwd = lambda j: pl.BlockSpec(blk, lambda g, c: (g, c, j))
    bwd = lambda j: pl.BlockSpec(blk, lambda g, c: (g, rev(c), j))
    full = lambda a: pl.BlockSpec(a.shape, lambda g, c: (0,) * a.ndim)
    n_prob = SCAN_BATCH * N_DIR
    return pl.pallas_call(
        functools.partial(_hg_kernel, layer=layer),
        grid=(bsz // SCAN_BATCH, n_chunks),
        in_specs=[fwd(0), fwd(1), fwd(3), bwd(0), bwd(2), bwd(3), full(lb_logits), full(tri)],
        out_specs=[fwd(0), bwd(0)],
        out_shape=[jax.ShapeDtypeStruct((bsz, seq, HG_WIDTH), jnp.float32)] * 2,
        scratch_shapes=[pltpu.VMEM((SCAN_BATCH, N_DIR, HG_HEADS, HG_DK, HG_DK), jnp.float32),
                        pltpu.VMEM((n_prob, HG_HEADS, CHUNK, HG_DK), jnp.float32),
                        pltpu.VMEM((n_prob, HG_HEADS, CHUNK, HG_DK), jnp.float32)],
        compiler_params=pltpu.CompilerParams(
            dimension_semantics=("arbitrary", "arbitrary"), vmem_limit_bytes=VMEM_LIMIT),
        name="hgrn2_mix",
    )(p_hg, p_hg, p_hg, p_hg, p_hg, p_hg, lb_logits, tri)


def _out_kernel(x_ref, yf_ref, yb_ref, bonus_ref, gate_ref, of_ref, ob_ref, hgate_ref,
                lnw_ref, lnb_ref, hgn_ref, seg_ref, w_ref, post_ref, out_ref):
    y = yf_ref[...].astype(jnp.float32) + yb_ref[...].astype(jnp.float32)
    seg = seg_ref[...]
    mu = _head_sums(y, seg) * (1.0 / RW_HEAD_DIM)
    yc = y - mu
    var = _head_sums(yc * yc, seg) * (1.0 / RW_HEAD_DIM)
    y = yc * lax.rsqrt(var + GN_EPS) * lnw_ref[...] + lnb_ref[...]
    rw = (y + bonus_ref[...].astype(jnp.float32)) * gate_ref[...].astype(jnp.float32)

    o = of_ref[...] + ob_ref[...]
    parts = []
    for h in range(HG_HEADS):
        oh = o[:, h * HG_DK:(h + 1) * HG_DK]
        parts.append(oh * lax.rsqrt(jnp.mean(oh * oh, axis=-1, keepdims=True) + NORM_EPS))
    hg = jnp.concatenate(parts, axis=1) * hgn_ref[...] * hgate_ref[...].astype(jnp.float32)

    cat = jnp.concatenate([rw, hg], axis=1).astype(BF16)
    yo = jnp.dot(cat, w_ref[...], preferred_element_type=jnp.float32)
    ms = jnp.mean(yo * yo, axis=-1, keepdims=True)
    out_ref[...] = x_ref[...] + yo * lax.rsqrt(ms + NORM_EPS) * post_ref[...]


def _out_proj(x2, y_f, y_b, bonus, gate, o_f, o_b, hg_gate, ln_w, ln_b, hg_norm, seg, w_bf16, post_g):
    rows = x2.shape[0]
    tile = lambda n: pl.BlockSpec((OUT_TILE, n), lambda i: (i, 0))
    row_spec = lambda n: pl.BlockSpec((1, n), lambda i: (0, 0))
    return pl.pallas_call(
        _out_kernel,
        grid=(rows // OUT_TILE,),
        in_specs=[
            tile(D_MODEL),
            tile(RW_WIDTH), tile(RW_WIDTH), tile(RW_WIDTH), tile(RW_WIDTH),
            tile(HG_WIDTH), tile(HG_WIDTH), tile(HG_WIDTH),
            row_spec(RW_WIDTH), row_spec(RW_WIDTH), row_spec(HG_WIDTH),
            pl.BlockSpec((MXU_TILE, MXU_TILE), lambda i: (0, 0)),
            pl.BlockSpec((D_MODEL, D_MODEL), lambda i: (0, 0)),
            row_spec(D_MODEL),
        ],
        out_specs=tile(D_MODEL),
        out_shape=jax.ShapeDtypeStruct((rows, D_MODEL), jnp.float32),
        compiler_params=pltpu.CompilerParams(
            dimension_semantics=("arbitrary",), vmem_limit_bytes=VMEM_LIMIT),
        name="out_proj",
    )(x2, y_f, y_b, bonus, gate, o_f, o_b, hg_gate, ln_w, ln_b, hg_norm, seg, w_bf16, post_g)


def _lora_weights(w):
    z = jnp.zeros_like(w[0])
    wp = jnp.stack([jnp.concatenate([w[0], z], axis=0), jnp.concatenate([z, w[1]], axis=0)])
    return wp.astype(BF16)


def _scan_order_selectors(tile):
    t = jnp.arange(tile)
    same = (t[:, None] // CHUNK) == (t[None, :] // CHUNK)
    fwd = same & (t[None, :] <= t[:, None])
    bwd = same & (t[None, :] >= t[:, None])
    return jnp.stack([fwd, bwd]).astype(BF16)


def kernel(x, pre_norm_g, w_in, rw_shift_prev, rw_shift_next, rw_w0, rw_w2, rw_a0, rw_a2, rw_k_k, rw_k_a,
           rw_r_k, rw_ln_w, rw_ln_b, hg_lb_logits, hg_norm_g, w_out, post_norm_g):
    bsz, seq, dm = x.shape
    depth = w_in.shape[0]
    rows = bsz * seq
    lane = jnp.arange(MXU_TILE) // RW_HEAD_DIM
    seg = (lane[:, None] == lane[None, :]).astype(BF16)
    tri = _scan_order_selectors(ROW_TILE)
    hg_tri = _scan_order_selectors(CHUNK)
    row = lambda t: t.reshape(1, -1)
    flat = lambda t: t.reshape(rows, t.shape[-1])
    for l in range(depth):
        x2 = x.reshape(rows, dm)
        p_hg, hg_gate, ops, v_bf, gend, bonus, gate = _in_rw(
            x, row(pre_norm_g[l]), w_in[l].astype(BF16),
            row(rw_shift_prev[l]), row(rw_shift_next[l]),
            rw_w0[l], _lora_weights(rw_w2[l]), rw_a0[l], _lora_weights(rw_a2[l]),
            row(rw_k_k[l]), row(rw_k_a[l]), row(rw_r_k[l]), seg, tri)
        y_f, y_b = _rw_scan(ops, v_bf, gend)
        o_f, o_b = _hg_mix(p_hg, hg_lb_logits, l, hg_tri)
        out = _out_proj(
            x2, flat(y_f), flat(y_b), flat(bonus), flat(gate), flat(o_f), flat(o_b), flat(hg_gate),
            row(rw_ln_w[l]), row(rw_ln_b[l]), row(jnp.tile(hg_norm_g[l], HG_HEADS)), seg,
            w_out[l].astype(BF16), row(post_norm_g[l]))
        x = out.reshape(bsz, seq, dm)
    return x
```

```python
import functools
import itertools

import jax
import jax.numpy as jnp
from jax import lax
from jax.experimental import pallas as pl
from jax.experimental.pallas import tpu as pltpu

D_MODEL = 1024
RW_HEAD_DIM = 64
RW_WIDTH = 512
RW_HEADS = 8
HG_DK = 128
HG_WIDTH = 512
HG_HEADS = 4
LORA = 64
NORM_EPS = 1e-6
GN_EPS = 64e-5
RW_COLS = 4 * RW_WIDTH + 4 * LORA
HG_COLS = 5 * HG_WIDTH
HG_SCAN_COLS = 4 * HG_WIDTH
IN_COLS = RW_COLS + HG_COLS
N_DIR = 2
RW_OPS = 6

LANES = 128
LHS_ROWS = 16
MXU_TILE = 256
CHUNK = 64
SUB = 8
SAFE_LOG2_DROP = 100.0
ROW_TILE = 256
OUT_TILE = 512
HALO = 8
SCAN_BATCH = 4
VMEM_LIMIT = 56 * 1024 * 1024

_NN = (((1,), (0,)), ((), ()))
_NT = (((1,), (1,)), ((), ()))
_TN = (((0,), (0,)), ((), ()))
BF16 = jnp.bfloat16
LOG2E = 1.4426950408889634


def _split_bf16(x, terms=2):
    pieces = []
    for _ in range(terms - 1):
        hi = x.astype(BF16)
        pieces.append(hi)
        x = x - hi.astype(jnp.float32)
    pieces.append(x.astype(BF16))
    return pieces


def _select_rows(sel, x, terms):
    acc = None
    for piece in _split_bf16(x, terms):
        t = jnp.dot(sel, piece, preferred_element_type=jnp.float32)
        acc = t if acc is None else acc + t
    return acc


def _head_sums(x, seg):
    xb = x.astype(BF16)
    return jnp.concatenate([jnp.dot(xb[:, c:c + MXU_TILE], seg, preferred_element_type=jnp.float32)
                            for c in range(0, x.shape[1], MXU_TILE)], axis=1)


def _sigmoid(x):
    return 1.0 / (1.0 + jnp.exp(-x))


def _chunk_totals(cum, d):
    last = CHUNK - 1 if d == 0 else 0
    return [cum[c * CHUNK + last:c * CHUNK + last + 1] for c in range(cum.shape[0] // CHUNK)]


def _expand_chunks(rows):
    return jnp.concatenate([jnp.broadcast_to(r, (CHUNK, r.shape[1])) for r in rows], axis=0)


def _in_rw_kernel(xc_ref, xp_ref, xn_ref, g_ref, w_ref,
                  mup_ref, mun_ref, w0_ref, w2_ref, a0_ref, a2_ref,
                  kk_ref, ka_ref, rk_ref, seg_ref, tri_ref,
                  phg_ref, hgate_ref, ops_ref, v_ref, gend_ref, bonus_ref, gate_ref, *, n_tiles):
    j = pl.program_id(1)

    def pre_norm(x):
        return x * lax.rsqrt(jnp.mean(x * x, axis=-1, keepdims=True) + NORM_EPS) * g_ref[...]

    h_c = pre_norm(xc_ref[...])
    h_p = pre_norm(xp_ref[...]) * (j > 0).astype(jnp.float32)
    h_n = pre_norm(xn_ref[...]) * (j < n_tiles - 1).astype(jnp.float32)
    h = h_c.astype(BF16)
    h_ext = jnp.concatenate([h_p, h_c, h_n], axis=0).astype(BF16)

    hg_blocks = iter(range(HG_COLS // HG_WIDTH))

    def hg_block():
        i = next(hg_blocks, None)
        if i is not None:
            cols = slice(i * HG_WIDTH, (i + 1) * HG_WIDTH)
            blk = jnp.dot(h, w_ref[:, RW_COLS + cols.start:RW_COLS + cols.stop], preferred_element_type=jnp.float32)
            if i < HG_SCAN_COLS // HG_WIDTH:
                phg_ref[:, cols] = blk
            else:
                hgate_ref[...] = (blk * _sigmoid(blk)).astype(BF16)

    W = RW_WIDTH
    ext = ROW_TILE + 2 * HALO
    tile = slice(HALO, HALO + ROW_TILE)

    def shifted(cols):
        p_ext = jnp.dot(h_ext, w_ref[:, cols], preferred_element_type=jnp.float32)
        p, p_prev, p_next = p_ext[tile], pltpu.roll(p_ext, 1, 0)[tile], pltpu.roll(p_ext, ext - 1, 0)[tile]
        mup, mun = mup_ref[:, cols], mun_ref[:, cols]
        return p * (1.0 - mup - mun) + mup * p_prev + mun * p_next

    r, k, v, g = [shifted(slice(i * W, (i + 1) * W)) for i in range(4)]
    lora_in = shifted(slice(4 * W, RW_COLS))
    wd = lora_in[:, :2 * LORA]
    ad = lora_in[:, 2 * LORA:].astype(BF16)
    hg_block()
    gate_ref[...] = (g * _sigmoid(g)).astype(BF16)
    v_ref[...] = v.astype(BF16)

    seg = seg_ref[...]
    kk = k * kk_ref[...]
    kk = kk * lax.rsqrt(jnp.maximum(_head_sums(kk * kk, seg), 1e-24))
    tanh_wd = jnp.tanh(wd).astype(BF16)
    hg_block()

    k_sum = None
    for d in range(N_DIR):
        w_raw = w0_ref[d:d + 1, :] + jnp.dot(tanh_wd, w2_ref[d], preferred_element_type=jnp.float32)
        lw = (-LOG2E * jnp.exp(jnp.float32(-0.5))) * _sigmoid(w_raw)
        a = _sigmoid(a0_ref[d:d + 1, :] + jnp.dot(ad, a2_ref[d], preferred_element_type=jnp.float32))
        k_d = k * (1.0 + (a - 1.0) * ka_ref[...])
        b_d = kk * a
        k_sum = k_d if k_sum is None else k_sum + k_d
        cl = _select_rows(tri_ref[d], lw, 2)
        g_end = [jnp.exp2(tot) for tot in _chunk_totals(cl, d)]
        hg_block()
        e_out = jnp.exp2(-cl)
        e_end = e_out * _expand_chunks(g_end)
        ops = (r * jnp.exp2(cl),
               kk * jnp.exp2(cl - lw),
               k_d * e_out, b_d * e_out,
               k_d * e_end, b_d * e_end)
        for i, op in enumerate(ops):
            ops_ref[d, :, i * W:(i + 1) * W] = op.astype(BF16)
        for q, g_q in enumerate(g_end):
            gend_ref[d, q] = jnp.broadcast_to(g_q, (HALO, W))
        hg_block()

    bonus_ref[...] = (_head_sums(r * k_sum * rk_ref[...], seg) * v).astype(BF16)
    for _ in hg_blocks:
        raise AssertionError("HGRN2 column blocks left unissued")


def _in_rw(x, g, w_in, mup, mun, w0, w2, a0, a2, k_k, k_a, r_k, seg, tri):
    bsz, seq, _ = x.shape
    n_tiles = seq // ROW_TILE
    hb = ROW_TILE // HALO
    n_halo = seq // HALO
    n_chunks = seq // CHUNK
    cpt = ROW_TILE // CHUNK
    full = lambda a: pl.BlockSpec(a.shape, lambda b, j: (0,) * a.ndim)
    tile = lambda n: pl.BlockSpec((None, ROW_TILE, n), lambda b, j: (b, j, 0))
    consts = (g, w_in, mup, mun, w0, w2, a0, a2, k_k, k_a, r_k, seg, tri)
    return pl.pallas_call(
        functools.partial(_in_rw_kernel, n_tiles=n_tiles),
        grid=(bsz, n_tiles),
        in_specs=[
            tile(D_MODEL),
            pl.BlockSpec((None, HALO, D_MODEL), lambda b, j: (b, jnp.maximum(j * hb - 1, 0), 0)),
            pl.BlockSpec((None, HALO, D_MODEL), lambda b, j: (b, jnp.minimum((j + 1) * hb, n_halo - 1), 0)),
            *[full(a) for a in consts],
        ],
        out_specs=[
            tile(HG_SCAN_COLS),
            tile(HG_WIDTH),
            pl.BlockSpec((N_DIR, None, ROW_TILE, RW_OPS * RW_WIDTH), lambda b, j: (0, b, j, 0)),
            tile(RW_WIDTH),
            pl.BlockSpec((N_DIR, None, cpt, HALO, RW_WIDTH), lambda b, j: (0, b, j, 0, 0)),
            tile(RW_WIDTH),
            tile(RW_WIDTH),
        ],
        out_shape=[
            jax.ShapeDtypeStruct((bsz, seq, HG_SCAN_COLS), jnp.float32),
            jax.ShapeDtypeStruct((bsz, seq, HG_WIDTH), BF16),
            jax.ShapeDtypeStruct((N_DIR, bsz, seq, RW_OPS * RW_WIDTH), BF16),
            jax.ShapeDtypeStruct((bsz, seq, RW_WIDTH), BF16),
            jax.ShapeDtypeStruct((N_DIR, bsz, n_chunks, HALO, RW_WIDTH), jnp.float32),
            jax.ShapeDtypeStruct((bsz, seq, RW_WIDTH), BF16),
            jax.ShapeDtypeStruct((bsz, seq, RW_WIDTH), BF16),
        ],
        compiler_params=pltpu.CompilerParams(
            dimension_semantics=("arbitrary", "arbitrary"), vmem_limit_bytes=VMEM_LIMIT),
        name="in_proj_rwkv7_prep",
    )(x, x, x, *consts)


def _rw_scan_kernel(of_ref, ob_ref, vf_ref, vb_ref, gf_ref, gb_ref, yf_ref, yb_ref, s_ref):
    c = pl.program_id(1)

    @pl.when(c == 0)
    def _():
        s_ref[...] = jnp.zeros_like(s_ref)

    N, W = RW_HEAD_DIM, RW_WIDTH
    lane = lax.broadcasted_iota(jnp.int32, (CHUNK, LANES), 1)
    tt = lax.broadcasted_iota(jnp.int32, (CHUNK, LANES), 0)
    half = (lane < N, lane >= N)
    diag = ((lax.broadcasted_iota(jnp.int32, (LANES, LANES), 0) < N)
            == (lax.broadcasted_iota(jnp.int32, (LANES, LANES), 1) < N))
    dir_refs = ((of_ref, vf_ref, gf_ref, yf_ref), (ob_ref, vb_ref, gb_ref, yb_ref))

    pairs = [(q, d, j) for q in range(SCAN_BATCH) for d in range(N_DIR) for j in range(W // LANES)]
    probs = [(pi, par) for pi in range(len(pairs)) for par in range(2)]

    def op(pi, i):
        q, d, j = pairs[pi]
        return dir_refs[d][0][q, :, i * W + j * LANES:i * W + (j + 1) * LANES]

    r_p = [op(pi, 0) for pi in range(len(pairs))]
    a_p = [op(pi, 1) for pi in range(len(pairs))]
    k_p = [op(pi, 2) for pi in range(len(pairs))]
    b_p = [op(pi, 3) for pi in range(len(pairs))]
    ke_p = [op(pi, 4) for pi in range(len(pairs))]
    be_p = [op(pi, 5) for pi in range(len(pairs))]
    v_p = [dir_refs[d][1][q, :, j * LANES:(j + 1) * LANES] for q, d, j in pairs]
    s_p = [s_ref[q, d, j] for q, d, j in pairs]

    def lag(pi):
        d = pairs[pi][1]
        delta = tt - (lane & (N - 1))
        return delta if d == 0 else -delta

    gram = []
    for pi, par in probs:
        lhs = jnp.concatenate([a_p[pi], r_p[pi]], axis=0)
        lhs = jnp.where(jnp.concatenate([half[par], half[par]], axis=0), lhs, jnp.zeros_like(lhs))
        rhs = jnp.concatenate([b_p[pi], k_p[pi]], axis=0)
        gram.append(lax.dot_general(lhs, rhs, _NT, preferred_element_type=jnp.float32))

    za, zb, bot = [], [], []
    for n, (pi, par) in enumerate(probs):
        top = gram[n][:CHUNK]
        lg = lag(pi)
        m0 = jnp.where(jnp.where(half[0], lg, 0) > 0, -top, 0.0)
        l_ak = jnp.where(jnp.where(half[1], lg, 0) > 0, top, 0.0)
        bot.append(jnp.where(lg >= 0, gram[n][CHUNK:], 0.0).astype(BF16))
        vv = jnp.concatenate([v_p[pi], v_p[pi]], axis=0)
        zb.append(jnp.dot(l_ak.astype(BF16), vv, preferred_element_type=jnp.float32))
        a_own = a_p[pi].astype(jnp.float32)
        za.append(jnp.where(half[1], a_own if par == 1 else pltpu.roll(a_own, N, 1), m0))

    for level in range(6):
        skip = (2 ** level) // LHS_ROWS * LHS_ROWS
        new_za, new_zb = [], []
        for n, (pi, par) in enumerate(probs):
            live = slice(skip, CHUNK) if pairs[pi][1] == 0 else slice(0, CHUNK - skip)
            m = za[n][live, :N].astype(BF16)
            z2 = jnp.concatenate([za[n], zb[n]], axis=1).astype(BF16)
            p = jnp.dot(m, z2, preferred_element_type=jnp.float32)
            keep_a = jnp.where(half[1], za[n], 0.0)
            pieces_a = [keep_a[:live.start], p[:, :LANES] + keep_a[live], keep_a[live.stop:]]
            pieces_b = [zb[n][:live.start], p[:, LANES:] + zb[n][live], zb[n][live.stop:]]
            new_za.append(jnp.concatenate([x for x in pieces_a if x.shape[0]], axis=0))
            new_zb.append(jnp.concatenate([x for x in pieces_b if x.shape[0]], axis=0))
        za, zb = new_za, new_zb

    n_pairs = range(len(pairs))
    s_bf = [s_p[pi].astype(BF16) for pi in n_pairs]
    wr = [jnp.concatenate([jnp.where(half[0], pltpu.roll(za[2 * pi], N, 1), za[2 * pi + 1]).astype(BF16),
                           r_p[pi]], axis=0) for pi in n_pairs]
    ws = [lax.dot_general(wr[pi], s_bf[pi], _NT, preferred_element_type=jnp.float32) for pi in n_pairs]
    u_bf = [(-(ws[pi][:CHUNK] + jnp.where(half[0], zb[2 * pi], zb[2 * pi + 1]))).astype(BF16)
            for pi in n_pairs]
    uv = [jnp.concatenate([u_bf[pi], v_p[pi]], axis=0) for pi in n_pairs]
    y_eo = [jnp.dot(jnp.concatenate([bot[2 * pi], bot[2 * pi + 1]], axis=0), uv[pi],
                    preferred_element_type=jnp.float32) for pi in n_pairs]
    upd = [lax.dot_general(uv[pi], jnp.concatenate([be_p[pi], ke_p[pi]], axis=0), _TN,
                           preferred_element_type=jnp.float32) for pi in n_pairs]
    for pi, (q, d, j) in enumerate(pairs):
        y = ws[pi][CHUNK:] + jnp.where(half[0], y_eo[pi][:CHUNK], y_eo[pi][CHUNK:])
        dir_refs[d][3][q, :, j * LANES:(j + 1) * LANES] = y.astype(BF16)
        g_end = dir_refs[d][2][q, 0:1, j * LANES:(j + 1) * LANES]
        s_ref[q, d, j] = s_p[pi] * g_end + jnp.where(diag, upd[pi], 0.0)


def _rw_scan(ops, v_bf, gend):
    _, bsz, seq, _ = ops.shape
    n_chunks = seq // CHUNK
    rev = lambda c: n_chunks - 1 - c
    return pl.pallas_call(
        _rw_scan_kernel,
        grid=(bsz // SCAN_BATCH, n_chunks),
        in_specs=[
            pl.BlockSpec((None, SCAN_BATCH, CHUNK, RW_OPS * RW_WIDTH), lambda b, c: (0, b, c, 0)),
            pl.BlockSpec((None, SCAN_BATCH, CHUNK, RW_OPS * RW_WIDTH), lambda b, c: (1, b, rev(c), 0)),
            pl.BlockSpec((SCAN_BATCH, CHUNK, RW_WIDTH), lambda b, c: (b, c, 0)),
            pl.BlockSpec((SCAN_BATCH, CHUNK, RW_WIDTH), lambda b, c: (b, rev(c), 0)),
            pl.BlockSpec((None, SCAN_BATCH, None, HALO, RW_WIDTH), lambda b, c: (0, b, c, 0, 0)),
            pl.BlockSpec((None, SCAN_BATCH, None, HALO, RW_WIDTH), lambda b, c: (1, b, rev(c), 0, 0)),
        ],
        out_specs=[
            pl.BlockSpec((SCAN_BATCH, CHUNK, RW_WIDTH), lambda b, c: (b, c, 0)),
            pl.BlockSpec((SCAN_BATCH, CHUNK, RW_WIDTH), lambda b, c: (b, rev(c), 0)),
        ],
        out_shape=[jax.ShapeDtypeStruct((bsz, seq, RW_WIDTH), BF16)] * 2,
        scratch_shapes=[pltpu.VMEM((SCAN_BATCH, N_DIR, RW_WIDTH // LANES, LANES, LANES), jnp.float32)],
        compiler_params=pltpu.CompilerParams(
            dimension_semantics=("arbitrary", "arbitrary"), vmem_limit_bytes=VMEM_LIMIT),
        name="rwkv7_scan",
    )(ops, ops, v_bf, v_bf, gend, gend)


def _hg_kernel(qf_ref, ff_ref, if_ref, qb_ref, fb_ref, ib_ref, lbl_ref, tri_ref,
               of_ref, ob_ref, s_ref, cs_ref, vs_ref, *, layer):
    c = pl.program_id(1)

    @pl.when(c == 0)
    def _():
        s_ref[...] = jnp.zeros_like(s_ref)

    lg = lbl_ref[...]
    e = jnp.exp(lg - jnp.max(lg, axis=0, keepdims=True))
    lb = jnp.sum(e[0:layer + 1, :], axis=0, keepdims=True) / jnp.sum(e, axis=0, keepdims=True)

    heads = range(HG_HEADS)
    sls = [slice(h * HG_DK, (h + 1) * HG_DK) for h in heads]
    trow = lax.broadcasted_iota(jnp.int32, (SUB, HG_DK), 0)
    dir_refs = ((qf_ref, ff_ref, if_ref, of_ref), (qb_ref, fb_ref, ib_ref, ob_ref))
    probs = [(n, d) for n in range(SCAN_BATCH) for d in range(N_DIR)]

    def fast_path(k, out):
        n, d = probs[k]
        q_ref, f_ref, i_ref, _ = dir_refs[d]
        q, v = q_ref[n], i_ref[n]
        v_bf = v.astype(BF16)
        f = lb + (1.0 - lb) * _sigmoid(f_ref[n])
        kf = 1.0 - f
        yield
        b2 = _select_rows(tri_ref[d], jnp.log2(f), 2)
        yield
        (total,) = _chunk_totals(b2, d)
        q_in = (q * jnp.exp2(b2)).astype(BF16)
        k_end = (kf * jnp.exp2(total - b2)).astype(BF16)
        g_end = jnp.exp2(total)
        cs = b2 - jnp.log2(kf)
        for h in heads:
            cs_ref[k, h] = cs[:, sls[h]]
            vs_ref[k, h] = v[:, sls[h]]

        st = [s_ref[n, d, h] for h in heads]
        inter = [lax.dot_general(q_in[:, sls[h]], st[h].astype(BF16), _NT, preferred_element_type=jnp.float32)
                 for h in heads]
        for h in heads:
            s_ref[n, d, h] = st[h] * g_end[:, sls[h]] + lax.dot_general(
                v_bf[:, sls[h]], k_end[:, sls[h]], _TN, preferred_element_type=jnp.float32)
        yield

        def rows(p_lo, p_hi):
            lo = p_lo if d == 0 else CHUNK - p_hi
            return slice(lo, lo + (p_hi - p_lo))

        pending = []
        size = CHUNK // 2
        while size >= SUB:
            for p in range(0, CHUNK, 2 * size):
                s_sl, t_sl = rows(p, p + size), rows(p + size, p + 2 * size)
                ref = b2[rows(p + size - 1, p + size)]
                q_t = (q[t_sl] * jnp.exp2(b2[t_sl] - ref)).astype(BF16)
                k_s = (kf[s_sl] * jnp.exp2(ref - b2[s_sl])).astype(BF16)
                scores = [lax.dot_general(q_t[:, sls[h]], k_s[:, sls[h]], _NT,
                                          preferred_element_type=jnp.float32).astype(BF16) for h in heads]
                pending.append((t_sl, s_sl, scores))
            size //= 2
            yield

        n_blk = CHUNK // SUB
        zero_row = jnp.zeros((1, HG_WIDTH), jnp.float32)
        before = [(b2[i * SUB - 1:i * SUB] if i > 0 else zero_row) if d == 0 else
                  (b2[(i + 1) * SUB:(i + 1) * SUB + 1] if i < n_blk - 1 else zero_row) for i in range(n_blk)]
        drop = jnp.concatenate([jnp.broadcast_to(r, (SUB, HG_WIDTH)) for r in before], axis=0) - b2
        q_blk = (q * jnp.exp2(-drop)).astype(BF16)
        k_blk = (kf * jnp.exp2(drop)).astype(BF16)
        tt = lax.broadcasted_iota(jnp.int32, (CHUNK, CHUNK), 0)
        ss = lax.broadcasted_iota(jnp.int32, (CHUNK, CHUNK), 1)
        lag = jnp.where(tt // SUB == ss // SUB, tt - ss if d == 0 else ss - tt, -1)
        d_scores = [lax.dot_general(q_blk[:, sls[h]], k_blk[:, sls[h]], _NT, preferred_element_type=jnp.float32)
                    for h in heads]
        yield
        diag = [jnp.dot(jnp.where(lag >= 0, d_scores[h], 0.0).astype(BF16), v_bf[:, sls[h]],
                        preferred_element_type=jnp.float32) for h in heads]
        out[k] = (q, b2, v_bf, inter, pending, jnp.max(drop, axis=0, keepdims=True))
        yield
        for h in heads:
            dir_refs[d][3][n, :, sls[h]] = inter[h] + diag[h]
        add_pending(k, v_bf, pending)

    def add_pending(k, v_bf, pending):
        n, d = probs[k]
        for t_sl, s_sl, scores in pending:
            for h in heads:
                dir_refs[d][3][n, t_sl, sls[h]] += jnp.dot(scores[h], v_bf[s_sl, sls[h]],
                                                           preferred_element_type=jnp.float32)

    def explicit_diagonal(k, q, b2, inter):
        n, d = probs[k]
        for lo in range(0, CHUNK, SUB):
            q_i, b_i = q[lo:lo + SUB], b2[lo:lo + SUB]
            acc = [inter[h][lo:lo + SUB] for h in heads]
            for s in range(SUB):
                row_s = pl.ds(lo + s, SUB, stride=0)
                m = (trow >= s) if d == 0 else (trow <= s)
                for h in heads:
                    xs = jnp.where(m, q_i[:, sls[h]] * jnp.exp2(b_i[:, sls[h]] - cs_ref[k, h, row_s, :]), 0.0)
                    w = jnp.sum(xs, axis=-1, keepdims=True)
                    acc[h] = acc[h] + w * vs_ref[k, h, row_s, :]
            for h in heads:
                dir_refs[d][3][n, lo:lo + SUB, sls[h]] = acc[h]

    ctxs = [None] * len(probs)
    for _ in itertools.zip_longest(*[fast_path(k, ctxs) for k in range(len(probs))]):
        pass
    worst_drop = functools.reduce(jnp.maximum, [ctx[-1] for ctx in ctxs])

    @pl.when(jnp.max(worst_drop) > SAFE_LOG2_DROP)
    def _():
        for k, (q, b2, v_bf, inter, pending, _) in enumerate(ctxs):
            explicit_diagonal(k, q, b2, inter)
            add_pending(k, v_bf, pending)


def _hg_mix(p_hg, lb_logits, layer, tri):
    bsz, seq, _ = p_hg.shape
    n_chunks = seq // CHUNK
    rev = lambda c: n_chunks - 1 - c
    blk = (SCAN_BATCH, CHUNK, HG_WIDTH)
    fwd = lambda j: pl.BlockSpec(blk, lambda g, c: (g, c, j))
    bwd = lambda j: pl.BlockSpec(blk, lambda g, c: (g, rev(c), j))
    full = lambda a: pl.BlockSpec(a.shape, lambda g, c: (0,) * a.ndim)
    n_prob = SCAN_BATCH * N_DIR
    return pl.pallas_call(
        functools.partial(_hg_kernel, layer=layer),
        grid=(bsz // SCAN_BATCH, n_chunks),
        in_specs=[fwd(0), fwd(1), fwd(3), bwd(0), bwd(2), bwd(3), full(lb_logits), full(tri)],
        out_specs=[fwd(0), bwd(0)],
        out_shape=[jax.ShapeDtypeStruct((bsz, seq, HG_WIDTH), jnp.float32)] * 2,
        scratch_shapes=[pltpu.VMEM((SCAN_BATCH, N_DIR, HG_HEADS, HG_DK, HG_DK), jnp.float32),
                        pltpu.VMEM((n_prob, HG_HEADS, CHUNK, HG_DK), jnp.float32),
                        pltpu.VMEM((n_prob, HG_HEADS, CHUNK, HG_DK), jnp.float32)],
        compiler_params=pltpu.CompilerParams(
            dimension_semantics=("arbitrary", "arbitrary"), vmem_limit_bytes=VMEM_LIMIT),
        name="hgrn2_mix",
    )(p_hg, p_hg, p_hg, p_hg, p_hg, p_hg, lb_logits, tri)


def _out_kernel(x_ref, yf_ref, yb_ref, bonus_ref, gate_ref, of_ref, ob_ref, hgate_ref,
                lnw_ref, lnb_ref, hgn_ref, seg_ref, w_ref, post_ref, out_ref):
    y = yf_ref[...].astype(jnp.float32) + yb_ref[...].astype(jnp.float32)
    seg = seg_ref[...]
    mu = _head_sums(y, seg) * (1.0 / RW_HEAD_DIM)
    yc = y - mu
    var = _head_sums(yc * yc, seg) * (1.0 / RW_HEAD_DIM)
    y = yc * lax.rsqrt(var + GN_EPS) * lnw_ref[...] + lnb_ref[...]
    rw = (y + bonus_ref[...].astype(jnp.float32)) * gate_ref[...].astype(jnp.float32)

    o = of_ref[...] + ob_ref[...]
    parts = []
    for h in range(HG_HEADS):
        oh = o[:, h * HG_DK:(h + 1) * HG_DK]
        parts.append(oh * lax.rsqrt(jnp.mean(oh * oh, axis=-1, keepdims=True) + NORM_EPS))
    hg = jnp.concatenate(parts, axis=1) * hgn_ref[...] * hgate_ref[...].astype(jnp.float32)

    cat = jnp.concatenate([rw, hg], axis=1).astype(BF16)
    yo = jnp.dot(cat, w_ref[...], preferred_element_type=jnp.float32)
    ms = jnp.mean(yo * yo, axis=-1, keepdims=True)
    out_ref[...] = x_ref[...] + yo * lax.rsqrt(ms + NORM_EPS) * post_ref[...]


def _out_proj(x2, y_f, y_b, bonus, gate, o_f, o_b, hg_gate, ln_w, ln_b, hg_norm, seg, w_bf16, post_g):
    rows = x2.shape[0]
    tile = lambda n: pl.BlockSpec((OUT_TILE, n), lambda i: (i, 0))
    row_spec = lambda n: pl.BlockSpec((1, n), lambda i: (0, 0))
    return pl.pallas_call(
        _out_kernel,
        grid=(rows // OUT_TILE,),
        in_specs=[
            tile(D_MODEL),
            tile(RW_WIDTH), tile(RW_WIDTH), tile(RW_WIDTH), tile(RW_WIDTH),
            tile(HG_WIDTH), tile(HG_WIDTH), tile(HG_WIDTH),
            row_spec(RW_WIDTH), row_spec(RW_WIDTH), row_spec(HG_WIDTH),
            pl.BlockSpec((MXU_TILE, MXU_TILE), lambda i: (0, 0)),
            pl.BlockSpec((D_MODEL, D_MODEL), lambda i: (0, 0)),
            row_spec(D_MODEL),
        ],
        out_specs=tile(D_MODEL),
        out_shape=jax.ShapeDtypeStruct((rows, D_MODEL), jnp.float32),
        compiler_params=pltpu.CompilerParams(
            dimension_semantics=("arbitrary",), vmem_limit_bytes=VMEM_LIMIT),
        name="out_proj",
    )(x2, y_f, y_b, bonus, gate, o_f, o_b, hg_gate, ln_w, ln_b, hg_norm, seg, w_bf16, post_g)


def _lora_weights(w):
    z = jnp.zeros_like(w[0])
    wp = jnp.stack([jnp.concatenate([w[0], z], axis=0), jnp.concatenate([z, w[1]], axis=0)])
    return wp.astype(BF16)


def _scan_order_selectors(tile):
    t = jnp.arange(tile)
    same = (t[:, None] // CHUNK) == (t[None, :] // CHUNK)
    fwd = same & (t[None, :] <= t[:, None])
    bwd = same & (t[None, :] >= t[:, None])
    return jnp.stack([fwd, bwd]).astype(BF16)


def kernel(x, pre_norm_g, w_in, rw_shift_prev, rw_shift_next, rw_w0, rw_w2, rw_a0, rw_a2, rw_k_k, rw_k_a,
           rw_r_k, rw_ln_w, rw_ln_b, hg_lb_logits, hg_norm_g, w_out, post_norm_g):
    bsz, seq, dm = x.shape
    depth = w_in.shape[0]
    rows = bsz * seq
    lane = jnp.arange(MXU_TILE) // RW_HEAD_DIM
    seg = (lane[:, None] == lane[None, :]).astype(BF16)
    tri = _scan_order_selectors(ROW_TILE)
    hg_tri = _scan_order_selectors(CHUNK)
    row = lambda t: t.reshape(1, -1)
    flat = lambda t: t.reshape(rows, t.shape[-1])
    for l in range(depth):
        x2 = x.reshape(rows, dm)
        p_hg, hg_gate, ops, v_bf, gend, bonus, gate = _in_rw(
            x, row(pre_norm_g[l]), w_in[l].astype(BF16),
            row(rw_shift_prev[l]), row(rw_shift_next[l]),
            rw_w0[l], _lora_weights(rw_w2[l]), rw_a0[l], _lora_weights(rw_a2[l]),
            row(rw_k_k[l]), row(rw_k_a[l]), row(rw_r_k[l]), seg, tri)
        y_f, y_b = _rw_scan(ops, v_bf, gend)
        o_f, o_b = _hg_mix(p_hg, hg_lb_logits, l, hg_tri)
        out = _out_proj(
            x2, flat(y_f), flat(y_b), flat(bonus), flat(gate), flat(o_f), flat(o_b), flat(hg_gate),
            row(rw_ln_w[l]), row(rw_ln_b[l]), row(jnp.tile(hg_norm_g[l], HG_HEADS)), seg,
            w_out[l].astype(BF16), row(post_norm_g[l]))
        x = out.reshape(bsz, seq, dm)
    return x
```

```python
import functools
import itertools

import jax
import jax.numpy as jnp
from jax import lax
from jax.experimental import pallas as pl
from jax.experimental.pallas import tpu as pltpu

D_MODEL = 1024
RW_HEAD_DIM = 64
RW_WIDTH = 512
RW_HEADS = 8
HG_DK = 128
HG_WIDTH = 512
HG_HEADS = 4
LORA = 64
NORM_EPS = 1e-6
GN_EPS = 64e-5
RW_COLS = 4 * RW_WIDTH + 4 * LORA
HG_COLS = 5 * HG_WIDTH
HG_SCAN_COLS = 4 * HG_WIDTH
IN_COLS = RW_COLS + HG_COLS
N_DIR = 2
RW_OPS = 6

LANES = 128
LHS_ROWS = 16
MXU_TILE = 256
CHUNK = 64
SUB = 8
SAFE_LOG2_DROP = 100.0
ROW_TILE = 256
OUT_TILE = 512
HALO = 8
SCAN_BATCH = 4
VMEM_LIMIT = 56 * 1024 * 1024

_NN = (((1,), (0,)), ((), ()))
_NT = (((1,), (1,)), ((), ()))
_TN = (((0,), (0,)), ((), ()))
BF16 = jnp.bfloat16
LOG2E = 1.4426950408889634


def _split_bf16(x, terms=2):
    pieces = []
    for _ in range(terms - 1):
        hi = x.astype(BF16)
        pieces.append(hi)
        x = x - hi.astype(jnp.float32)
    pieces.append(x.astype(BF16))
    return pieces


def _select_rows(sel, x, terms):
    acc = None
    for piece in _split_bf16(x, terms):
        t = jnp.dot(sel, piece, preferred_element_type=jnp.float32)
        acc = t if acc is None else acc + t
    return acc


def _head_sums(x, seg):
    xb = x.astype(BF16)
    return jnp.concatenate([jnp.dot(xb[:, c:c + MXU_TILE], seg, preferred_element_type=jnp.float32)
                            for c in range(0, x.shape[1], MXU_TILE)], axis=1)


def _sigmoid(x):
    return 1.0 / (1.0 + jnp.exp(-x))


def _chunk_totals(cum, d):
    last = CHUNK - 1 if d == 0 else 0
    return [cum[c * CHUNK + last:c * CHUNK + last + 1] for c in range(cum.shape[0] // CHUNK)]


def _expand_chunks(rows):
    return jnp.concatenate([jnp.broadcast_to(r, (CHUNK, r.shape[1])) for r in rows], axis=0)


def _in_rw_kernel(xc_ref, xp_ref, xn_ref, g_ref, w_ref,
                  mup_ref, mun_ref, w0_ref, w2_ref, a0_ref, a2_ref,
                  kk_ref, ka_ref, rk_ref, seg_ref, tri_ref,
                  phg_ref, hgate_ref, ops_ref, v_ref, gend_ref, bonus_ref, gate_ref, *, n_tiles):
    j = pl.program_id(1)

    def pre_norm(x):
        return x * lax.rsqrt(jnp.mean(x * x, axis=-1, keepdims=True) + NORM_EPS) * g_ref[...]

    h_c = pre_norm(xc_ref[...])
    h_p = pre_norm(xp_ref[...]) * (j > 0).astype(jnp.float32)
    h_n = pre_norm(xn_ref[...]) * (j < n_tiles - 1).astype(jnp.float32)
    h = h_c.astype(BF16)
    h_ext = jnp.concatenate([h_p, h_c, h_n], axis=0).astype(BF16)

    hg_blocks = iter(range(HG_COLS // HG_WIDTH))

    def hg_block():
        i = next(hg_blocks, None)
        if i is not None:
            cols = slice(i * HG_WIDTH, (i + 1) * HG_WIDTH)
            blk = jnp.dot(h, w_ref[:, RW_COLS + cols.start:RW_COLS + cols.stop], preferred_element_type=jnp.float32)
            if i < HG_SCAN_COLS // HG_WIDTH:
                phg_ref[:, cols] = blk
            else:
                hgate_ref[...] = (blk * _sigmoid(blk)).astype(BF16)

    W = RW_WIDTH
    ext = ROW_TILE + 2 * HALO
    tile = slice(HALO, HALO + ROW_TILE)

    def shifted(cols):
        p_ext = jnp.dot(h_ext, w_ref[:, cols], preferred_element_type=jnp.float32)
        p, p_prev, p_next = p_ext[tile], pltpu.roll(p_ext, 1, 0)[tile], pltpu.roll(p_ext, ext - 1, 0)[tile]
        mup, mun = mup_ref[:, cols], mun_ref[:, cols]
        return p * (1.0 - mup - mun) + mup * p_prev + mun * p_next

    r, k, v, g = [shifted(slice(i * W, (i + 1) * W)) for i in range(4)]
    lora_in = shifted(slice(4 * W, RW_COLS))
    wd = lora_in[:, :2 * LORA]
    ad = lora_in[:, 2 * LORA:].astype(BF16)
    hg_block()
    gate_ref[...] = (g * _sigmoid(g)).astype(BF16)
    v_ref[...] = v.astype(BF16)

    seg = seg_ref[...]
    kk = k * kk_ref[...]
    kk = kk * lax.rsqrt(jnp.maximum(_head_sums(kk * kk, seg), 1e-24))
    tanh_wd = jnp.tanh(wd).astype(BF16)
    hg_block()

    k_sum = None
    for d in range(N_DIR):
        w_raw = w0_ref[d:d + 1, :] + jnp.dot(tanh_wd, w2_ref[d], preferred_element_type=jnp.float32)
        lw = (-LOG2E * jnp.exp(jnp.float32(-0.5))) * _sigmoid(w_raw)
        a = _sigmoid(a0_ref[d:d + 1, :] + jnp.dot(ad, a2_ref[d], preferred_element_type=jnp.float32))
        k_d = k * (1.0 + (a - 1.0) * ka_ref[...])
        b_d = kk * a
        k_sum = k_d if k_sum is None else k_sum + k_d
        cl = _select_rows(tri_ref[d], lw, 2)
        g_end = [jnp.exp2(tot) for tot in _chunk_totals(cl, d)]
        hg_block()
        e_out = jnp.exp2(-cl)
        e_end = e_out * _expand_chunks(g_end)
        ops = (r * jnp.exp2(cl),
               kk * jnp.exp2(cl - lw),
               k_d * e_out, b_d * e_out,
               k_d * e_end, b_d * e_end)
        for i, op in enumerate(ops):
            ops_ref[d, :, i * W:(i + 1) * W] = op.astype(BF16)
        for q, g_q in enumerate(g_end):
            gend_ref[d, q] = jnp.broadcast_to(g_q, (HALO, W))
        hg_block()

    bonus_ref[...] = (_head_sums(r * k_sum * rk_ref[...], seg) * v).astype(BF16)
    for _ in hg_blocks:
        raise AssertionError("HGRN2 column blocks left unissued")


def _in_rw(x, g, w_in, mup, mun, w0, w2, a0, a2, k_k, k_a, r_k, seg, tri):
    bsz, seq, _ = x.shape
    n_tiles = seq // ROW_TILE
    hb = ROW_TILE // HALO
    n_halo = seq // HALO
    n_chunks = seq // CHUNK
    cpt = ROW_TILE // CHUNK
    full = lambda a: pl.BlockSpec(a.shape, lambda b, j: (0,) * a.ndim)
    tile = lambda n: pl.BlockSpec((None, ROW_TILE, n), lambda b, j: (b, j, 0))
    consts = (g, w_in, mup, mun, w0, w2, a0, a2, k_k, k_a, r_k, seg, tri)
    return pl.pallas_call(
        functools.partial(_in_rw_kernel, n_tiles=n_tiles),
        grid=(bsz, n_tiles),
        in_specs=[
            tile(D_MODEL),
            pl.BlockSpec((None, HALO, D_MODEL), lambda b, j: (b, jnp.maximum(j * hb - 1, 0), 0)),
            pl.BlockSpec((None, HALO, D_MODEL), lambda b, j: (b, jnp.minimum((j + 1) * hb, n_halo - 1), 0)),
            *[full(a) for a in consts],
        ],
        out_specs=[
            tile(HG_SCAN_COLS),
            tile(HG_WIDTH),
            pl.BlockSpec((N_DIR, None, ROW_TILE, RW_OPS * RW_WIDTH), lambda b, j: (0, b, j, 0)),
            tile(RW_WIDTH),
            pl.BlockSpec((N_DIR, None, cpt, HALO, RW_WIDTH), lambda b, j: (0, b, j, 0, 0)),
            tile(RW_WIDTH),
            tile(RW_WIDTH),
        ],
        out_shape=[
            jax.ShapeDtypeStruct((bsz, seq, HG_SCAN_COLS), jnp.float32),
            jax.ShapeDtypeStruct((bsz, seq, HG_WIDTH), BF16),
            jax.ShapeDtypeStruct((N_DIR, bsz, seq, RW_OPS * RW_WIDTH), BF16),
            jax.ShapeDtypeStruct((bsz, seq, RW_WIDTH), BF16),
            jax.ShapeDtypeStruct((N_DIR, bsz, n_chunks, HALO, RW_WIDTH), jnp.float32),
            jax.ShapeDtypeStruct((bsz, seq, RW_WIDTH), BF16),
            jax.ShapeDtypeStruct((bsz, seq, RW_WIDTH), BF16),
        ],
        compiler_params=pltpu.CompilerParams(
            dimension_semantics=("arbitrary", "arbitrary"), vmem_limit_bytes=VMEM_LIMIT),
        name="in_proj_rwkv7_prep",
    )(x, x, x, *consts)


def _rw_scan_kernel(of_ref, ob_ref, vf_ref, vb_ref, gf_ref, gb_ref, yf_ref, yb_ref, s_ref):
    c = pl.program_id(1)

    @pl.when(c == 0)
    def _():
        s_ref[...] = jnp.zeros_like(s_ref)

    N, W = RW_HEAD_DIM, RW_WIDTH
    lane = lax.broadcasted_iota(jnp.int32, (CHUNK, LANES), 1)
    tt = lax.broadcasted_iota(jnp.int32, (CHUNK, LANES), 0)
    half = (lane < N, lane >= N)
    diag = ((lax.broadcasted_iota(jnp.int32, (LANES, LANES), 0) < N)
            == (lax.broadcasted_iota(jnp.int32, (LANES, LANES), 1) < N))
    dir_refs = ((of_ref, vf_ref, gf_ref, yf_ref), (ob_ref, vb_ref, gb_ref, yb_ref))

    pairs = [(q, d, j) for q in range(SCAN_BATCH) for d in range(N_DIR) for j in range(W // LANES)]
    probs = [(pi, par) for pi in range(len(pairs)) for par in range(2)]

    def op(pi, i):
        q, d, j = pairs[pi]
        return dir_refs[d][0][q, :, i * W + j * LANES:i * W + (j + 1) * LANES]

    r_p = [op(pi, 0) for pi in range(len(pairs))]
    a_p = [op(pi, 1) for pi in range(len(pairs))]
    k_p = [op(pi, 2) for pi in range(len(pairs))]
    b_p = [op(pi, 3) for pi in range(len(pairs))]
    ke_p = [op(pi, 4) for pi in range(len(pairs))]
    be_p = [op(pi, 5) for pi in range(len(pairs))]
    v_p = [dir_refs[d][1][q, :, j * LANES:(j + 1) * LANES] for q, d, j in pairs]
    s_p = [s_ref[q, d, j] for q, d, j in pairs]

    def lag(pi):
        d = pairs[pi][1]
        delta = tt - (lane & (N - 1))
        return delta if d == 0 else -delta

    gram = []
    for pi, par in probs:
        lhs = jnp.concatenate([a_p[pi], r_p[pi]], axis=0)
        lhs = jnp.where(jnp.concatenate([half[par], half[par]], axis=0), lhs, jnp.zeros_like(lhs))
        rhs = jnp.concatenate([b_p[pi], k_p[pi]], axis=0)
        gram.append(lax.dot_general(lhs, rhs, _NT, preferred_element_type=jnp.float32))

    za, zb, bot = [], [], []
    for n, (pi, par) in enumerate(probs):
        top = gram[n][:CHUNK]
        lg = lag(pi)
        m0 = jnp.where(jnp.where(half[0], lg, 0) > 0, -top, 0.0)
        l_ak = jnp.where(jnp.where(half[1], lg, 0) > 0, top, 0.0)
        bot.append(jnp.where(lg >= 0, gram[n][CHUNK:], 0.0).astype(BF16))
        vv = jnp.concatenate([v_p[pi], v_p[pi]], axis=0)
        zb.append(jnp.dot(l_ak.astype(BF16), vv, preferred_element_type=jnp.float32))
        a_own = a_p[pi].astype(jnp.float32)
        za.append(jnp.where(half[1], a_own if par == 1 else pltpu.roll(a_own, N, 1), m0))

    for level in range(6):
        skip = (2 ** level) // LHS_ROWS * LHS_ROWS
        new_za, new_zb = [], []
        for n, (pi, par) in enumerate(probs):
            live = slice(skip, CHUNK) if pairs[pi][1] == 0 else slice(0, CHUNK - skip)
            m = za[n][live, :N].astype(BF16)
            z2 = jnp.concatenate([za[n], zb[n]], axis=1).astype(BF16)
            p = jnp.dot(m, z2, preferred_element_type=jnp.float32)
            keep_a = jnp.where(half[1], za[n], 0.0)
            pieces_a = [keep_a[:live.start], p[:, :LANES] + keep_a[live], keep_a[live.stop:]]
            pieces_b = [zb[n][:live.start], p[:, LANES:] + zb[n][live], zb[n][live.stop:]]
            new_za.append(jnp.concatenate([x for x in pieces_a if x.shape[0]], axis=0))
            new_zb.append(jnp.concatenate([x for x in pieces_b if x.shape[0]], axis=0))
        za, zb = new_za, new_zb

    n_pairs = range(len(pairs))
    s_bf = [s_p[pi].astype(BF16) for pi in n_pairs]
    wr = [jnp.concatenate([jnp.where(half[0], pltpu.roll(za[2 * pi], N, 1), za[2 * pi + 1]).astype(BF16),
                           r_p[pi]], axis=0) for pi in n_pairs]
    ws = [lax.dot_general(wr[pi], s_bf[pi], _NT, preferred_element_type=jnp.float32) for pi in n_pairs]
    u_bf = [(-(ws[pi][:CHUNK] + jnp.where(half[0], zb[2 * pi], zb[2 * pi + 1]))).astype(BF16)
            for pi in n_pairs]
    uv = [jnp.concatenate([u_bf[pi], v_p[pi]], axis=0) for pi in n_pairs]
    y_eo = [jnp.dot(jnp.concatenate([bot[2 * pi], bot[2 * pi + 1]], axis=0), uv[pi],
                    preferred_element_type=jnp.float32) for pi in n_pairs]
    upd = [lax.dot_general(uv[pi], jnp.concatenate([be_p[pi], ke_p[pi]], axis=0), _TN,
                           preferred_element_type=jnp.float32) for pi in n_pairs]
    for pi, (q, d, j) in enumerate(pairs):
        y = ws[pi][CHUNK:] + jnp.where(half[0], y_eo[pi][:CHUNK], y_eo[pi][CHUNK:])
        dir_refs[d][3][q, :, j * LANES:(j + 1) * LANES] = y.astype(BF16)
        g_end = dir_refs[d][2][q, 0:1, j * LANES:(j + 1) * LANES]
        s_ref[q, d, j] = s_p[pi] * g_end + jnp.where(diag, upd[pi], 0.0)


def _rw_scan(ops, v_bf, gend):
    _, bsz, seq, _ = ops.shape
    n_chunks = seq // CHUNK
    rev = lambda c: n_chunks - 1 - c
    return pl.pallas_call(
        _rw_scan_kernel,
        grid=(bsz // SCAN_BATCH, n_chunks),
        in_specs=[
            pl.BlockSpec((None, SCAN_BATCH, CHUNK, RW_OPS * RW_WIDTH), lambda b, c: (0, b, c, 0)),
            pl.BlockSpec((None, SCAN_BATCH, CHUNK, RW_OPS * RW_WIDTH), lambda b, c: (1, b, rev(c), 0)),
            pl.BlockSpec((SCAN_BATCH, CHUNK, RW_WIDTH), lambda b, c: (b, c, 0)),
            pl.BlockSpec((SCAN_BATCH, CHUNK, RW_WIDTH), lambda b, c: (b, rev(c), 0)),
            pl.BlockSpec((None, SCAN_BATCH, None, HALO, RW_WIDTH), lambda b, c: (0, b, c, 0, 0)),
            pl.BlockSpec((None, SCAN_BATCH, None, HALO, RW_WIDTH), lambda b, c: (1, b, rev(c), 0, 0)),
        ],
        out_specs=[
            pl.BlockSpec((SCAN_BATCH, CHUNK, RW_WIDTH), lambda b, c: (b, c, 0)),
            pl.BlockSpec((SCAN_BATCH, CHUNK, RW_WIDTH), lambda b, c: (b, rev(c), 0)),
        ],
        out_shape=[jax.ShapeDtypeStruct((bsz, seq, RW_WIDTH), BF16)] * 2,
        scratch_shapes=[pltpu.VMEM((SCAN_BATCH, N_DIR, RW_WIDTH // LANES, LANES, LANES), jnp.float32)],
        compiler_params=pltpu.CompilerParams(
            dimension_semantics=("arbitrary", "arbitrary"), vmem_limit_bytes=VMEM_LIMIT),
        name="rwkv7_scan",
    )(ops, ops, v_bf, v_bf, gend, gend)


def _hg_kernel(qf_ref, ff_ref, if_ref, qb_ref, fb_ref, ib_ref, lbl_ref, tri_ref,
               of_ref, ob_ref, s_ref, cs_ref, vs_ref, *, layer):
    c = pl.program_id(1)

    @pl.when(c == 0)
    def _():
        s_ref[...] = jnp.zeros_like(s_ref)

    lg = lbl_ref[...]
    e = jnp.exp(lg - jnp.max(lg, axis=0, keepdims=True))
    lb = jnp.sum(e[0:layer + 1, :], axis=0, keepdims=True) / jnp.sum(e, axis=0, keepdims=True)

    heads = range(HG_HEADS)
    sls = [slice(h * HG_DK, (h + 1) * HG_DK) for h in heads]
    trow = lax.broadcasted_iota(jnp.int32, (SUB, HG_DK), 0)
    dir_refs = ((qf_ref, ff_ref, if_ref, of_ref), (qb_ref, fb_ref, ib_ref, ob_ref))
    probs = [(n, d) for n in range(SCAN_BATCH) for d in range(N_DIR)]

    def fast_path(k, out):
        n, d = probs[k]
        q_ref, f_ref, i_ref, o_ref = dir_refs[d]
        q, v = q_ref[n], i_ref[n]
        v_bf = v.astype(BF16)
        f = lb + (1.0 - lb) * _sigmoid(f_ref[n])
        kf = 1.0 - f
        yield
        b2 = _select_rows(tri_ref[d], jnp.log2(f), 2)
        yield
        (total,) = _chunk_totals(b2, d)
        q_in = (q * jnp.exp2(b2)).astype(BF16)
        k_end = (kf * jnp.exp2(total - b2)).astype(BF16)
        g_end = jnp.exp2(total)
        st = [s_ref[n, d, h] for h in heads]
        inter = [lax.dot_general(q_in[:, sls[h]], st[h].astype(BF16), _NT, preferred_element_type=jnp.float32)
                 for h in heads]
        for h in heads:
            s_ref[n, d, h] = st[h] * g_end[:, sls[h]] + lax.dot_general(
                v_bf[:, sls[h]], k_end[:, sls[h]], _TN, preferred_element_type=jnp.float32)
        yield
        mid = CHUNK // 2 - 1 if d == 0 else CHUNK // 2
        dev = b2 - b2[mid:mid + 1]
        q_c = (q * jnp.exp2(dev)).astype(BF16)
        k_c = (kf * jnp.exp2(-dev)).astype(BF16)
        scores = [lax.dot_general(q_c[:, sls[h]], k_c[:, sls[h]], _NT, preferred_element_type=jnp.float32)
                  for h in heads]
        out[k] = (q, kf, b2, v, v_bf, inter, jnp.max(jnp.abs(dev), axis=0, keepdims=True))
        yield
        tt = lax.broadcasted_iota(jnp.int32, (CHUNK, CHUNK), 0)
        ss = lax.broadcasted_iota(jnp.int32, (CHUNK, CHUNK), 1)
        seen = (tt >= ss) if d == 0 else (tt <= ss)
        for h in heads:
            o_ref[n, :, sls[h]] = inter[h] + jnp.dot(jnp.where(seen, scores[h], 0.0).astype(BF16),
                                                      v_bf[:, sls[h]], preferred_element_type=jnp.float32)

    def safe_path(k, q, kf, b2, v, v_bf, inter):
        n, d = probs[k]
        o_ref = dir_refs[d][3]
        cs = b2 - jnp.log2(kf)
        for h in heads:
            cs_ref[h] = cs[:, sls[h]]
            vs_ref[h] = v[:, sls[h]]
        for lo in range(0, CHUNK, SUB):
            q_i, b_i = q[lo:lo + SUB], b2[lo:lo + SUB]
            acc = [inter[h][lo:lo + SUB] for h in heads]
            for s in range(SUB):
                row_s = pl.ds(lo + s, SUB, stride=0)
                m = (trow >= s) if d == 0 else (trow <= s)
                for h in heads:
                    xs = jnp.where(m, q_i[:, sls[h]] * jnp.exp2(b_i[:, sls[h]] - cs_ref[h, row_s, :]), 0.0)
                    w = jnp.sum(xs, axis=-1, keepdims=True)
                    acc[h] = acc[h] + w * vs_ref[h, row_s, :]
            for h in heads:
                o_ref[n, lo:lo + SUB, sls[h]] = acc[h]

        def rows(p_lo, p_hi):
            lo = p_lo if d == 0 else CHUNK - p_hi
            return slice(lo, lo + (p_hi - p_lo))

        size = CHUNK // 2
        while size >= SUB:
            for p in range(0, CHUNK, 2 * size):
                s_sl, t_sl = rows(p, p + size), rows(p + size, p + 2 * size)
                ref = b2[rows(p + size - 1, p + size)]
                q_t = (q[t_sl] * jnp.exp2(b2[t_sl] - ref)).astype(BF16)
                k_s = (kf[s_sl] * jnp.exp2(ref - b2[s_sl])).astype(BF16)
                for h in heads:
                    sc = lax.dot_general(q_t[:, sls[h]], k_s[:, sls[h]], _NT, preferred_element_type=jnp.float32)
                    o_ref[n, t_sl, sls[h]] += jnp.dot(sc.astype(BF16), v_bf[s_sl, sls[h]],
                                                      preferred_element_type=jnp.float32)
            size //= 2

    ctxs = [None] * len(probs)
    for _ in itertools.zip_longest(*[fast_path(k, ctxs) for k in range(len(probs))]):
        pass
    worst = functools.reduce(jnp.maximum, [ctx[-1] for ctx in ctxs])

    @pl.when(jnp.max(worst) > SAFE_LOG2_DROP)
    def _():
        for k, ctx in enumerate(ctxs):
            safe_path(k, *ctx[:-1])


def _hg_mix(p_hg, lb_logits, layer, tri):
    bsz, seq, _ = p_hg.shape
    n_chunks = seq // CHUNK
    rev = lambda c: n_chunks - 1 - c
    blk = (SCAN_BATCH, CHUNK, HG_WIDTH)
    fwd = lambda j: pl.BlockSpec(blk, lambda g, c: (g, c, j))
    bwd = lambda j: pl.BlockSpec(blk, lambda g, c: (g, rev(c), j))
    full = lambda a: pl.BlockSpec(a.shape, lambda g, c: (0,) * a.ndim)
    return pl.pallas_call(
        functools.partial(_hg_kernel, layer=layer),
        grid=(bsz // SCAN_BATCH, n_chunks),
        in_specs=[fwd(0), fwd(1), fwd(3), bwd(0), bwd(2), bwd(3), full(lb_logits), full(tri)],
        out_specs=[fwd(0), bwd(0)],
        out_shape=[jax.ShapeDtypeStruct((bsz, seq, HG_WIDTH), jnp.float32)] * 2,
        scratch_shapes=[pltpu.VMEM((SCAN_BATCH, N_DIR, HG_HEADS, HG_DK, HG_DK), jnp.float32),
                        pltpu.VMEM((HG_HEADS, CHUNK, HG_DK), jnp.float32),
                        pltpu.VMEM((HG_HEADS, CHUNK, HG_DK), jnp.float32)],
        compiler_params=pltpu.CompilerParams(
            dimension_semantics=("arbitrary", "arbitrary"), vmem_limit_bytes=VMEM_LIMIT),
        name="hgrn2_mix",
    )(p_hg, p_hg, p_hg, p_hg, p_hg, p_hg, lb_logits, tri)


def _out_kernel(x_ref, yf_ref, yb_ref, bonus_ref, gate_ref, of_ref, ob_ref, hgate_ref,
                lnw_ref, lnb_ref, hgn_ref, seg_ref, w_ref, post_ref, out_ref):
    y = yf_ref[...].astype(jnp.float32) + yb_ref[...].astype(jnp.float32)
    seg = seg_ref[...]
    mu = _head_sums(y, seg) * (1.0 / RW_HEAD_DIM)
    yc = y - mu
    var = _head_sums(yc * yc, seg) * (1.0 / RW_HEAD_DIM)
    y = yc * lax.rsqrt(var + GN_EPS) * lnw_ref[...] + lnb_ref[...]
    rw = (y + bonus_ref[...].astype(jnp.float32)) * gate_ref[...].astype(jnp.float32)

    o = of_ref[...] + ob_ref[...]
    parts = []
    for h in range(HG_HEADS):
        oh = o[:, h * HG_DK:(h + 1) * HG_DK]
        parts.append(oh * lax.rsqrt(jnp.mean(oh * oh, axis=-1, keepdims=True) + NORM_EPS))
    hg = jnp.concatenate(parts, axis=1) * hgn_ref[...] * hgate_ref[...].astype(jnp.float32)

    cat = jnp.concatenate([rw, hg], axis=1).astype(BF16)
    yo = jnp.dot(cat, w_ref[...], preferred_element_type=jnp.float32)
    ms = jnp.mean(yo * yo, axis=-1, keepdims=True)
    out_ref[...] = x_ref[...] + yo * lax.rsqrt(ms + NORM_EPS) * post_ref[...]


def _out_proj(x2, y_f, y_b, bonus, gate, o_f, o_b, hg_gate, ln_w, ln_b, hg_norm, seg, w_bf16, post_g):
    rows = x2.shape[0]
    tile = lambda n: pl.BlockSpec((OUT_TILE, n), lambda i: (i, 0))
    row_spec = lambda n: pl.BlockSpec((1, n), lambda i: (0, 0))
    return pl.pallas_call(
        _out_kernel,
        grid=(rows // OUT_TILE,),
        in_specs=[
            tile(D_MODEL),
            tile(RW_WIDTH), tile(RW_WIDTH), tile(RW_WIDTH), tile(RW_WIDTH),
            tile(HG_WIDTH), tile(HG_WIDTH), tile(HG_WIDTH),
            row_spec(RW_WIDTH), row_spec(RW_WIDTH), row_spec(HG_WIDTH),
            pl.BlockSpec((MXU_TILE, MXU_TILE), lambda i: (0, 0)),
            pl.BlockSpec((D_MODEL, D_MODEL), lambda i: (0, 0)),
            row_spec(D_MODEL),
        ],
        out_specs=tile(D_MODEL),
        out_shape=jax.ShapeDtypeStruct((rows, D_MODEL), jnp.float32),
        compiler_params=pltpu.CompilerParams(
            dimension_semantics=("arbitrary",), vmem_limit_bytes=VMEM_LIMIT),
        name="out_proj",
    )(x2, y_f, y_b, bonus, gate, o_f, o_b, hg_gate, ln_w, ln_b, hg_norm, seg, w_bf16, post_g)


def _lora_weights(w):
    z = jnp.zeros_like(w[0])
    wp = jnp.stack([jnp.concatenate([w[0], z], axis=0), jnp.concatenate([z, w[1]], axis=0)])
    return wp.astype(BF16)


def _scan_order_selectors(tile):
    t = jnp.arange(tile)
    same = (t[:, None] // CHUNK) == (t[None, :] // CHUNK)
    fwd = same & (t[None, :] <= t[:, None])
    bwd = same & (t[None, :] >= t[:, None])
    return jnp.stack([fwd, bwd]).astype(BF16)


def kernel(x, pre_norm_g, w_in, rw_shift_prev, rw_shift_next, rw_w0, rw_w2, rw_a0, rw_a2, rw_k_k, rw_k_a,
           rw_r_k, rw_ln_w, rw_ln_b, hg_lb_logits, hg_norm_g, w_out, post_norm_g):
    bsz, seq, dm = x.shape
    depth = w_in.shape[0]
    rows = bsz * seq
    lane = jnp.arange(MXU_TILE) // RW_HEAD_DIM
    seg = (lane[:, None] == lane[None, :]).astype(BF16)
    tri = _scan_order_selectors(ROW_TILE)
    hg_tri = _scan_order_selectors(CHUNK)
    row = lambda t: t.reshape(1, -1)
    flat = lambda t: t.reshape(rows, t.shape[-1])
    for l in range(depth):
        x2 = x.reshape(rows, dm)
        p_hg, hg_gate, ops, v_bf, gend, bonus, gate = _in_rw(
            x, row(pre_norm_g[l]), w_in[l].astype(BF16),
            row(rw_shift_prev[l]), row(rw_shift_next[l]),
            rw_w0[l], _lora_weights(rw_w2[l]), rw_a0[l], _lora_weights(rw_a2[l]),
            row(rw_k_k[l]), row(rw_k_a[l]), row(rw_r_k[l]), seg, tri)
        y_f, y_b = _rw_scan(ops, v_bf, gend)
        o_f, o_b = _hg_mix(p_hg, hg_lb_logits, l, hg_tri)
        out = _out_proj(
            x2, flat(y_f), flat(y_b), flat(bonus), flat(gate), flat(o_f), flat(o_b), flat(hg_gate),
            row(rw_ln_w[l]), row(rw_ln_b[l]), row(jnp.tile(hg_norm_g[l], HG_HEADS)), seg,
            w_out[l].astype(BF16), row(post_norm_g[l]))
        x = out.reshape(bsz, seq, dm)
    return x
```

```python
import functools
import itertools

import jax
import jax.numpy as jnp
from jax import lax
from jax.experimental import pallas as pl
from jax.experimental.pallas import tpu as pltpu

D_MODEL = 1024
RW_HEAD_DIM = 64
RW_WIDTH = 512
RW_HEADS = 8
HG_DK = 128
HG_WIDTH = 512
HG_HEADS = 4
LORA = 64
NORM_EPS = 1e-6
GN_EPS = 64e-5
RW_COLS = 4 * RW_WIDTH + 4 * LORA
HG_COLS = 5 * HG_WIDTH
HG_SCAN_COLS = 4 * HG_WIDTH
IN_COLS = RW_COLS + HG_COLS
N_DIR = 2
RW_OPS = 4

LANES = 128
LHS_ROWS = 16
MXU_TILE = 256
CHUNK = 64
SUB = 8
SAFE_LOG2_DROP = 100.0
ROW_TILE = 256
OUT_TILE = 512
HALO = 8
SCAN_BATCH = 4
VMEM_LIMIT = 56 * 1024 * 1024

_NN = (((1,), (0,)), ((), ()))
_NT = (((1,), (1,)), ((), ()))
_TN = (((0,), (0,)), ((), ()))
BF16 = jnp.bfloat16
LOG2E = 1.4426950408889634


def _split_bf16(x, terms=2):
    pieces = []
    for _ in range(terms - 1):
        hi = x.astype(BF16)
        pieces.append(hi)
        x = x - hi.astype(jnp.float32)
    pieces.append(x.astype(BF16))
    return pieces


def _select_rows(sel, x, terms):
    acc = None
    for piece in _split_bf16(x, terms):
        t = jnp.dot(sel, piece, preferred_element_type=jnp.float32)
        acc = t if acc is None else acc + t
    return acc


def _head_sums(x, seg):
    xb = x.astype(BF16)
    return jnp.concatenate([jnp.dot(xb[:, c:c + MXU_TILE], seg, preferred_element_type=jnp.float32)
                            for c in range(0, x.shape[1], MXU_TILE)], axis=1)


def _sigmoid(x):
    return 1.0 / (1.0 + jnp.exp(-x))


class _PerPair:
    def __init__(self, load):
        self._load = load

    def __getitem__(self, pi):
        return self._load(pi)


def _chunk_totals(cum, d):
    last = CHUNK - 1 if d == 0 else 0
    return [cum[c * CHUNK + last:c * CHUNK + last + 1] for c in range(cum.shape[0] // CHUNK)]


def _expand_chunks(rows):
    return jnp.concatenate([jnp.broadcast_to(r, (CHUNK, r.shape[1])) for r in rows], axis=0)


def _in_rw_kernel(xc_ref, xp_ref, xn_ref, g_ref, w_ref,
                  mup_ref, mun_ref, w0_ref, w2_ref, a0_ref, a2_ref,
                  kk_ref, ka_ref, rk_ref, seg_ref, tri_ref,
                  phg_ref, hgate_ref, ops_ref, bkt_ref, v_ref, gend_ref, bonus_ref, gate_ref, *, n_tiles):
    j = pl.program_id(1)

    def pre_norm(x):
        return x * lax.rsqrt(jnp.mean(x * x, axis=-1, keepdims=True) + NORM_EPS) * g_ref[...]

    h_c = pre_norm(xc_ref[...])
    h_p = pre_norm(xp_ref[...]) * (j > 0).astype(jnp.float32)
    h_n = pre_norm(xn_ref[...]) * (j < n_tiles - 1).astype(jnp.float32)
    h = h_c.astype(BF16)
    h_ext = jnp.concatenate([h_p, h_c, h_n], axis=0).astype(BF16)

    hg_blocks = iter(range(HG_COLS // HG_WIDTH))

    def hg_block():
        i = next(hg_blocks, None)
        if i is not None:
            cols = slice(i * HG_WIDTH, (i + 1) * HG_WIDTH)
            blk = jnp.dot(h, w_ref[:, RW_COLS + cols.start:RW_COLS + cols.stop], preferred_element_type=jnp.float32)
            if i < HG_SCAN_COLS // HG_WIDTH:
                phg_ref[:, cols] = blk
            else:
                hgate_ref[...] = (blk * _sigmoid(blk)).astype(BF16)

    W = RW_WIDTH
    ext = ROW_TILE + 2 * HALO
    tile = slice(HALO, HALO + ROW_TILE)

    def shifted(cols):
        p_ext = jnp.dot(h_ext, w_ref[:, cols], preferred_element_type=jnp.float32)
        p, p_prev, p_next = p_ext[tile], pltpu.roll(p_ext, 1, 0)[tile], pltpu.roll(p_ext, ext - 1, 0)[tile]
        mup, mun = mup_ref[:, cols], mun_ref[:, cols]
        return p * (1.0 - mup - mun) + mup * p_prev + mun * p_next

    r, k, v, g = [shifted(slice(i * W, (i + 1) * W)) for i in range(4)]
    lora_in = shifted(slice(4 * W, RW_COLS))
    wd = lora_in[:, :2 * LORA]
    ad = lora_in[:, 2 * LORA:].astype(BF16)
    hg_block()
    gate_ref[...] = (g * _sigmoid(g)).astype(BF16)
    v_ref[...] = v.astype(BF16)

    seg = seg_ref[...]
    kk = k * kk_ref[...]
    kk = kk * lax.rsqrt(jnp.maximum(_head_sums(kk * kk, seg), 1e-24))
    tanh_wd = jnp.tanh(wd).astype(BF16)
    hg_block()

    k_sum = None
    for d in range(N_DIR):
        w_raw = w0_ref[d:d + 1, :] + jnp.dot(tanh_wd, w2_ref[d], preferred_element_type=jnp.float32)
        lw = (-LOG2E * jnp.exp(jnp.float32(-0.5))) * _sigmoid(w_raw)
        a = _sigmoid(a0_ref[d:d + 1, :] + jnp.dot(ad, a2_ref[d], preferred_element_type=jnp.float32))
        k_d = k * (1.0 + (a - 1.0) * ka_ref[...])
        b_d = kk * a
        k_sum = k_d if k_sum is None else k_sum + k_d
        cl = _select_rows(tri_ref[d], lw, 2)
        g_end = [jnp.exp2(tot) for tot in _chunk_totals(cl, d)]
        hg_block()
        e_out = jnp.exp2(-cl)
        e_end = e_out * _expand_chunks(g_end)
        ops = (r * jnp.exp2(cl),
               kk * jnp.exp2(cl - lw),
               k_d * e_end, b_d * e_end)
        for i, op in enumerate(ops):
            ops_ref[d, :, i * W:(i + 1) * W] = op.astype(BF16)
        b_t, k_t = b_d * e_out, k_d * e_out
        for q in range(ROW_TILE // CHUNK):
            rows = slice(q * CHUNK, (q + 1) * CHUNK)
            for t in range(W // LANES):
                cols = slice(t * LANES, (t + 1) * LANES)
                bk = jnp.concatenate([b_t[rows, cols], k_t[rows, cols]], axis=0)
                bkt_ref[d, q, t] = bk.astype(BF16).T
        for q, g_q in enumerate(g_end):
            gend_ref[d, q] = jnp.broadcast_to(g_q, (HALO, W))
        hg_block()

    bonus_ref[...] = (_head_sums(r * k_sum * rk_ref[...], seg) * v).astype(BF16)
    for _ in hg_blocks:
        raise AssertionError("HGRN2 column blocks left unissued")


def _in_rw(x, g, w_in, mup, mun, w0, w2, a0, a2, k_k, k_a, r_k, seg, tri):
    bsz, seq, _ = x.shape
    n_tiles = seq // ROW_TILE
    hb = ROW_TILE // HALO
    n_halo = seq // HALO
    n_chunks = seq // CHUNK
    cpt = ROW_TILE // CHUNK
    full = lambda a: pl.BlockSpec(a.shape, lambda b, j: (0,) * a.ndim)
    tile = lambda n: pl.BlockSpec((None, ROW_TILE, n), lambda b, j: (b, j, 0))
    consts = (g, w_in, mup, mun, w0, w2, a0, a2, k_k, k_a, r_k, seg, tri)
    return pl.pallas_call(
        functools.partial(_in_rw_kernel, n_tiles=n_tiles),
        grid=(bsz, n_tiles),
        in_specs=[
            tile(D_MODEL),
            pl.BlockSpec((None, HALO, D_MODEL), lambda b, j: (b, jnp.maximum(j * hb - 1, 0), 0)),
            pl.BlockSpec((None, HALO, D_MODEL), lambda b, j: (b, jnp.minimum((j + 1) * hb, n_halo - 1), 0)),
            *[full(a) for a in consts],
        ],
        out_specs=[
            tile(HG_SCAN_COLS),
            tile(HG_WIDTH),
            pl.BlockSpec((N_DIR, None, ROW_TILE, RW_OPS * RW_WIDTH), lambda b, j: (0, b, j, 0)),
            pl.BlockSpec((N_DIR, None, cpt, RW_WIDTH // LANES, 2 * CHUNK, LANES), lambda b, j: (0, b, j, 0, 0, 0)),
            tile(RW_WIDTH),
            pl.BlockSpec((N_DIR, None, cpt, HALO, RW_WIDTH), lambda b, j: (0, b, j, 0, 0)),
            tile(RW_WIDTH),
            tile(RW_WIDTH),
        ],
        out_shape=[
            jax.ShapeDtypeStruct((bsz, seq, HG_SCAN_COLS), jnp.float32),
            jax.ShapeDtypeStruct((bsz, seq, HG_WIDTH), BF16),
            jax.ShapeDtypeStruct((N_DIR, bsz, seq, RW_OPS * RW_WIDTH), BF16),
            jax.ShapeDtypeStruct((N_DIR, bsz, n_chunks, RW_WIDTH // LANES, 2 * CHUNK, LANES), BF16),
            jax.ShapeDtypeStruct((bsz, seq, RW_WIDTH), BF16),
            jax.ShapeDtypeStruct((N_DIR, bsz, n_chunks, HALO, RW_WIDTH), jnp.float32),
            jax.ShapeDtypeStruct((bsz, seq, RW_WIDTH), BF16),
            jax.ShapeDtypeStruct((bsz, seq, RW_WIDTH), BF16),
        ],
        compiler_params=pltpu.CompilerParams(
            dimension_semantics=("arbitrary", "arbitrary"), vmem_limit_bytes=VMEM_LIMIT),
        name="in_proj_rwkv7_prep",
    )(x, x, x, *consts)


def _rw_scan_kernel(of_ref, ob_ref, tf_ref, tb_ref, vf_ref, vb_ref, gf_ref, gb_ref, yf_ref, yb_ref, s_ref):
    c = pl.program_id(1)

    @pl.when(c == 0)
    def _():
        s_ref[...] = jnp.zeros_like(s_ref)

    N, W = RW_HEAD_DIM, RW_WIDTH
    lane = lax.broadcasted_iota(jnp.int32, (CHUNK, LANES), 1)
    tt = lax.broadcasted_iota(jnp.int32, (CHUNK, LANES), 0)
    half = (lane < N, lane >= N)
    diag = ((lax.broadcasted_iota(jnp.int32, (LANES, LANES), 0) < N)
            == (lax.broadcasted_iota(jnp.int32, (LANES, LANES), 1) < N))
    dir_refs = ((of_ref, vf_ref, gf_ref, yf_ref, tf_ref), (ob_ref, vb_ref, gb_ref, yb_ref, tb_ref))

    pairs = [(q, d, j) for q in range(SCAN_BATCH) for d in range(N_DIR) for j in range(W // LANES)]
    probs = [(pi, par) for pi in range(len(pairs)) for par in range(2)]

    def op(pi, i):
        q, d, j = pairs[pi]
        return dir_refs[d][0][q, :, i * W + j * LANES:i * W + (j + 1) * LANES]

    r_p, a_p, ke_p, be_p = [_PerPair(functools.partial(op, i=i)) for i in range(RW_OPS)]
    bkt_p = _PerPair(lambda pi: dir_refs[pairs[pi][1]][4][pairs[pi][0], pairs[pi][2]])
    v_p = _PerPair(lambda pi: dir_refs[pairs[pi][1]][1][pairs[pi][0], :,
                                                        pairs[pi][2] * LANES:(pairs[pi][2] + 1) * LANES])
    s_p = _PerPair(lambda pi: s_ref[pairs[pi]])

    def lag(pi):
        d = pairs[pi][1]
        delta = tt - (lane & (N - 1))
        return delta if d == 0 else -delta

    gram = []
    for pi, par in probs:
        lhs = jnp.concatenate([a_p[pi], r_p[pi]], axis=0)
        lhs = jnp.where(jnp.concatenate([half[par], half[par]], axis=0), lhs, jnp.zeros_like(lhs))
        gram.append(jnp.dot(lhs, bkt_p[pi], preferred_element_type=jnp.float32))

    za, zb, bot = [], [], []
    for n, (pi, par) in enumerate(probs):
        top = gram[n][:CHUNK]
        lg = lag(pi)
        m0 = jnp.where(jnp.where(half[0], lg, 0) > 0, -top, 0.0)
        l_ak = jnp.where(jnp.where(half[1], lg, 0) > 0, top, 0.0)
        bot.append(jnp.where(lg >= 0, gram[n][CHUNK:], 0.0).astype(BF16))
        vv = jnp.concatenate([v_p[pi], v_p[pi]], axis=0)
        zb.append(jnp.dot(l_ak.astype(BF16), vv, preferred_element_type=jnp.float32))
        a_own = a_p[pi].astype(jnp.float32)
        za.append(jnp.where(half[1], a_own if par == 1 else pltpu.roll(a_own, N, 1), m0))

    for level in range(6):
        skip = (2 ** level) // LHS_ROWS * LHS_ROWS
        new_za, new_zb = [], []
        for n, (pi, par) in enumerate(probs):
            live = slice(skip, CHUNK) if pairs[pi][1] == 0 else slice(0, CHUNK - skip)
            m = za[n][live, :N].astype(BF16)
            z2 = jnp.concatenate([za[n], zb[n]], axis=1).astype(BF16)
            p = jnp.dot(m, z2, preferred_element_type=jnp.float32)
            keep_a = jnp.where(half[1], za[n], 0.0)
            pieces_a = [keep_a[:live.start], p[:, :LANES] + keep_a[live], keep_a[live.stop:]]
            pieces_b = [zb[n][:live.start], p[:, LANES:] + zb[n][live], zb[n][live.stop:]]
            new_za.append(jnp.concatenate([x for x in pieces_a if x.shape[0]], axis=0))
            new_zb.append(jnp.concatenate([x for x in pieces_b if x.shape[0]], axis=0))
        za, zb = new_za, new_zb

    n_pairs = range(len(pairs))
    s_bf = [s_p[pi].astype(BF16) for pi in n_pairs]
    wr = [jnp.concatenate([jnp.where(half[0], pltpu.roll(za[2 * pi], N, 1), za[2 * pi + 1]).astype(BF16),
                           r_p[pi]], axis=0) for pi in n_pairs]
    ws = [lax.dot_general(wr[pi], s_bf[pi], _NT, preferred_element_type=jnp.float32) for pi in n_pairs]
    u_bf = [(-(ws[pi][:CHUNK] + jnp.where(half[0], zb[2 * pi], zb[2 * pi + 1]))).astype(BF16)
            for pi in n_pairs]
    uv = [jnp.concatenate([u_bf[pi], v_p[pi]], axis=0) for pi in n_pairs]
    y_eo = [jnp.dot(jnp.concatenate([bot[2 * pi], bot[2 * pi + 1]], axis=0), uv[pi],
                    preferred_element_type=jnp.float32) for pi in n_pairs]
    upd = [lax.dot_general(uv[pi], jnp.concatenate([be_p[pi], ke_p[pi]], axis=0), _TN,
                           preferred_element_type=jnp.float32) for pi in n_pairs]
    for pi, (q, d, j) in enumerate(pairs):
        y = ws[pi][CHUNK:] + jnp.where(half[0], y_eo[pi][:CHUNK], y_eo[pi][CHUNK:])
        dir_refs[d][3][q, :, j * LANES:(j + 1) * LANES] = y.astype(BF16)
        g_end = dir_refs[d][2][q, 0:1, j * LANES:(j + 1) * LANES]
        s_ref[q, d, j] = s_p[pi] * g_end + jnp.where(diag, upd[pi], 0.0)


def _rw_scan(ops, bkt, v_bf, gend):
    _, bsz, seq, _ = ops.shape
    n_chunks = seq // CHUNK
    rev = lambda c: n_chunks - 1 - c
    return pl.pallas_call(
        _rw_scan_kernel,
        grid=(bsz // SCAN_BATCH, n_chunks),
        in_specs=[
            pl.BlockSpec((None, SCAN_BATCH, CHUNK, RW_OPS * RW_WIDTH), lambda b, c: (0, b, c, 0)),
            pl.BlockSpec((None, SCAN_BATCH, CHUNK, RW_OPS * RW_WIDTH), lambda b, c: (1, b, rev(c), 0)),
            pl.BlockSpec((None, SCAN_BATCH, None, RW_WIDTH // LANES, 2 * CHUNK, LANES),
                         lambda b, c: (0, b, c, 0, 0, 0)),
            pl.BlockSpec((None, SCAN_BATCH, None, RW_WIDTH // LANES, 2 * CHUNK, LANES),
                         lambda b, c: (1, b, rev(c), 0, 0, 0)),
            pl.BlockSpec((SCAN_BATCH, CHUNK, RW_WIDTH), lambda b, c: (b, c, 0)),
            pl.BlockSpec((SCAN_BATCH, CHUNK, RW_WIDTH), lambda b, c: (b, rev(c), 0)),
            pl.BlockSpec((None, SCAN_BATCH, None, HALO, RW_WIDTH), lambda b, c: (0, b, c, 0, 0)),
            pl.BlockSpec((None, SCAN_BATCH, None, HALO, RW_WIDTH), lambda b, c: (1, b, rev(c), 0, 0)),
        ],
        out_specs=[
            pl.BlockSpec((SCAN_BATCH, CHUNK, RW_WIDTH), lambda b, c: (b, c, 0)),
            pl.BlockSpec((SCAN_BATCH, CHUNK, RW_WIDTH), lambda b, c: (b, rev(c), 0)),
        ],
        out_shape=[jax.ShapeDtypeStruct((bsz, seq, RW_WIDTH), BF16)] * 2,
        scratch_shapes=[pltpu.VMEM((SCAN_BATCH, N_DIR, RW_WIDTH // LANES, LANES, LANES), jnp.float32)],
        compiler_params=pltpu.CompilerParams(
            dimension_semantics=("arbitrary", "arbitrary"), vmem_limit_bytes=VMEM_LIMIT),
        name="rwkv7_scan",
    )(ops, ops, bkt, bkt, v_bf, v_bf, gend, gend)


def _hg_kernel(qf_ref, ff_ref, if_ref, qb_ref, fb_ref, ib_ref, lbl_ref, tri_ref,
               of_ref, ob_ref, s_ref, cs_ref, vs_ref, *, layer):
    c = pl.program_id(1)

    @pl.when(c == 0)
    def _():
        s_ref[...] = jnp.zeros_like(s_ref)

    lg = lbl_ref[...]
    e = jnp.exp(lg - jnp.max(lg, axis=0, keepdims=True))
    lb = jnp.sum(e[0:layer + 1, :], axis=0, keepdims=True) / jnp.sum(e, axis=0, keepdims=True)

    heads = range(HG_HEADS)
    sls = [slice(h * HG_DK, (h + 1) * HG_DK) for h in heads]
    trow = lax.broadcasted_iota(jnp.int32, (SUB, HG_DK), 0)
    dir_refs = ((qf_ref, ff_ref, if_ref, of_ref), (qb_ref, fb_ref, ib_ref, ob_ref))
    probs = [(n, d) for n in range(SCAN_BATCH) for d in range(N_DIR)]

    def fast_path(k, out):
        n, d = probs[k]
        q_ref, f_ref, i_ref, o_ref = dir_refs[d]
        q, v = q_ref[n], i_ref[n]
        v_bf = v.astype(BF16)
        f = lb + (1.0 - lb) * _sigmoid(f_ref[n])
        kf = 1.0 - f
        yield
        b2 = _select_rows(tri_ref[d], jnp.log2(f), 2)
        yield
        (total,) = _chunk_totals(b2, d)
        q_in = (q * jnp.exp2(b2)).astype(BF16)
        k_end = (kf * jnp.exp2(total - b2)).astype(BF16)
        g_end = jnp.exp2(total)
        st = [s_ref[n, d, h] for h in heads]
        inter = [lax.dot_general(q_in[:, sls[h]], st[h].astype(BF16), _NT, preferred_element_type=jnp.float32)
                 for h in heads]
        for h in heads:
            s_ref[n, d, h] = st[h] * g_end[:, sls[h]] + lax.dot_general(
                v_bf[:, sls[h]], k_end[:, sls[h]], _TN, preferred_element_type=jnp.float32)
        yield
        mid = CHUNK // 2 - 1 if d == 0 else CHUNK // 2
        dev = b2 - b2[mid:mid + 1]
        q_c = (q * jnp.exp2(dev)).astype(BF16)
        k_c = (kf * jnp.exp2(-dev)).astype(BF16)
        scores = [lax.dot_general(q_c[:, sls[h]], k_c[:, sls[h]], _NT, preferred_element_type=jnp.float32)
                  for h in heads]
        out[k] = (q, kf, b2, v, v_bf, inter, jnp.max(jnp.abs(dev), axis=0, keepdims=True))
        yield
        tt = lax.broadcasted_iota(jnp.int32, (CHUNK, CHUNK), 0)
        ss = lax.broadcasted_iota(jnp.int32, (CHUNK, CHUNK), 1)
        seen = (tt >= ss) if d == 0 else (tt <= ss)
        for h in heads:
            o_ref[n, :, sls[h]] = inter[h] + jnp.dot(jnp.where(seen, scores[h], 0.0).astype(BF16),
                                                      v_bf[:, sls[h]], preferred_element_type=jnp.float32)

    def safe_path(k, q, kf, b2, v, v_bf, inter):
        n, d = probs[k]
        o_ref = dir_refs[d][3]
        cs = b2 - jnp.log2(kf)
        for h in heads:
            cs_ref[h] = cs[:, sls[h]]
            vs_ref[h] = v[:, sls[h]]
        for lo in range(0, CHUNK, SUB):
            q_i, b_i = q[lo:lo + SUB], b2[lo:lo + SUB]
            acc = [inter[h][lo:lo + SUB] for h in heads]
            for s in range(SUB):
                row_s = pl.ds(lo + s, SUB, stride=0)
                m = (trow >= s) if d == 0 else (trow <= s)
                for h in heads:
                    xs = jnp.where(m, q_i[:, sls[h]] * jnp.exp2(b_i[:, sls[h]] - cs_ref[h, row_s, :]), 0.0)
                    w = jnp.sum(xs, axis=-1, keepdims=True)
                    acc[h] = acc[h] + w * vs_ref[h, row_s, :]
            for h in heads:
                o_ref[n, lo:lo + SUB, sls[h]] = acc[h]

        def rows(p_lo, p_hi):
            lo = p_lo if d == 0 else CHUNK - p_hi
            return slice(lo, lo + (p_hi - p_lo))

        size = CHUNK // 2
        while size >= SUB:
            for p in range(0, CHUNK, 2 * size):
                s_sl, t_sl = rows(p, p + size), rows(p + size, p + 2 * size)
                ref = b2[rows(p + size - 1, p + size)]
                q_t = (q[t_sl] * jnp.exp2(b2[t_sl] - ref)).astype(BF16)
                k_s = (kf[s_sl] * jnp.exp2(ref - b2[s_sl])).astype(BF16)
                for h in heads:
                    sc = lax.dot_general(q_t[:, sls[h]], k_s[:, sls[h]], _NT, preferred_element_type=jnp.float32)
                    o_ref[n, t_sl, sls[h]] += jnp.dot(sc.astype(BF16), v_bf[s_sl, sls[h]],
                                                      preferred_element_type=jnp.float32)
            size //= 2

    ctxs = [None] * len(probs)
    for _ in itertools.zip_longest(*[fast_path(k, ctxs) for k in range(len(probs))]):
        pass
    worst = functools.reduce(jnp.maximum, [ctx[-1] for ctx in ctxs])

    @pl.when(jnp.max(worst) > SAFE_LOG2_DROP)
    def _():
        for k, ctx in enumerate(ctxs):
            safe_path(k, *ctx[:-1])


def _hg_mix(p_hg, lb_logits, layer, tri):
    bsz, seq, _ = p_hg.shape
    n_chunks = seq // CHUNK
    rev = lambda c: n_chunks - 1 - c
    blk = (SCAN_BATCH, CHUNK, HG_WIDTH)
    fwd = lambda j: pl.BlockSpec(blk, lambda g, c: (g, c, j))
    bwd = lambda j: pl.BlockSpec(blk, lambda g, c: (g, rev(c), j))
    full = lambda a: pl.BlockSpec(a.shape, lambda g, c: (0,) * a.ndim)
    return pl.pallas_call(
        functools.partial(_hg_kernel, layer=layer),
        grid=(bsz // SCAN_BATCH, n_chunks),
        in_specs=[fwd(0), fwd(1), fwd(3), bwd(0), bwd(2), bwd(3), full(lb_logits), full(tri)],
        out_specs=[fwd(0), bwd(0)],
        out_shape=[jax.ShapeDtypeStruct((bsz, seq, HG_WIDTH), jnp.float32)] * 2,
        scratch_shapes=[pltpu.VMEM((SCAN_BATCH, N_DIR, HG_HEADS, HG_DK, HG_DK), jnp.float32),
                        pltpu.VMEM((HG_HEADS, CHUNK, HG_DK), jnp.float32),
                        pltpu.VMEM((HG_HEADS, CHUNK, HG_DK), jnp.float32)],
        compiler_params=pltpu.CompilerParams(
            dimension_semantics=("arbitrary", "arbitrary"), vmem_limit_bytes=VMEM_LIMIT),
        name="hgrn2_mix",
    )(p_hg, p_hg, p_hg, p_hg, p_hg, p_hg, lb_logits, tri)


def _out_kernel(x_ref, yf_ref, yb_ref, bonus_ref, gate_ref, of_ref, ob_ref, hgate_ref,
                lnw_ref, lnb_ref, hgn_ref, seg_ref, w_ref, post_ref, out_ref):
    y = yf_ref[...].astype(jnp.float32) + yb_ref[...].astype(jnp.float32)
    seg = seg_ref[...]
    mu = _head_sums(y, seg) * (1.0 / RW_HEAD_DIM)
    yc = y - mu
    var = _head_sums(yc * yc, seg) * (1.0 / RW_HEAD_DIM)
    y = yc * lax.rsqrt(var + GN_EPS) * lnw_ref[...] + lnb_ref[...]
    rw = (y + bonus_ref[...].astype(jnp.float32)) * gate_ref[...].astype(jnp.float32)

    o = of_ref[...] + ob_ref[...]
    parts = []
    for h in range(HG_HEADS):
        oh = o[:, h * HG_DK:(h + 1) * HG_DK]
        parts.append(oh * lax.rsqrt(jnp.mean(oh * oh, axis=-1, keepdims=True) + NORM_EPS))
    hg = jnp.concatenate(parts, axis=1) * hgn_ref[...] * hgate_ref[...].astype(jnp.float32)

    cat = jnp.concatenate([rw, hg], axis=1).astype(BF16)
    yo = jnp.dot(cat, w_ref[...], preferred_element_type=jnp.float32)
    ms = jnp.mean(yo * yo, axis=-1, keepdims=True)
    out_ref[...] = x_ref[...] + yo * lax.rsqrt(ms + NORM_EPS) * post_ref[...]


def _out_proj(x2, y_f, y_b, bonus, gate, o_f, o_b, hg_gate, ln_w, ln_b, hg_norm, seg, w_bf16, post_g):
    rows = x2.shape[0]
    tile = lambda n: pl.BlockSpec((OUT_TILE, n), lambda i: (i, 0))
    row_spec = lambda n: pl.BlockSpec((1, n), lambda i: (0, 0))
    return pl.pallas_call(
        _out_kernel,
        grid=(rows // OUT_TILE,),
        in_specs=[
            tile(D_MODEL),
            tile(RW_WIDTH), tile(RW_WIDTH), tile(RW_WIDTH), tile(RW_WIDTH),
            tile(HG_WIDTH), tile(HG_WIDTH), tile(HG_WIDTH),
            row_spec(RW_WIDTH), row_spec(RW_WIDTH), row_spec(HG_WIDTH),
            pl.BlockSpec((MXU_TILE, MXU_TILE), lambda i: (0, 0)),
            pl.BlockSpec((D_MODEL, D_MODEL), lambda i: (0, 0)),
            row_spec(D_MODEL),
        ],
        out_specs=tile(D_MODEL),
        out_shape=jax.ShapeDtypeStruct((rows, D_MODEL), jnp.float32),
        compiler_params=pltpu.CompilerParams(
            dimension_semantics=("arbitrary",), vmem_limit_bytes=VMEM_LIMIT),
        name="out_proj",
    )(x2, y_f, y_b, bonus, gate, o_f, o_b, hg_gate, ln_w, ln_b, hg_norm, seg, w_bf16, post_g)


def _lora_weights(w):
    z = jnp.zeros_like(w[0])
    wp = jnp.stack([jnp.concatenate([w[0], z], axis=0), jnp.concatenate([z, w[1]], axis=0)])
    return wp.astype(BF16)


def _scan_order_selectors(tile):
    t = jnp.arange(tile)
    same = (t[:, None] // CHUNK) == (t[None, :] // CHUNK)
    fwd = same & (t[None, :] <= t[:, None])
    bwd = same & (t[None, :] >= t[:, None])
    return jnp.stack([fwd, bwd]).astype(BF16)


def kernel(x, pre_norm_g, w_in, rw_shift_prev, rw_shift_next, rw_w0, rw_w2, rw_a0, rw_a2, rw_k_k, rw_k_a,
           rw_r_k, rw_ln_w, rw_ln_b, hg_lb_logits, hg_norm_g, w_out, post_norm_g):
    bsz, seq, dm = x.shape
    depth = w_in.shape[0]
    rows = bsz * seq
    lane = jnp.arange(MXU_TILE) // RW_HEAD_DIM
    seg = (lane[:, None] == lane[None, :]).astype(BF16)
    tri = _scan_order_selectors(ROW_TILE)
    hg_tri = _scan_order_selectors(CHUNK)
    row = lambda t: t.reshape(1, -1)
    flat = lambda t: t.reshape(rows, t.shape[-1])
    for l in range(depth):
        x2 = x.reshape(rows, dm)
        p_hg, hg_gate, ops, bkt, v_bf, gend, bonus, gate = _in_rw(
            x, row(pre_norm_g[l]), w_in[l].astype(BF16),
            row(rw_shift_prev[l]), row(rw_shift_next[l]),
            rw_w0[l], _lora_weights(rw_w2[l]), rw_a0[l], _lora_weights(rw_a2[l]),
            row(rw_k_k[l]), row(rw_k_a[l]), row(rw_r_k[l]), seg, tri)
        y_f, y_b = _rw_scan(ops, bkt, v_bf, gend)
        o_f, o_b = _hg_mix(p_hg, hg_lb_logits, l, hg_tri)
        out = _out_proj(
            x2, flat(y_f), flat(y_b), flat(bonus), flat(gate), flat(o_f), flat(o_b), flat(hg_gate),
            row(rw_ln_w[l]), row(rw_ln_b[l]), row(jnp.tile(hg_norm_g[l], HG_HEADS)), seg,
            w_out[l].astype(BF16), row(post_norm_g[l]))
        x = out.reshape(bsz, seq, dm)
    return x
```

```python
import functools
import itertools

import jax
import jax.numpy as jnp
from jax import lax
from jax.experimental import pallas as pl
from jax.experimental.pallas import tpu as pltpu

D_MODEL = 1024
RW_HEAD_DIM = 64
RW_WIDTH = 512
RW_HEADS = 8
HG_DK = 128
HG_WIDTH = 512
HG_HEADS = 4
LORA = 64
NORM_EPS = 1e-6
GN_EPS = 64e-5
RW_COLS = 4 * RW_WIDTH + 4 * LORA
HG_COLS = 5 * HG_WIDTH
HG_SCAN_COLS = 4 * HG_WIDTH
IN_COLS = RW_COLS + HG_COLS
N_DIR = 2
RW_OPS = 4

LANES = 128
LHS_ROWS = 16
MXU_TILE = 256
CHUNK = 64
SUB = 8
SAFE_LOG2_DROP = 100.0
ROW_TILE = 256
OUT_TILE = 512
HALO = 8
SCAN_BATCH = 4
VMEM_LIMIT = 56 * 1024 * 1024

_NN = (((1,), (0,)), ((), ()))
_NT = (((1,), (1,)), ((), ()))
_TN = (((0,), (0,)), ((), ()))
BF16 = jnp.bfloat16
LOG2E = 1.4426950408889634


def _split_bf16(x, terms=2):
    pieces = []
    for _ in range(terms - 1):
        hi = x.astype(BF16)
        pieces.append(hi)
        x = x - hi.astype(jnp.float32)
    pieces.append(x.astype(BF16))
    return pieces


def _select_rows(sel, x, terms):
    acc = None
    for piece in _split_bf16(x, terms):
        t = jnp.dot(sel, piece, preferred_element_type=jnp.float32)
        acc = t if acc is None else acc + t
    return acc


def _head_sums(x, seg):
    xb = x.astype(BF16)
    return jnp.concatenate([jnp.dot(xb[:, c:c + MXU_TILE], seg, preferred_element_type=jnp.float32)
                            for c in range(0, x.shape[1], MXU_TILE)], axis=1)


def _sigmoid(x):
    return 1.0 / (1.0 + jnp.exp(-x))


class _PerPair:
    def __init__(self, load):
        self._load = load

    def __getitem__(self, pi):
        return self._load(pi)


def _chunk_totals(cum, d):
    last = CHUNK - 1 if d == 0 else 0
    return [cum[c * CHUNK + last:c * CHUNK + last + 1] for c in range(cum.shape[0] // CHUNK)]


def _expand_chunks(rows):
    return jnp.concatenate([jnp.broadcast_to(r, (CHUNK, r.shape[1])) for r in rows], axis=0)


def _in_rw_kernel(xc_ref, xp_ref, xn_ref, g_ref, w_ref,
                  mup_ref, mun_ref, w0_ref, w2_ref, a0_ref, a2_ref,
                  kk_ref, ka_ref, rk_ref, seg_ref, tri_ref,
                  phg_ref, hgate_ref, ops_ref, bkt_ref, v_ref, gend_ref, bonus_ref, gate_ref, *, n_tiles):
    j = pl.program_id(1)

    def pre_norm(x):
        return x * lax.rsqrt(jnp.mean(x * x, axis=-1, keepdims=True) + NORM_EPS) * g_ref[...]

    h_c = pre_norm(xc_ref[...])
    h_p = pre_norm(xp_ref[...]) * (j > 0).astype(jnp.float32)
    h_n = pre_norm(xn_ref[...]) * (j < n_tiles - 1).astype(jnp.float32)
    h = h_c.astype(BF16)
    h_ext = jnp.concatenate([h_p, h_c, h_n], axis=0).astype(BF16)

    hg_blocks = iter(range(HG_COLS // HG_WIDTH))

    def hg_block():
        i = next(hg_blocks, None)
        if i is not None:
            cols = slice(i * HG_WIDTH, (i + 1) * HG_WIDTH)
            blk = jnp.dot(h, w_ref[:, RW_COLS + cols.start:RW_COLS + cols.stop], preferred_element_type=jnp.float32)
            if i < HG_SCAN_COLS // HG_WIDTH:
                phg_ref[:, cols] = blk
            else:
                hgate_ref[...] = (blk * _sigmoid(blk)).astype(BF16)

    W = RW_WIDTH
    ext = ROW_TILE + 2 * HALO
    tile = slice(HALO, HALO + ROW_TILE)

    def shifted(cols):
        p_ext = jnp.dot(h_ext, w_ref[:, cols], preferred_element_type=jnp.float32)
        p, p_prev, p_next = p_ext[tile], pltpu.roll(p_ext, 1, 0)[tile], pltpu.roll(p_ext, ext - 1, 0)[tile]
        mup, mun = mup_ref[:, cols], mun_ref[:, cols]
        return p * (1.0 - mup - mun) + mup * p_prev + mun * p_next

    r, k, v, g = [shifted(slice(i * W, (i + 1) * W)) for i in range(4)]
    lora_in = shifted(slice(4 * W, RW_COLS))
    wd = lora_in[:, :2 * LORA]
    ad = lora_in[:, 2 * LORA:].astype(BF16)
    hg_block()
    gate_ref[...] = (g * _sigmoid(g)).astype(BF16)
    v_ref[...] = v.astype(BF16)

    seg = seg_ref[...]
    kk = k * kk_ref[...]
    kk = kk * lax.rsqrt(jnp.maximum(_head_sums(kk * kk, seg), 1e-24))
    tanh_wd = jnp.tanh(wd).astype(BF16)
    hg_block()

    k_sum = None
    for d in range(N_DIR):
        w_raw = w0_ref[d:d + 1, :] + jnp.dot(tanh_wd, w2_ref[d], preferred_element_type=jnp.float32)
        lw = (-LOG2E * jnp.exp(jnp.float32(-0.5))) * _sigmoid(w_raw)
        a = _sigmoid(a0_ref[d:d + 1, :] + jnp.dot(ad, a2_ref[d], preferred_element_type=jnp.float32))
        k_d = k * (1.0 + (a - 1.0) * ka_ref[...])
        b_d = kk * a
        k_sum = k_d if k_sum is None else k_sum + k_d
        cl = _select_rows(tri_ref[d], lw, 2)
        g_end = [jnp.exp2(tot) for tot in _chunk_totals(cl, d)]
        hg_block()
        e_out = jnp.exp2(-cl)
        e_end = e_out * _expand_chunks(g_end)
        ops = (r * jnp.exp2(cl),
               kk * jnp.exp2(cl - lw),
               k_d * e_end, b_d * e_end)
        for i, op in enumerate(ops):
            ops_ref[d, :, i * W:(i + 1) * W] = op.astype(BF16)
        b_t, k_t = b_d * e_out, k_d * e_out
        for q in range(ROW_TILE // CHUNK):
            rows = slice(q * CHUNK, (q + 1) * CHUNK)
            for t in range(W // LANES):
                cols = slice(t * LANES, (t + 1) * LANES)
                bk = jnp.concatenate([b_t[rows, cols], k_t[rows, cols]], axis=0)
                bkt_ref[d, q, t] = bk.astype(BF16).T
        for q, g_q in enumerate(g_end):
            gend_ref[d, q] = jnp.broadcast_to(g_q, (HALO, W))
        hg_block()

    bonus_ref[...] = (_head_sums(r * k_sum * rk_ref[...], seg) * v).astype(BF16)
    for _ in hg_blocks:
        raise AssertionError("HGRN2 column blocks left unissued")


def _in_rw(x, g, w_in, mup, mun, w0, w2, a0, a2, k_k, k_a, r_k, seg, tri):
    bsz, seq, _ = x.shape
    n_tiles = seq // ROW_TILE
    hb = ROW_TILE // HALO
    n_halo = seq // HALO
    n_chunks = seq // CHUNK
    cpt = ROW_TILE // CHUNK
    full = lambda a: pl.BlockSpec(a.shape, lambda b, j: (0,) * a.ndim)
    tile = lambda n: pl.BlockSpec((None, ROW_TILE, n), lambda b, j: (b, j, 0))
    consts = (g, w_in, mup, mun, w0, w2, a0, a2, k_k, k_a, r_k, seg, tri)
    return pl.pallas_call(
        functools.partial(_in_rw_kernel, n_tiles=n_tiles),
        grid=(bsz, n_tiles),
        in_specs=[
            tile(D_MODEL),
            pl.BlockSpec((None, HALO, D_MODEL), lambda b, j: (b, jnp.maximum(j * hb - 1, 0), 0)),
            pl.BlockSpec((None, HALO, D_MODEL), lambda b, j: (b, jnp.minimum((j + 1) * hb, n_halo - 1), 0)),
            *[full(a) for a in consts],
        ],
        out_specs=[
            tile(HG_SCAN_COLS),
            tile(HG_WIDTH),
            pl.BlockSpec((N_DIR, None, ROW_TILE, RW_OPS * RW_WIDTH), lambda b, j: (0, b, j, 0)),
            pl.BlockSpec((N_DIR, None, cpt, RW_WIDTH // LANES, 2 * CHUNK, LANES), lambda b, j: (0, b, j, 0, 0, 0)),
            tile(RW_WIDTH),
            pl.BlockSpec((N_DIR, None, cpt, HALO, RW_WIDTH), lambda b, j: (0, b, j, 0, 0)),
            tile(RW_WIDTH),
            tile(RW_WIDTH),
        ],
        out_shape=[
            jax.ShapeDtypeStruct((bsz, seq, HG_SCAN_COLS), jnp.float32),
            jax.ShapeDtypeStruct((bsz, seq, HG_WIDTH), BF16),
            jax.ShapeDtypeStruct((N_DIR, bsz, seq, RW_OPS * RW_WIDTH), BF16),
            jax.ShapeDtypeStruct((N_DIR, bsz, n_chunks, RW_WIDTH // LANES, 2 * CHUNK, LANES), BF16),
            jax.ShapeDtypeStruct((bsz, seq, RW_WIDTH), BF16),
            jax.ShapeDtypeStruct((N_DIR, bsz, n_chunks, HALO, RW_WIDTH), jnp.float32),
            jax.ShapeDtypeStruct((bsz, seq, RW_WIDTH), BF16),
            jax.ShapeDtypeStruct((bsz, seq, RW_WIDTH), BF16),
        ],
        compiler_params=pltpu.CompilerParams(
            dimension_semantics=("arbitrary", "arbitrary"), vmem_limit_bytes=VMEM_LIMIT),
        name="in_proj_rwkv7_prep",
    )(x, x, x, *consts)


def _rw_scan_kernel(of_ref, ob_ref, tf_ref, tb_ref, vf_ref, vb_ref, gf_ref, gb_ref, yf_ref, yb_ref, s_ref):
    c = pl.program_id(1)

    @pl.when(c == 0)
    def _():
        s_ref[...] = jnp.zeros_like(s_ref)

    N, W = RW_HEAD_DIM, RW_WIDTH
    lane = lax.broadcasted_iota(jnp.int32, (CHUNK, LANES), 1)
    tt = lax.broadcasted_iota(jnp.int32, (CHUNK, LANES), 0)
    half = (lane < N, lane >= N)
    diag = ((lax.broadcasted_iota(jnp.int32, (LANES, LANES), 0) < N)
            == (lax.broadcasted_iota(jnp.int32, (LANES, LANES), 1) < N))
    dir_refs = ((of_ref, vf_ref, gf_ref, yf_ref, tf_ref), (ob_ref, vb_ref, gb_ref, yb_ref, tb_ref))

    pairs = [(q, d, j) for q in range(SCAN_BATCH) for d in range(N_DIR) for j in range(W // LANES)]
    probs = [(pi, par) for pi in range(len(pairs)) for par in range(2)]

    def op(pi, i):
        q, d, j = pairs[pi]
        return dir_refs[d][0][q, :, i * W + j * LANES:i * W + (j + 1) * LANES]

    r_p, a_p, ke_p, be_p = [_PerPair(functools.partial(op, i=i)) for i in range(RW_OPS)]
    bkt_p = _PerPair(lambda pi: dir_refs[pairs[pi][1]][4][pairs[pi][0], pairs[pi][2]])
    v_p = _PerPair(lambda pi: dir_refs[pairs[pi][1]][1][pairs[pi][0], :,
                                                        pairs[pi][2] * LANES:(pairs[pi][2] + 1) * LANES])
    s_p = _PerPair(lambda pi: s_ref[pairs[pi]])

    def lag(pi):
        d = pairs[pi][1]
        delta = tt - (lane & (N - 1))
        return delta if d == 0 else -delta

    gram = []
    for pi, par in probs:
        lhs = jnp.concatenate([a_p[pi], r_p[pi]], axis=0)
        lhs = jnp.where(jnp.concatenate([half[par], half[par]], axis=0), lhs, jnp.zeros_like(lhs))
        gram.append(jnp.dot(lhs, bkt_p[pi], preferred_element_type=jnp.float32))

    za, l_ak, bot = [], [], []
    for n, (pi, par) in enumerate(probs):
        top = gram[n][:CHUNK]
        lg = lag(pi)
        m0 = jnp.where(jnp.where(half[0], lg, 0) > 0, -top, 0.0)
        l_ak.append(jnp.where(jnp.where(half[1], lg, 0) > 0, top, 0.0).astype(BF16))
        bot.append(jnp.where(lg >= 0, gram[n][CHUNK:], 0.0).astype(BF16))
        a_own = a_p[pi].astype(jnp.float32)
        za.append(jnp.where(half[1], a_own if par == 1 else pltpu.roll(a_own, N, 1), m0))
    zb = []
    for pi in range(len(pairs)):
        vv = jnp.concatenate([v_p[pi], v_p[pi]], axis=0)
        lv = jnp.dot(jnp.concatenate([l_ak[2 * pi], l_ak[2 * pi + 1]], axis=0), vv,
                     preferred_element_type=jnp.float32)
        zb += [lv[:CHUNK], lv[CHUNK:]]

    for level in range(6):
        skip = (2 ** level) // LHS_ROWS * LHS_ROWS
        new_za, new_zb = [], []
        for n, (pi, par) in enumerate(probs):
            live = slice(skip, CHUNK) if pairs[pi][1] == 0 else slice(0, CHUNK - skip)
            m = za[n][live, :N].astype(BF16)
            z2 = jnp.concatenate([za[n], zb[n]], axis=1).astype(BF16)
            p = jnp.dot(m, z2, preferred_element_type=jnp.float32)
            keep_a = jnp.where(half[1], za[n], 0.0)
            pieces_a = [keep_a[:live.start], p[:, :LANES] + keep_a[live], keep_a[live.stop:]]
            pieces_b = [zb[n][:live.start], p[:, LANES:] + zb[n][live], zb[n][live.stop:]]
            new_za.append(jnp.concatenate([x for x in pieces_a if x.shape[0]], axis=0))
            new_zb.append(jnp.concatenate([x for x in pieces_b if x.shape[0]], axis=0))
        za, zb = new_za, new_zb

    n_pairs = range(len(pairs))
    s_bf = [s_p[pi].astype(BF16) for pi in n_pairs]
    wr = [jnp.concatenate([jnp.where(half[0], pltpu.roll(za[2 * pi], N, 1), za[2 * pi + 1]).astype(BF16),
                           r_p[pi]], axis=0) for pi in n_pairs]
    ws = [lax.dot_general(wr[pi], s_bf[pi], _NT, preferred_element_type=jnp.float32) for pi in n_pairs]
    u_bf = [(-(ws[pi][:CHUNK] + jnp.where(half[0], zb[2 * pi], zb[2 * pi + 1]))).astype(BF16)
            for pi in n_pairs]
    uv = [jnp.concatenate([u_bf[pi], v_p[pi]], axis=0) for pi in n_pairs]
    y_eo = [jnp.dot(jnp.concatenate([bot[2 * pi], bot[2 * pi + 1]], axis=0), uv[pi],
                    preferred_element_type=jnp.float32) for pi in n_pairs]
    upd = [lax.dot_general(uv[pi], jnp.concatenate([be_p[pi], ke_p[pi]], axis=0), _TN,
                           preferred_element_type=jnp.float32) for pi in n_pairs]
    for pi, (q, d, j) in enumerate(pairs):
        y = ws[pi][CHUNK:] + jnp.where(half[0], y_eo[pi][:CHUNK], y_eo[pi][CHUNK:])
        dir_refs[d][3][q, :, j * LANES:(j + 1) * LANES] = y.astype(BF16)
        g_end = dir_refs[d][2][q, 0:1, j * LANES:(j + 1) * LANES]
        s_ref[q, d, j] = s_p[pi] * g_end + jnp.where(diag, upd[pi], 0.0)


def _rw_scan(ops, bkt, v_bf, gend):
    _, bsz, seq, _ = ops.shape
    n_chunks = seq // CHUNK
    rev = lambda c: n_chunks - 1 - c
    return pl.pallas_call(
        _rw_scan_kernel,
        grid=(bsz // SCAN_BATCH, n_chunks),
        in_specs=[
            pl.BlockSpec((None, SCAN_BATCH, CHUNK, RW_OPS * RW_WIDTH), lambda b, c: (0, b, c, 0)),
            pl.BlockSpec((None, SCAN_BATCH, CHUNK, RW_OPS * RW_WIDTH), lambda b, c: (1, b, rev(c), 0)),
            pl.BlockSpec((None, SCAN_BATCH, None, RW_WIDTH // LANES, 2 * CHUNK, LANES),
                         lambda b, c: (0, b, c, 0, 0, 0)),
            pl.BlockSpec((None, SCAN_BATCH, None, RW_WIDTH // LANES, 2 * CHUNK, LANES),
                         lambda b, c: (1, b, rev(c), 0, 0, 0)),
            pl.BlockSpec((SCAN_BATCH, CHUNK, RW_WIDTH), lambda b, c: (b, c, 0)),
            pl.BlockSpec((SCAN_BATCH, CHUNK, RW_WIDTH), lambda b, c: (b, rev(c), 0)),
            pl.BlockSpec((None, SCAN_BATCH, None, HALO, RW_WIDTH), lambda b, c: (0, b, c, 0, 0)),
            pl.BlockSpec((None, SCAN_BATCH, None, HALO, RW_WIDTH), lambda b, c: (1, b, rev(c), 0, 0)),
        ],
        out_specs=[
            pl.BlockSpec((SCAN_BATCH, CHUNK, RW_WIDTH), lambda b, c: (b, c, 0)),
            pl.BlockSpec((SCAN_BATCH, CHUNK, RW_WIDTH), lambda b, c: (b, rev(c), 0)),
        ],
        out_shape=[jax.ShapeDtypeStruct((bsz, seq, RW_WIDTH), BF16)] * 2,
        scratch_shapes=[pltpu.VMEM((SCAN_BATCH, N_DIR, RW_WIDTH // LANES, LANES, LANES), jnp.float32)],
        compiler_params=pltpu.CompilerParams(
            dimension_semantics=("arbitrary", "arbitrary"), vmem_limit_bytes=VMEM_LIMIT),
        name="rwkv7_scan",
    )(ops, ops, bkt, bkt, v_bf, v_bf, gend, gend)


def _hg_kernel(qf_ref, ff_ref, if_ref, qb_ref, fb_ref, ib_ref, lbl_ref, tri_ref,
               of_ref, ob_ref, s_ref, cs_ref, vs_ref, *, layer):
    c = pl.program_id(1)

    @pl.when(c == 0)
    def _():
        s_ref[...] = jnp.zeros_like(s_ref)

    lg = lbl_ref[...]
    e = jnp.exp(lg - jnp.max(lg, axis=0, keepdims=True))
    lb = jnp.sum(e[0:layer + 1, :], axis=0, keepdims=True) / jnp.sum(e, axis=0, keepdims=True)

    heads = range(HG_HEADS)
    sls = [slice(h * HG_DK, (h + 1) * HG_DK) for h in heads]
    trow = lax.broadcasted_iota(jnp.int32, (SUB, HG_DK), 0)
    dir_refs = ((qf_ref, ff_ref, if_ref, of_ref), (qb_ref, fb_ref, ib_ref, ob_ref))
    probs = [(n, d) for n in range(SCAN_BATCH) for d in range(N_DIR)]

    def fast_path(k, out):
        n, d = probs[k]
        q_ref, f_ref, i_ref, o_ref = dir_refs[d]
        q, v = q_ref[n], i_ref[n]
        v_bf = v.astype(BF16)
        f = lb + (1.0 - lb) * _sigmoid(f_ref[n])
        kf = 1.0 - f
        yield
        b2 = _select_rows(tri_ref[d], jnp.log2(f), 2)
        yield
        (total,) = _chunk_totals(b2, d)
        q_in = (q * jnp.exp2(b2)).astype(BF16)
        k_end = (kf * jnp.exp2(total - b2)).astype(BF16)
        g_end = jnp.exp2(total)
        st = [s_ref[n, d, h] for h in heads]
        inter = [lax.dot_general(q_in[:, sls[h]], st[h].astype(BF16), _NT, preferred_element_type=jnp.float32)
                 for h in heads]
        for h in heads:
            s_ref[n, d, h] = st[h] * g_end[:, sls[h]] + lax.dot_general(
                v_bf[:, sls[h]], k_end[:, sls[h]], _TN, preferred_element_type=jnp.float32)
        yield
        mid = CHUNK // 2 - 1 if d == 0 else CHUNK // 2
        dev = b2 - b2[mid:mid + 1]
        q_c = (q * jnp.exp2(dev)).astype(BF16)
        k_c = (kf * jnp.exp2(-dev)).astype(BF16)
        scores = [lax.dot_general(q_c[:, sls[h]], k_c[:, sls[h]], _NT, preferred_element_type=jnp.float32)
                  for h in heads]
        out[k] = (q, kf, b2, v, v_bf, inter, jnp.max(jnp.abs(dev), axis=0, keepdims=True))
        yield
        tt = lax.broadcasted_iota(jnp.int32, (CHUNK, CHUNK), 0)
        ss = lax.broadcasted_iota(jnp.int32, (CHUNK, CHUNK), 1)
        seen = (tt >= ss) if d == 0 else (tt <= ss)
        for h in heads:
            o_ref[n, :, sls[h]] = inter[h] + jnp.dot(jnp.where(seen, scores[h], 0.0).astype(BF16),
                                                      v_bf[:, sls[h]], preferred_element_type=jnp.float32)

    def safe_path(k, q, kf, b2, v, v_bf, inter):
        n, d = probs[k]
        o_ref = dir_refs[d][3]
        cs = b2 - jnp.log2(kf)
        for h in heads:
            cs_ref[h] = cs[:, sls[h]]
            vs_ref[h] = v[:, sls[h]]
        for lo in range(0, CHUNK, SUB):
            q_i, b_i = q[lo:lo + SUB], b2[lo:lo + SUB]
            acc = [inter[h][lo:lo + SUB] for h in heads]
            for s in range(SUB):
                row_s = pl.ds(lo + s, SUB, stride=0)
                m = (trow >= s) if d == 0 else (trow <= s)
                for h in heads:
                    xs = jnp.where(m, q_i[:, sls[h]] * jnp.exp2(b_i[:, sls[h]] - cs_ref[h, row_s, :]), 0.0)
                    w = jnp.sum(xs, axis=-1, keepdims=True)
                    acc[h] = acc[h] + w * vs_ref[h, row_s, :]
            for h in heads:
                o_ref[n, lo:lo + SUB, sls[h]] = acc[h]

        def rows(p_lo, p_hi):
            lo = p_lo if d == 0 else CHUNK - p_hi
            return slice(lo, lo + (p_hi - p_lo))

        size = CHUNK // 2
        while size >= SUB:
            for p in range(0, CHUNK, 2 * size):
                s_sl, t_sl = rows(p, p + size), rows(p + size, p + 2 * size)
                ref = b2[rows(p + size - 1, p + size)]
                q_t = (q[t_sl] * jnp.exp2(b2[t_sl] - ref)).astype(BF16)
                k_s = (kf[s_sl] * jnp.exp2(ref - b2[s_sl])).astype(BF16)
                for h in heads:
                    sc = lax.dot_general(q_t[:, sls[h]], k_s[:, sls[h]], _NT, preferred_element_type=jnp.float32)
                    o_ref[n, t_sl, sls[h]] += jnp.dot(sc.astype(BF16), v_bf[s_sl, sls[h]],
                                                      preferred_element_type=jnp.float32)
            size //= 2

    ctxs = [None] * len(probs)
    for _ in itertools.zip_longest(*[fast_path(k, ctxs) for k in range(len(probs))]):
        pass
    worst = functools.reduce(jnp.maximum, [ctx[-1] for ctx in ctxs])

    @pl.when(jnp.max(worst) > SAFE_LOG2_DROP)
    def _():
        for k, ctx in enumerate(ctxs):
            safe_path(k, *ctx[:-1])


def _hg_mix(p_hg, lb_logits, layer, tri):
    bsz, seq, _ = p_hg.shape
    n_chunks = seq // CHUNK
    rev = lambda c: n_chunks - 1 - c
    blk = (SCAN_BATCH, CHUNK, HG_WIDTH)
    fwd = lambda j: pl.BlockSpec(blk, lambda g, c: (g, c, j))
    bwd = lambda j: pl.BlockSpec(blk, lambda g, c: (g, rev(c), j))
    full = lambda a: pl.BlockSpec(a.shape, lambda g, c: (0,) * a.ndim)
    return pl.pallas_call(
        functools.partial(_hg_kernel, layer=layer),
        grid=(bsz // SCAN_BATCH, n_chunks),
        in_specs=[fwd(0), fwd(1), fwd(3), bwd(0), bwd(2), bwd(3), full(lb_logits), full(tri)],
        out_specs=[fwd(0), bwd(0)],
        out_shape=[jax.ShapeDtypeStruct((bsz, seq, HG_WIDTH), jnp.float32)] * 2,
        scratch_shapes=[pltpu.VMEM((SCAN_BATCH, N_DIR, HG_HEADS, HG_DK, HG_DK), jnp.float32),
                        pltpu.VMEM((HG_HEADS, CHUNK, HG_DK), jnp.float32),
                        pltpu.VMEM((HG_HEADS, CHUNK, HG_DK), jnp.float32)],
        compiler_params=pltpu.CompilerParams(
            dimension_semantics=("arbitrary", "arbitrary"), vmem_limit_bytes=VMEM_LIMIT),
        name="hgrn2_mix",
    )(p_hg, p_hg, p_hg, p_hg, p_hg, p_hg, lb_logits, tri)


def _out_kernel(x_ref, yf_ref, yb_ref, bonus_ref, gate_ref, of_ref, ob_ref, hgate_ref,
                lnw_ref, lnb_ref, hgn_ref, seg_ref, w_ref, post_ref, out_ref):
    y = yf_ref[...].astype(jnp.float32) + yb_ref[...].astype(jnp.float32)
    seg = seg_ref[...]
    mu = _head_sums(y, seg) * (1.0 / RW_HEAD_DIM)
    yc = y - mu
    var = _head_sums(yc * yc, seg) * (1.0 / RW_HEAD_DIM)
    y = yc * lax.rsqrt(var + GN_EPS) * lnw_ref[...] + lnb_ref[...]
    rw = (y + bonus_ref[...].astype(jnp.float32)) * gate_ref[...].astype(jnp.float32)

    o = of_ref[...] + ob_ref[...]
    parts = []
    for h in range(HG_HEADS):
        oh = o[:, h * HG_DK:(h + 1) * HG_DK]
        parts.append(oh * lax.rsqrt(jnp.mean(oh * oh, axis=-1, keepdims=True) + NORM_EPS))
    hg = jnp.concatenate(parts, axis=1) * hgn_ref[...] * hgate_ref[...].astype(jnp.float32)

    cat = jnp.concatenate([rw, hg], axis=1).astype(BF16)
    yo = jnp.dot(cat, w_ref[...], preferred_element_type=jnp.float32)
    ms = jnp.mean(yo * yo, axis=-1, keepdims=True)
    out_ref[...] = x_ref[...] + yo * lax.rsqrt(ms + NORM_EPS) * post_ref[...]


def _out_proj(x2, y_f, y_b, bonus, gate, o_f, o_b, hg_gate, ln_w, ln_b, hg_norm, seg, w_bf16, post_g):
    rows = x2.shape[0]
    tile = lambda n: pl.BlockSpec((OUT_TILE, n), lambda i: (i, 0))
    row_spec = lambda n: pl.BlockSpec((1, n), lambda i: (0, 0))
    return pl.pallas_call(
        _out_kernel,
        grid=(rows // OUT_TILE,),
        in_specs=[
            tile(D_MODEL),
            tile(RW_WIDTH), tile(RW_WIDTH), tile(RW_WIDTH), tile(RW_WIDTH),
            tile(HG_WIDTH), tile(HG_WIDTH), tile(HG_WIDTH),
            row_spec(RW_WIDTH), row_spec(RW_WIDTH), row_spec(HG_WIDTH),
            pl.BlockSpec((MXU_TILE, MXU_TILE), lambda i: (0, 0)),
            pl.BlockSpec((D_MODEL, D_MODEL), lambda i: (0, 0)),
            row_spec(D_MODEL),
        ],
        out_specs=tile(D_MODEL),
        out_shape=jax.ShapeDtypeStruct((rows, D_MODEL), jnp.float32),
        compiler_params=pltpu.CompilerParams(
            dimension_semantics=("arbitrary",), vmem_limit_bytes=VMEM_LIMIT),
        name="out_proj",
    )(x2, y_f, y_b, bonus, gate, o_f, o_b, hg_gate, ln_w, ln_b, hg_norm, seg, w_bf16, post_g)


def _lora_weights(w):
    z = jnp.zeros_like(w[0])
    wp = jnp.stack([jnp.concatenate([w[0], z], axis=0), jnp.concatenate([z, w[1]], axis=0)])
    return wp.astype(BF16)


def _scan_order_selectors(tile):
    t = jnp.arange(tile)
    same = (t[:, None] // CHUNK) == (t[None, :] // CHUNK)
    fwd = same & (t[None, :] <= t[:, None])
    bwd = same & (t[None, :] >= t[:, None])
    return jnp.stack([fwd, bwd]).astype(BF16)


def kernel(x, pre_norm_g, w_in, rw_shift_prev, rw_shift_next, rw_w0, rw_w2, rw_a0, rw_a2, rw_k_k, rw_k_a,
           rw_r_k, rw_ln_w, rw_ln_b, hg_lb_logits, hg_norm_g, w_out, post_norm_g):
    bsz, seq, dm = x.shape
    depth = w_in.shape[0]
    rows = bsz * seq
    lane = jnp.arange(MXU_TILE) // RW_HEAD_DIM
    seg = (lane[:, None] == lane[None, :]).astype(BF16)
    tri = _scan_order_selectors(ROW_TILE)
    hg_tri = _scan_order_selectors(CHUNK)
    row = lambda t: t.reshape(1, -1)
    flat = lambda t: t.reshape(rows, t.shape[-1])
    for l in range(depth):
        x2 = x.reshape(rows, dm)
        p_hg, hg_gate, ops, bkt, v_bf, gend, bonus, gate = _in_rw(
            x, row(pre_norm_g[l]), w_in[l].astype(BF16),
            row(rw_shift_prev[l]), row(rw_shift_next[l]),
            rw_w0[l], _lora_weights(rw_w2[l]), rw_a0[l], _lora_weights(rw_a2[l]),
            row(rw_k_k[l]), row(rw_k_a[l]), row(rw_r_k[l]), seg, tri)
        y_f, y_b = _rw_scan(ops, bkt, v_bf, gend)
        o_f, o_b = _hg_mix(p_hg, hg_lb_logits, l, hg_tri)
        out = _out_proj(
            x2, flat(y_f), flat(y_b), flat(bonus), flat(gate), flat(o_f), flat(o_b), flat(hg_gate),
            row(rw_ln_w[l]), row(rw_ln_b[l]), row(jnp.tile(hg_norm_g[l], HG_HEADS)), seg,
            w_out[l].astype(BF16), row(post_norm_g[l]))
        x = out.reshape(bsz, seq, dm)
    return x
```

```python
import functools
import itertools

import jax
import jax.numpy as jnp
from jax import lax
from jax.experimental import pallas as pl
from jax.experimental.pallas import tpu as pltpu

D_MODEL = 1024
RW_HEAD_DIM = 64
RW_WIDTH = 512
HG_DK = 128
HG_WIDTH = 512
HG_HEADS = 4
LORA = 64
NORM_EPS = 1e-6
GN_EPS = 64e-5
RW_COLS = 4 * RW_WIDTH + 4 * LORA
HG_COLS = 5 * HG_WIDTH
HG_SCAN_COLS = 4 * HG_WIDTH
N_DIR = 2
RW_OPS = 4

LANES = 128
LHS_ROWS = 16
MXU_TILE = 256
CHUNK = 64
SUB = 8
SAFE_LOG2_DROP = 100.0
ROW_TILE = 256
OUT_TILE = 512
HALO = 8
SCAN_BATCH = 4
VMEM_LIMIT = 56 * 1024 * 1024

_NT = (((1,), (1,)), ((), ()))
_TN = (((0,), (0,)), ((), ()))
BF16 = jnp.bfloat16
LOG2E = 1.4426950408889634


def _split_bf16(x, terms=2):
    pieces = []
    for _ in range(terms - 1):
        hi = x.astype(BF16)
        pieces.append(hi)
        x = x - hi.astype(jnp.float32)
    pieces.append(x.astype(BF16))
    return pieces


def _select_rows(sel, x, terms):
    acc = None
    for piece in _split_bf16(x, terms):
        t = jnp.dot(sel, piece, preferred_element_type=jnp.float32)
        acc = t if acc is None else acc + t
    return acc


def _head_sums(x, seg):
    xb = x.astype(BF16)
    return jnp.concatenate([jnp.dot(xb[:, c:c + MXU_TILE], seg, preferred_element_type=jnp.float32)
                            for c in range(0, x.shape[1], MXU_TILE)], axis=1)


def _sigmoid(x):
    return 1.0 / (1.0 + jnp.exp(-x))


class _PerPair:
    def __init__(self, load):
        self._load = load

    def __getitem__(self, pi):
        return self._load(pi)


def _chunk_totals(cum, d):
    last = CHUNK - 1 if d == 0 else 0
    return [cum[c * CHUNK + last:c * CHUNK + last + 1] for c in range(cum.shape[0] // CHUNK)]


def _expand_chunks(rows):
    return jnp.concatenate([jnp.broadcast_to(r, (CHUNK, r.shape[1])) for r in rows], axis=0)


def _in_rw_kernel(xc_ref, xp_ref, xn_ref, g_ref, w_ref,
                  mup_ref, mun_ref, w0_ref, w2_ref, a0_ref, a2_ref,
                  kk_ref, ka_ref, rk_ref, seg_ref, tri_ref,
                  phg_ref, hgate_ref, ops_ref, bkt_ref, v_ref, gend_ref, bonus_ref, gate_ref, *, n_tiles):
    j = pl.program_id(1)

    def pre_norm(x):
        return x * lax.rsqrt(jnp.mean(x * x, axis=-1, keepdims=True) + NORM_EPS) * g_ref[...]

    h_c = pre_norm(xc_ref[...])
    h_p = pre_norm(xp_ref[...]) * (j > 0).astype(jnp.float32)
    h_n = pre_norm(xn_ref[...]) * (j < n_tiles - 1).astype(jnp.float32)
    h = h_c.astype(BF16)
    h_ext = jnp.concatenate([h_p, h_c, h_n], axis=0).astype(BF16)

    hg_blocks = iter(range(HG_COLS // HG_WIDTH))

    def hg_block():
        i = next(hg_blocks, None)
        if i is not None:
            cols = slice(i * HG_WIDTH, (i + 1) * HG_WIDTH)
            blk = jnp.dot(h, w_ref[:, RW_COLS + cols.start:RW_COLS + cols.stop], preferred_element_type=jnp.float32)
            if i < HG_SCAN_COLS // HG_WIDTH:
                phg_ref[:, cols] = blk
            else:
                hgate_ref[...] = (blk * _sigmoid(blk)).astype(BF16)

    W = RW_WIDTH
    ext = ROW_TILE + 2 * HALO
    tile = slice(HALO, HALO + ROW_TILE)

    def shifted(cols):
        p_ext = jnp.dot(h_ext, w_ref[:, cols], preferred_element_type=jnp.float32)
        p, p_prev, p_next = p_ext[tile], pltpu.roll(p_ext, 1, 0)[tile], pltpu.roll(p_ext, ext - 1, 0)[tile]
        mup, mun = mup_ref[:, cols], mun_ref[:, cols]
        return p * (1.0 - mup - mun) + mup * p_prev + mun * p_next

    r, k, v, g = [shifted(slice(i * W, (i + 1) * W)) for i in range(4)]
    lora_in = shifted(slice(4 * W, RW_COLS))
    wd = lora_in[:, :2 * LORA]
    ad = lora_in[:, 2 * LORA:].astype(BF16)
    hg_block()
    gate_ref[...] = (g * _sigmoid(g)).astype(BF16)
    v_ref[...] = v.astype(BF16)

    seg = seg_ref[...]
    kk = k * kk_ref[...]
    kk = kk * lax.rsqrt(jnp.maximum(_head_sums(kk * kk, seg), 1e-24))
    tanh_wd = jnp.tanh(wd).astype(BF16)
    hg_block()

    k_sum = None
    for d in range(N_DIR):
        w_raw = w0_ref[d:d + 1, :] + jnp.dot(tanh_wd, w2_ref[d], preferred_element_type=jnp.float32)
        lw = (-LOG2E * jnp.exp(jnp.float32(-0.5))) * _sigmoid(w_raw)
        a = _sigmoid(a0_ref[d:d + 1, :] + jnp.dot(ad, a2_ref[d], preferred_element_type=jnp.float32))
        k_d = k * (1.0 + (a - 1.0) * ka_ref[...])
        b_d = kk * a
        k_sum = k_d if k_sum is None else k_sum + k_d
        cl = _select_rows(tri_ref[d], lw, 2)
        g_end = [jnp.exp2(tot) for tot in _chunk_totals(cl, d)]
        hg_block()
        e_out = jnp.exp2(-cl)
        e_end = e_out * _expand_chunks(g_end)
        ops = (r * jnp.exp2(cl),
               kk * jnp.exp2(cl - lw),
               k_d * e_end, b_d * e_end)
        for i, op in enumerate(ops):
            ops_ref[d, :, i * W:(i + 1) * W] = op.astype(BF16)
        b_t, k_t = b_d * e_out, k_d * e_out
        for q in range(ROW_TILE // CHUNK):
            rows = slice(q * CHUNK, (q + 1) * CHUNK)
            for t in range(W // LANES):
                cols = slice(t * LANES, (t + 1) * LANES)
                bk = jnp.concatenate([b_t[rows, cols], k_t[rows, cols]], axis=0)
                bkt_ref[d, q, t] = bk.astype(BF16).T
        for q, g_q in enumerate(g_end):
            gend_ref[d, q] = jnp.broadcast_to(g_q, (HALO, W))
        hg_block()

    bonus_ref[...] = (_head_sums(r * k_sum * rk_ref[...], seg) * v).astype(BF16)
    for _ in hg_blocks:
        raise AssertionError("HGRN2 column blocks left unissued")


def _in_rw(x, g, w_in, mup, mun, w0, w2, a0, a2, k_k, k_a, r_k, seg, tri):
    bsz, seq, _ = x.shape
    n_tiles = seq // ROW_TILE
    hb = ROW_TILE // HALO
    n_halo = seq // HALO
    n_chunks = seq // CHUNK
    cpt = ROW_TILE // CHUNK
    full = lambda a: pl.BlockSpec(a.shape, lambda b, j: (0,) * a.ndim)
    tile = lambda n: pl.BlockSpec((None, ROW_TILE, n), lambda b, j: (b, j, 0))
    consts = (g, w_in, mup, mun, w0, w2, a0, a2, k_k, k_a, r_k, seg, tri)
    return pl.pallas_call(
        functools.partial(_in_rw_kernel, n_tiles=n_tiles),
        grid=(bsz, n_tiles),
        in_specs=[
            tile(D_MODEL),
            pl.BlockSpec((None, HALO, D_MODEL), lambda b, j: (b, jnp.maximum(j * hb - 1, 0), 0)),
            pl.BlockSpec((None, HALO, D_MODEL), lambda b, j: (b, jnp.minimum((j + 1) * hb, n_halo - 1), 0)),
            *[full(a) for a in consts],
        ],
        out_specs=[
            tile(HG_SCAN_COLS),
            tile(HG_WIDTH),
            pl.BlockSpec((N_DIR, None, ROW_TILE, RW_OPS * RW_WIDTH), lambda b, j: (0, b, j, 0)),
            pl.BlockSpec((N_DIR, None, cpt, RW_WIDTH // LANES, 2 * CHUNK, LANES), lambda b, j: (0, b, j, 0, 0, 0)),
            tile(RW_WIDTH),
            pl.BlockSpec((N_DIR, None, cpt, HALO, RW_WIDTH), lambda b, j: (0, b, j, 0, 0)),
            tile(RW_WIDTH),
            tile(RW_WIDTH),
        ],
        out_shape=[
            jax.ShapeDtypeStruct((bsz, seq, HG_SCAN_COLS), jnp.float32),
            jax.ShapeDtypeStruct((bsz, seq, HG_WIDTH), BF16),
            jax.ShapeDtypeStruct((N_DIR, bsz, seq, RW_OPS * RW_WIDTH), BF16),
            jax.ShapeDtypeStruct((N_DIR, bsz, n_chunks, RW_WIDTH // LANES, 2 * CHUNK, LANES), BF16),
            jax.ShapeDtypeStruct((bsz, seq, RW_WIDTH), BF16),
            jax.ShapeDtypeStruct((N_DIR, bsz, n_chunks, HALO, RW_WIDTH), jnp.float32),
            jax.ShapeDtypeStruct((bsz, seq, RW_WIDTH), BF16),
            jax.ShapeDtypeStruct((bsz, seq, RW_WIDTH), BF16),
        ],
        compiler_params=pltpu.CompilerParams(
            dimension_semantics=("arbitrary", "arbitrary"), vmem_limit_bytes=VMEM_LIMIT),
        name="in_proj_rwkv7_prep",
    )(x, x, x, *consts)


def _rw_scan_kernel(of_ref, ob_ref, tf_ref, tb_ref, vf_ref, vb_ref, gf_ref, gb_ref, yf_ref, yb_ref, s_ref):
    c = pl.program_id(1)

    @pl.when(c == 0)
    def _():
        s_ref[...] = jnp.zeros_like(s_ref)

    N, W = RW_HEAD_DIM, RW_WIDTH
    lane = lax.broadcasted_iota(jnp.int32, (CHUNK, LANES), 1)
    tt = lax.broadcasted_iota(jnp.int32, (CHUNK, LANES), 0)
    half = (lane < N, lane >= N)
    diag = ((lax.broadcasted_iota(jnp.int32, (LANES, LANES), 0) < N)
            == (lax.broadcasted_iota(jnp.int32, (LANES, LANES), 1) < N))
    dir_refs = ((of_ref, vf_ref, gf_ref, yf_ref, tf_ref), (ob_ref, vb_ref, gb_ref, yb_ref, tb_ref))

    pairs = [(q, d, j) for q in range(SCAN_BATCH) for d in range(N_DIR) for j in range(W // LANES)]
    probs = [(pi, par) for pi in range(len(pairs)) for par in range(2)]

    def op(pi, i):
        q, d, j = pairs[pi]
        return dir_refs[d][0][q, :, i * W + j * LANES:i * W + (j + 1) * LANES]

    r_p, a_p, ke_p, be_p = [_PerPair(functools.partial(op, i=i)) for i in range(RW_OPS)]
    bkt_p = _PerPair(lambda pi: dir_refs[pairs[pi][1]][4][pairs[pi][0], pairs[pi][2]])
    v_p = _PerPair(lambda pi: dir_refs[pairs[pi][1]][1][pairs[pi][0], :,
                                                        pairs[pi][2] * LANES:(pairs[pi][2] + 1) * LANES])
    s_p = _PerPair(lambda pi: s_ref[pairs[pi]])

    def lag(pi):
        d = pairs[pi][1]
        delta = tt - (lane & (N - 1))
        return delta if d == 0 else -delta

    gram = []
    for pi, par in probs:
        lhs = jnp.concatenate([a_p[pi], r_p[pi]], axis=0)
        lhs = jnp.where(jnp.concatenate([half[par], half[par]], axis=0), lhs, jnp.zeros_like(lhs))
        gram.append(jnp.dot(lhs, bkt_p[pi], preferred_element_type=jnp.float32))

    za, l_ak, bot = [], [], []
    for n, (pi, par) in enumerate(probs):
        top = gram[n][:CHUNK]
        lg = lag(pi)
        m0 = jnp.where(jnp.where(half[0], lg, 0) > 0, -top, 0.0)
        l_ak.append(jnp.where(jnp.where(half[1], lg, 0) > 0, top, 0.0).astype(BF16))
        bot.append(jnp.where(lg >= 0, gram[n][CHUNK:], 0.0).astype(BF16))
        a_own = a_p[pi].astype(jnp.float32)
        za.append(jnp.where(half[1], a_own if par == 1 else pltpu.roll(a_own, N, 1), m0))
    zb = []
    for pi in range(len(pairs)):
        vv = jnp.concatenate([v_p[pi], v_p[pi]], axis=0)
        lv = jnp.dot(jnp.concatenate([l_ak[2 * pi], l_ak[2 * pi + 1]], axis=0), vv,
                     preferred_element_type=jnp.float32)
        zb += [lv[:CHUNK], lv[CHUNK:]]

    for level in range(6):
        skip = (2 ** level) // LHS_ROWS * LHS_ROWS
        new_za, new_zb = [], []
        for n, (pi, par) in enumerate(probs):
            live = slice(skip, CHUNK) if pairs[pi][1] == 0 else slice(0, CHUNK - skip)
            m = za[n][live, :N].astype(BF16)
            z2 = jnp.concatenate([za[n], zb[n]], axis=1).astype(BF16)
            p = jnp.dot(m, z2, preferred_element_type=jnp.float32)
            keep_a = jnp.where(half[1], za[n], 0.0)
            pieces_a = [keep_a[:live.start], p[:, :LANES] + keep_a[live], keep_a[live.stop:]]
            pieces_b = [zb[n][:live.start], p[:, LANES:] + zb[n][live], zb[n][live.stop:]]
            new_za.append(jnp.concatenate([x for x in pieces_a if x.shape[0]], axis=0))
            new_zb.append(jnp.concatenate([x for x in pieces_b if x.shape[0]], axis=0))
        za, zb = new_za, new_zb

    n_pairs = range(len(pairs))
    s_bf = [s_p[pi].astype(BF16) for pi in n_pairs]
    wr = [jnp.concatenate([jnp.where(half[0], pltpu.roll(za[2 * pi], N, 1), za[2 * pi + 1]).astype(BF16),
                           r_p[pi]], axis=0) for pi in n_pairs]
    ws = [lax.dot_general(wr[pi], s_bf[pi], _NT, preferred_element_type=jnp.float32) for pi in n_pairs]
    u_bf = [(-(ws[pi][:CHUNK] + jnp.where(half[0], zb[2 * pi], zb[2 * pi + 1]))).astype(BF16)
            for pi in n_pairs]
    uv = [jnp.concatenate([u_bf[pi], v_p[pi]], axis=0) for pi in n_pairs]
    y_eo = [jnp.dot(jnp.concatenate([bot[2 * pi], bot[2 * pi + 1]], axis=0), uv[pi],
                    preferred_element_type=jnp.float32) for pi in n_pairs]
    upd = [lax.dot_general(uv[pi], jnp.concatenate([be_p[pi], ke_p[pi]], axis=0), _TN,
                           preferred_element_type=jnp.float32) for pi in n_pairs]
    for pi, (q, d, j) in enumerate(pairs):
        y = ws[pi][CHUNK:] + jnp.where(half[0], y_eo[pi][:CHUNK], y_eo[pi][CHUNK:])
        dir_refs[d][3][q, :, j * LANES:(j + 1) * LANES] = y.astype(BF16)
        g_end = dir_refs[d][2][q, 0:1, j * LANES:(j + 1) * LANES]
        s_ref[q, d, j] = s_p[pi] * g_end + jnp.where(diag, upd[pi], 0.0)


def _rw_scan(ops, bkt, v_bf, gend):
    _, bsz, seq, _ = ops.shape
    n_chunks = seq // CHUNK
    rev = lambda c: n_chunks - 1 - c
    return pl.pallas_call(
        _rw_scan_kernel,
        grid=(bsz // SCAN_BATCH, n_chunks),
        in_specs=[
            pl.BlockSpec((None, SCAN_BATCH, CHUNK, RW_OPS * RW_WIDTH), lambda b, c: (0, b, c, 0)),
            pl.BlockSpec((None, SCAN_BATCH, CHUNK, RW_OPS * RW_WIDTH), lambda b, c: (1, b, rev(c), 0)),
            pl.BlockSpec((None, SCAN_BATCH, None, RW_WIDTH // LANES, 2 * CHUNK, LANES),
                         lambda b, c: (0, b, c, 0, 0, 0)),
            pl.BlockSpec((None, SCAN_BATCH, None, RW_WIDTH // LANES, 2 * CHUNK, LANES),
                         lambda b, c: (1, b, rev(c), 0, 0, 0)),
            pl.BlockSpec((SCAN_BATCH, CHUNK, RW_WIDTH), lambda b, c: (b, c, 0)),
            pl.BlockSpec((SCAN_BATCH, CHUNK, RW_WIDTH), lambda b, c: (b, rev(c), 0)),
            pl.BlockSpec((None, SCAN_BATCH, None, HALO, RW_WIDTH), lambda b, c: (0, b, c, 0, 0)),
            pl.BlockSpec((None, SCAN_BATCH, None, HALO, RW_WIDTH), lambda b, c: (1, b, rev(c), 0, 0)),
        ],
        out_specs=[
            pl.BlockSpec((SCAN_BATCH, CHUNK, RW_WIDTH), lambda b, c: (b, c, 0)),
            pl.BlockSpec((SCAN_BATCH, CHUNK, RW_WIDTH), lambda b, c: (b, rev(c), 0)),
        ],
        out_shape=[jax.ShapeDtypeStruct((bsz, seq, RW_WIDTH), BF16)] * 2,
        scratch_shapes=[pltpu.VMEM((SCAN_BATCH, N_DIR, RW_WIDTH // LANES, LANES, LANES), jnp.float32)],
        compiler_params=pltpu.CompilerParams(
            dimension_semantics=("arbitrary", "arbitrary"), vmem_limit_bytes=VMEM_LIMIT),
        name="rwkv7_scan",
    )(ops, ops, bkt, bkt, v_bf, v_bf, gend, gend)


def _hg_kernel(qf_ref, ff_ref, if_ref, qb_ref, fb_ref, ib_ref, lbl_ref, tri_ref,
               of_ref, ob_ref, s_ref, cs_ref, vs_ref, acc_ref, *, layer):
    c = pl.program_id(1)

    @pl.when(c == 0)
    def _():
        s_ref[...] = jnp.zeros_like(s_ref)

    lg = lbl_ref[...]
    e = jnp.exp(lg - jnp.max(lg, axis=0, keepdims=True))
    lb = jnp.sum(e[0:layer + 1, :], axis=0, keepdims=True) / jnp.sum(e, axis=0, keepdims=True)

    heads = range(HG_HEADS)
    sls = [slice(h * HG_DK, (h + 1) * HG_DK) for h in heads]
    trow = lax.broadcasted_iota(jnp.int32, (SUB, HG_DK), 0)
    dir_refs = ((qf_ref, ff_ref, if_ref, of_ref), (qb_ref, fb_ref, ib_ref, ob_ref))
    probs = [(n, d) for n in range(SCAN_BATCH) for d in range(N_DIR)]

    def fast_path(k, out):
        n, d = probs[k]
        q_ref, f_ref, i_ref, o_ref = dir_refs[d]
        q, v = q_ref[n], i_ref[n]
        v_bf = v.astype(BF16)
        f = lb + (1.0 - lb) * _sigmoid(f_ref[n])
        kf = 1.0 - f
        yield
        b2 = _select_rows(tri_ref[d], jnp.log2(f), 2)
        yield
        (total,) = _chunk_totals(b2, d)
        q_in = (q * jnp.exp2(b2)).astype(BF16)
        k_end = (kf * jnp.exp2(total - b2)).astype(BF16)
        g_end = jnp.exp2(total)
        st = [s_ref[n, d, h] for h in heads]
        inter = [lax.dot_general(q_in[:, sls[h]], st[h].astype(BF16), _NT, preferred_element_type=jnp.float32)
                 for h in heads]
        for h in heads:
            s_ref[n, d, h] = st[h] * g_end[:, sls[h]] + lax.dot_general(
                v_bf[:, sls[h]], k_end[:, sls[h]], _TN, preferred_element_type=jnp.float32)
        yield
        mid = CHUNK // 2 - 1 if d == 0 else CHUNK // 2
        dev = b2 - b2[mid:mid + 1]
        q_c = (q * jnp.exp2(dev)).astype(BF16)
        k_c = (kf * jnp.exp2(-dev)).astype(BF16)
        scores = [lax.dot_general(q_c[:, sls[h]], k_c[:, sls[h]], _NT, preferred_element_type=jnp.float32)
                  for h in heads]
        out[k] = (q, kf, b2, v, v_bf, inter, jnp.max(jnp.abs(dev), axis=0, keepdims=True))
        yield
        tt = lax.broadcasted_iota(jnp.int32, (CHUNK, CHUNK), 0)
        ss = lax.broadcasted_iota(jnp.int32, (CHUNK, CHUNK), 1)
        seen = (tt >= ss) if d == 0 else (tt <= ss)
        for h in heads:
            o_ref[n, :, sls[h]] = (inter[h] + jnp.dot(jnp.where(seen, scores[h], 0.0).astype(BF16), v_bf[:, sls[h]],
                                                       preferred_element_type=jnp.float32)).astype(BF16)

    def safe_path(k, q, kf, b2, v, v_bf, inter):
        n, d = probs[k]
        o_ref = dir_refs[d][3]
        cs = b2 - jnp.log2(kf)
        for h in heads:
            cs_ref[h] = cs[:, sls[h]]
            vs_ref[h] = v[:, sls[h]]
        for lo in range(0, CHUNK, SUB):
            q_i, b_i = q[lo:lo + SUB], b2[lo:lo + SUB]
            acc = [inter[h][lo:lo + SUB] for h in heads]
            for s in range(SUB):
                row_s = pl.ds(lo + s, SUB, stride=0)
                m = (trow >= s) if d == 0 else (trow <= s)
                for h in heads:
                    xs = jnp.where(m, q_i[:, sls[h]] * jnp.exp2(b_i[:, sls[h]] - cs_ref[h, row_s, :]), 0.0)
                    w = jnp.sum(xs, axis=-1, keepdims=True)
                    acc[h] = acc[h] + w * vs_ref[h, row_s, :]
            for h in heads:
                acc_ref[lo:lo + SUB, sls[h]] = acc[h]

        def rows(p_lo, p_hi):
            lo = p_lo if d == 0 else CHUNK - p_hi
            return slice(lo, lo + (p_hi - p_lo))

        size = CHUNK // 2
        while size >= SUB:
            for p in range(0, CHUNK, 2 * size):
                s_sl, t_sl = rows(p, p + size), rows(p + size, p + 2 * size)
                ref = b2[rows(p + size - 1, p + size)]
                q_t = (q[t_sl] * jnp.exp2(b2[t_sl] - ref)).astype(BF16)
                k_s = (kf[s_sl] * jnp.exp2(ref - b2[s_sl])).astype(BF16)
                for h in heads:
                    sc = lax.dot_general(q_t[:, sls[h]], k_s[:, sls[h]], _NT, preferred_element_type=jnp.float32)
                    acc_ref[t_sl, sls[h]] += jnp.dot(sc.astype(BF16), v_bf[s_sl, sls[h]],
                                                     preferred_element_type=jnp.float32)
            size //= 2
        o_ref[n] = acc_ref[...].astype(BF16)

    ctxs = [None] * len(probs)
    for _ in itertools.zip_longest(*[fast_path(k, ctxs) for k in range(len(probs))]):
        pass
    worst = functools.reduce(jnp.maximum, [ctx[-1] for ctx in ctxs])

    @pl.when(jnp.max(worst) > SAFE_LOG2_DROP)
    def _():
        for k, ctx in enumerate(ctxs):
            safe_path(k, *ctx[:-1])


def _hg_mix(p_hg, lb_logits, layer, tri):
    bsz, seq, _ = p_hg.shape
    n_chunks = seq // CHUNK
    rev = lambda c: n_chunks - 1 - c
    blk = (SCAN_BATCH, CHUNK, HG_WIDTH)
    fwd = lambda j: pl.BlockSpec(blk, lambda g, c: (g, c, j))
    bwd = lambda j: pl.BlockSpec(blk, lambda g, c: (g, rev(c), j))
    full = lambda a: pl.BlockSpec(a.shape, lambda g, c: (0,) * a.ndim)
    return pl.pallas_call(
        functools.partial(_hg_kernel, layer=layer),
        grid=(bsz // SCAN_BATCH, n_chunks),
        in_specs=[fwd(0), fwd(1), fwd(3), bwd(0), bwd(2), bwd(3), full(lb_logits), full(tri)],
        out_specs=[fwd(0), bwd(0)],
        out_shape=[jax.ShapeDtypeStruct((bsz, seq, HG_WIDTH), BF16)] * 2,
        scratch_shapes=[pltpu.VMEM((SCAN_BATCH, N_DIR, HG_HEADS, HG_DK, HG_DK), jnp.float32),
                        pltpu.VMEM((HG_HEADS, CHUNK, HG_DK), jnp.float32),
                        pltpu.VMEM((HG_HEADS, CHUNK, HG_DK), jnp.float32),
                        pltpu.VMEM((CHUNK, HG_WIDTH), jnp.float32)],
        compiler_params=pltpu.CompilerParams(
            dimension_semantics=("arbitrary", "arbitrary"), vmem_limit_bytes=VMEM_LIMIT),
        name="hgrn2_mix",
    )(p_hg, p_hg, p_hg, p_hg, p_hg, p_hg, lb_logits, tri)


def _out_kernel(x_ref, yf_ref, yb_ref, bonus_ref, gate_ref, of_ref, ob_ref, hgate_ref,
                lnw_ref, lnb_ref, hgn_ref, seg_ref, w_ref, post_ref, out_ref):
    y = yf_ref[...].astype(jnp.float32) + yb_ref[...].astype(jnp.float32)
    seg = seg_ref[...]
    mu = _head_sums(y, seg) * (1.0 / RW_HEAD_DIM)
    yc = y - mu
    var = _head_sums(yc * yc, seg) * (1.0 / RW_HEAD_DIM)
    y = yc * lax.rsqrt(var + GN_EPS) * lnw_ref[...] + lnb_ref[...]
    rw = (y + bonus_ref[...].astype(jnp.float32)) * gate_ref[...].astype(jnp.float32)

    o = of_ref[...].astype(jnp.float32) + ob_ref[...].astype(jnp.float32)
    parts = []
    for h in range(HG_HEADS):
        oh = o[:, h * HG_DK:(h + 1) * HG_DK]
        parts.append(oh * lax.rsqrt(jnp.mean(oh * oh, axis=-1, keepdims=True) + NORM_EPS))
    hg = jnp.concatenate(parts, axis=1) * hgn_ref[...] * hgate_ref[...].astype(jnp.float32)

    cat = jnp.concatenate([rw, hg], axis=1).astype(BF16)
    yo = jnp.dot(cat, w_ref[...], preferred_element_type=jnp.float32)
    ms = jnp.mean(yo * yo, axis=-1, keepdims=True)
    out_ref[...] = x_ref[...] + yo * lax.rsqrt(ms + NORM_EPS) * post_ref[...]


def _out_proj(x2, y_f, y_b, bonus, gate, o_f, o_b, hg_gate, ln_w, ln_b, hg_norm, seg, w_bf16, post_g):
    rows = x2.shape[0]
    tile = lambda n: pl.BlockSpec((OUT_TILE, n), lambda i: (i, 0))
    row_spec = lambda n: pl.BlockSpec((1, n), lambda i: (0, 0))
    return pl.pallas_call(
        _out_kernel,
        grid=(rows // OUT_TILE,),
        in_specs=[
            tile(D_MODEL),
            tile(RW_WIDTH), tile(RW_WIDTH), tile(RW_WIDTH), tile(RW_WIDTH),
            tile(HG_WIDTH), tile(HG_WIDTH), tile(HG_WIDTH),
            row_spec(RW_WIDTH), row_spec(RW_WIDTH), row_spec(HG_WIDTH),
            pl.BlockSpec((MXU_TILE, MXU_TILE), lambda i: (0, 0)),
            pl.BlockSpec((D_MODEL, D_MODEL), lambda i: (0, 0)),
            row_spec(D_MODEL),
        ],
        out_specs=tile(D_MODEL),
        out_shape=jax.ShapeDtypeStruct((rows, D_MODEL), jnp.float32),
        compiler_params=pltpu.CompilerParams(
            dimension_semantics=("arbitrary",), vmem_limit_bytes=VMEM_LIMIT),
        name="out_proj",
    )(x2, y_f, y_b, bonus, gate, o_f, o_b, hg_gate, ln_w, ln_b, hg_norm, seg, w_bf16, post_g)


def _lora_weights(w):
    z = jnp.zeros_like(w[0])
    wp = jnp.stack([jnp.concatenate([w[0], z], axis=0), jnp.concatenate([z, w[1]], axis=0)])
    return wp.astype(BF16)


def _scan_order_selectors(tile):
    t = jnp.arange(tile)
    same = (t[:, None] // CHUNK) == (t[None, :] // CHUNK)
    fwd = same & (t[None, :] <= t[:, None])
    bwd = same & (t[None, :] >= t[:, None])
    return jnp.stack([fwd, bwd]).astype(BF16)


def kernel(x, pre_norm_g, w_in, rw_shift_prev, rw_shift_next, rw_w0, rw_w2, rw_a0, rw_a2, rw_k_k, rw_k_a,
           rw_r_k, rw_ln_w, rw_ln_b, hg_lb_logits, hg_norm_g, w_out, post_norm_g):
    bsz, seq, dm = x.shape
    depth = w_in.shape[0]
    rows = bsz * seq
    lane = jnp.arange(MXU_TILE) // RW_HEAD_DIM
    seg = (lane[:, None] == lane[None, :]).astype(BF16)
    tri = _scan_order_selectors(ROW_TILE)
    hg_tri = _scan_order_selectors(CHUNK)
    row = lambda t: t.reshape(1, -1)
    flat = lambda t: t.reshape(rows, t.shape[-1])
    for l in range(depth):
        x2 = x.reshape(rows, dm)
        p_hg, hg_gate, ops, bkt, v_bf, gend, bonus, gate = _in_rw(
            x, row(pre_norm_g[l]), w_in[l].astype(BF16),
            row(rw_shift_prev[l]), row(rw_shift_next[l]),
            rw_w0[l], _lora_weights(rw_w2[l]), rw_a0[l], _lora_weights(rw_a2[l]),
            row(rw_k_k[l]), row(rw_k_a[l]), row(rw_r_k[l]), seg, tri)
        y_f, y_b = _rw_scan(ops, bkt, v_bf, gend)
        o_f, o_b = _hg_mix(p_hg, hg_lb_logits, l, hg_tri)
        out = _out_proj(
            x2, flat(y_f), flat(y_b), flat(bonus), flat(gate), flat(o_f), flat(o_b), flat(hg_gate),
            row(rw_ln_w[l]), row(rw_ln_b[l]), row(jnp.tile(hg_norm_g[l], HG_HEADS)), seg,
            w_out[l].astype(BF16), row(post_norm_g[l]))
        x = out.reshape(bsz, seq, dm)
    return x
```

```python
import functools
import itertools

import jax
import jax.numpy as jnp
from jax import lax
from jax.experimental import pallas as pl
from jax.experimental.pallas import tpu as pltpu

D_MODEL = 1024
RW_HEAD_DIM = 64
RW_WIDTH = 512
HG_DK = 128
HG_WIDTH = 512
HG_HEADS = 4
LORA = 64
NORM_EPS = 1e-6
GN_EPS = 64e-5
RW_COLS = 4 * RW_WIDTH + 4 * LORA
HG_COLS = 5 * HG_WIDTH
HG_SCAN_COLS = 4 * HG_WIDTH
N_DIR = 2
RW_OPS = 4

LANES = 128
LHS_ROWS = 16
MXU_TILE = 256
CHUNK = 64
SUB = 8
SAFE_LOG2_DROP = 100.0
ROW_TILE = 256
OUT_TILE = 512
HALO = 8
SCAN_BATCH = 4
VMEM_LIMIT = 56 * 1024 * 1024

_NT = (((1,), (1,)), ((), ()))
_TN = (((0,), (0,)), ((), ()))
BF16 = jnp.bfloat16
LOG2E = 1.4426950408889634


def _split_bf16(x, terms=2):
    pieces = []
    for _ in range(terms - 1):
        hi = x.astype(BF16)
        pieces.append(hi)
        x = x - hi.astype(jnp.float32)
    pieces.append(x.astype(BF16))
    return pieces


def _select_rows(sel, x, terms):
    acc = None
    for piece in _split_bf16(x, terms):
        t = jnp.dot(sel, piece, preferred_element_type=jnp.float32)
        acc = t if acc is None else acc + t
    return acc


def _head_sums(x, seg):
    xb = x.astype(BF16)
    return jnp.concatenate([jnp.dot(xb[:, c:c + MXU_TILE], seg, preferred_element_type=jnp.float32)
                            for c in range(0, x.shape[1], MXU_TILE)], axis=1)


def _sigmoid(x):
    return 1.0 / (1.0 + jnp.exp(-x))


class _PerPair:
    def __init__(self, load):
        self._load = load

    def __getitem__(self, pi):
        return self._load(pi)


def _chunk_totals(cum, d):
    last = CHUNK - 1 if d == 0 else 0
    return [cum[c * CHUNK + last:c * CHUNK + last + 1] for c in range(cum.shape[0] // CHUNK)]


def _expand_chunks(rows):
    return jnp.concatenate([jnp.broadcast_to(r, (CHUNK, r.shape[1])) for r in rows], axis=0)


def _in_rw_kernel(xc_ref, xp_ref, xn_ref, g_ref, w_ref,
                  mup_ref, mun_ref, w0_ref, w2_ref, a0_ref, a2_ref,
                  kk_ref, ka_ref, rk_ref, seg_ref, tri_ref,
                  phg_ref, hgate_ref, ops_ref, bkt_ref, v_ref, gend_ref, bonus_ref, gate_ref, *, n_tiles):
    j = pl.program_id(1)

    def pre_norm(x):
        return x * lax.rsqrt(jnp.mean(x * x, axis=-1, keepdims=True) + NORM_EPS) * g_ref[...]

    h_c = pre_norm(xc_ref[...])
    h_p = pre_norm(xp_ref[...]) * (j > 0).astype(jnp.float32)
    h_n = pre_norm(xn_ref[...]) * (j < n_tiles - 1).astype(jnp.float32)
    h = h_c.astype(BF16)
    h_ext = jnp.concatenate([h_p, h_c, h_n], axis=0).astype(BF16)

    hg_blocks = iter(range(HG_COLS // HG_WIDTH))

    def hg_block():
        i = next(hg_blocks, None)
        if i is not None:
            cols = slice(i * HG_WIDTH, (i + 1) * HG_WIDTH)
            blk = jnp.dot(h, w_ref[:, RW_COLS + cols.start:RW_COLS + cols.stop], preferred_element_type=jnp.float32)
            if i < HG_SCAN_COLS // HG_WIDTH:
                phg_ref[:, cols] = blk
            else:
                hgate_ref[...] = (blk * _sigmoid(blk)).astype(BF16)

    W = RW_WIDTH
    ext = ROW_TILE + 2 * HALO
    tile = slice(HALO, HALO + ROW_TILE)

    def shifted(cols):
        p_ext = jnp.dot(h_ext, w_ref[:, cols], preferred_element_type=jnp.float32)
        p, p_prev, p_next = p_ext[tile], pltpu.roll(p_ext, 1, 0)[tile], pltpu.roll(p_ext, ext - 1, 0)[tile]
        mup, mun = mup_ref[:, cols], mun_ref[:, cols]
        return p * (1.0 - mup - mun) + mup * p_prev + mun * p_next

    col = lambda i: slice(i * W, (i + 1) * W)
    lora_in = shifted(slice(4 * W, RW_COLS))
    tanh_wd = jnp.tanh(lora_in[:, :2 * LORA]).astype(BF16)
    ad = lora_in[:, 2 * LORA:].astype(BF16)
    k = shifted(col(1))
    w_raw = [w0_ref[d:d + 1, :] + jnp.dot(tanh_wd, w2_ref[d], preferred_element_type=jnp.float32)
             for d in range(N_DIR)]
    a_raw = [a0_ref[d:d + 1, :] + jnp.dot(ad, a2_ref[d], preferred_element_type=jnp.float32)
             for d in range(N_DIR)]
    r = shifted(col(0))

    seg = seg_ref[...]
    kk = k * kk_ref[...]
    kk = kk * lax.rsqrt(jnp.maximum(_head_sums(kk * kk, seg), 1e-24))
    lws = [(-LOG2E * jnp.exp(jnp.float32(-0.5))) * _sigmoid(w) for w in w_raw]
    v = shifted(col(2))
    cls = [_select_rows(tri_ref[d], lws[d], 2) for d in range(N_DIR)]
    v_ref[...] = v.astype(BF16)
    g = shifted(col(3))
    gate_ref[...] = (g * _sigmoid(g)).astype(BF16)
    hg_block()

    k_sum = None
    for d in range(N_DIR):
        lw, cl = lws[d], cls[d]
        a = _sigmoid(a_raw[d])
        k_d = k * (1.0 + (a - 1.0) * ka_ref[...])
        b_d = kk * a
        k_sum = k_d if k_sum is None else k_sum + k_d
        hg_block()
        g_end = [jnp.exp2(tot) for tot in _chunk_totals(cl, d)]
        e_out = jnp.exp2(-cl)
        e_end = e_out * _expand_chunks(g_end)
        ops = (r * jnp.exp2(cl),
               kk * jnp.exp2(cl - lw),
               k_d * e_end, b_d * e_end)
        for i, op in enumerate(ops):
            ops_ref[d, :, i * W:(i + 1) * W] = op.astype(BF16)
        b_t, k_t = b_d * e_out, k_d * e_out
        for q in range(ROW_TILE // CHUNK):
            rows = slice(q * CHUNK, (q + 1) * CHUNK)
            for t in range(W // LANES):
                cols = slice(t * LANES, (t + 1) * LANES)
                bk = jnp.concatenate([b_t[rows, cols], k_t[rows, cols]], axis=0)
                bkt_ref[d, q, t] = bk.astype(BF16).T
        for q, g_q in enumerate(g_end):
            gend_ref[d, q] = jnp.broadcast_to(g_q, (HALO, W))
        hg_block()

    bonus_ref[...] = (_head_sums(r * k_sum * rk_ref[...], seg) * v).astype(BF16)
    for _ in hg_blocks:
        raise AssertionError("HGRN2 column blocks left unissued")


def _in_rw(x, g, w_in, mup, mun, w0, w2, a0, a2, k_k, k_a, r_k, seg, tri):
    bsz, seq, _ = x.shape
    n_tiles = seq // ROW_TILE
    hb = ROW_TILE // HALO
    n_halo = seq // HALO
    n_chunks = seq // CHUNK
    cpt = ROW_TILE // CHUNK
    full = lambda a: pl.BlockSpec(a.shape, lambda b, j: (0,) * a.ndim)
    tile = lambda n: pl.BlockSpec((None, ROW_TILE, n), lambda b, j: (b, j, 0))
    consts = (g, w_in, mup, mun, w0, w2, a0, a2, k_k, k_a, r_k, seg, tri)
    return pl.pallas_call(
        functools.partial(_in_rw_kernel, n_tiles=n_tiles),
        grid=(bsz, n_tiles),
        in_specs=[
            tile(D_MODEL),
            pl.BlockSpec((None, HALO, D_MODEL), lambda b, j: (b, jnp.maximum(j * hb - 1, 0), 0)),
            pl.BlockSpec((None, HALO, D_MODEL), lambda b, j: (b, jnp.minimum((j + 1) * hb, n_halo - 1), 0)),
            *[full(a) for a in consts],
        ],
        out_specs=[
            tile(HG_SCAN_COLS),
            tile(HG_WIDTH),
            pl.BlockSpec((N_DIR, None, ROW_TILE, RW_OPS * RW_WIDTH), lambda b, j: (0, b, j, 0)),
            pl.BlockSpec((N_DIR, None, cpt, RW_WIDTH // LANES, 2 * CHUNK, LANES), lambda b, j: (0, b, j, 0, 0, 0)),
            tile(RW_WIDTH),
            pl.BlockSpec((N_DIR, None, cpt, HALO, RW_WIDTH), lambda b, j: (0, b, j, 0, 0)),
            tile(RW_WIDTH),
            tile(RW_WIDTH),
        ],
        out_shape=[
            jax.ShapeDtypeStruct((bsz, seq, HG_SCAN_COLS), jnp.float32),
            jax.ShapeDtypeStruct((bsz, seq, HG_WIDTH), BF16),
            jax.ShapeDtypeStruct((N_DIR, bsz, seq, RW_OPS * RW_WIDTH), BF16),
            jax.ShapeDtypeStruct((N_DIR, bsz, n_chunks, RW_WIDTH // LANES, 2 * CHUNK, LANES), BF16),
            jax.ShapeDtypeStruct((bsz, seq, RW_WIDTH), BF16),
            jax.ShapeDtypeStruct((N_DIR, bsz, n_chunks, HALO, RW_WIDTH), jnp.float32),
            jax.ShapeDtypeStruct((bsz, seq, RW_WIDTH), BF16),
            jax.ShapeDtypeStruct((bsz, seq, RW_WIDTH), BF16),
        ],
        compiler_params=pltpu.CompilerParams(
            dimension_semantics=("arbitrary", "arbitrary"), vmem_limit_bytes=VMEM_LIMIT),
        name="in_proj_rwkv7_prep",
    )(x, x, x, *consts)


def _rw_scan_kernel(of_ref, ob_ref, tf_ref, tb_ref, vf_ref, vb_ref, gf_ref, gb_ref, yf_ref, yb_ref, s_ref):
    c = pl.program_id(1)

    @pl.when(c == 0)
    def _():
        s_ref[...] = jnp.zeros_like(s_ref)

    N, W = RW_HEAD_DIM, RW_WIDTH
    lane = lax.broadcasted_iota(jnp.int32, (CHUNK, LANES), 1)
    tt = lax.broadcasted_iota(jnp.int32, (CHUNK, LANES), 0)
    half = (lane < N, lane >= N)
    diag = ((lax.broadcasted_iota(jnp.int32, (LANES, LANES), 0) < N)
            == (lax.broadcasted_iota(jnp.int32, (LANES, LANES), 1) < N))
    dir_refs = ((of_ref, vf_ref, gf_ref, yf_ref, tf_ref), (ob_ref, vb_ref, gb_ref, yb_ref, tb_ref))

    pairs = [(q, d, j) for q in range(SCAN_BATCH) for d in range(N_DIR) for j in range(W // LANES)]
    probs = [(pi, par) for pi in range(len(pairs)) for par in range(2)]

    def op(pi, i):
        q, d, j = pairs[pi]
        return dir_refs[d][0][q, :, i * W + j * LANES:i * W + (j + 1) * LANES]

    r_p, a_p, ke_p, be_p = [_PerPair(functools.partial(op, i=i)) for i in range(RW_OPS)]
    bkt_p = _PerPair(lambda pi: dir_refs[pairs[pi][1]][4][pairs[pi][0], pairs[pi][2]])
    v_p = _PerPair(lambda pi: dir_refs[pairs[pi][1]][1][pairs[pi][0], :,
                                                        pairs[pi][2] * LANES:(pairs[pi][2] + 1) * LANES])
    s_p = _PerPair(lambda pi: s_ref[pairs[pi]])

    def lag(pi):
        d = pairs[pi][1]
        delta = tt - (lane & (N - 1))
        return delta if d == 0 else -delta

    gram = []
    for pi, par in probs:
        lhs = jnp.concatenate([a_p[pi], r_p[pi]], axis=0)
        lhs = jnp.where(jnp.concatenate([half[par], half[par]], axis=0), lhs, jnp.zeros_like(lhs))
        gram.append(jnp.dot(lhs, bkt_p[pi], preferred_element_type=jnp.float32))

    za, l_ak, bot = [], [], []
    for n, (pi, par) in enumerate(probs):
        top = gram[n][:CHUNK]
        lg = lag(pi)
        m0 = jnp.where(jnp.where(half[0], lg, 0) > 0, -top, 0.0)
        l_ak.append(jnp.where(jnp.where(half[1], lg, 0) > 0, top, 0.0).astype(BF16))
        bot.append(jnp.where(lg >= 0, gram[n][CHUNK:], 0.0).astype(BF16))
        a_own = a_p[pi].astype(jnp.float32)
        za.append(jnp.where(half[1], a_own if par == 1 else pltpu.roll(a_own, N, 1), m0))
    zb = []
    for pi in range(len(pairs)):
        vv = jnp.concatenate([v_p[pi], v_p[pi]], axis=0)
        lv = jnp.dot(jnp.concatenate([l_ak[2 * pi], l_ak[2 * pi + 1]], axis=0), vv,
                     preferred_element_type=jnp.float32)
        zb += [lv[:CHUNK], lv[CHUNK:]]

    for level in range(6):
        skip = (2 ** level) // LHS_ROWS * LHS_ROWS
        new_za, new_zb = [], []
        for n, (pi, par) in enumerate(probs):
            live = slice(skip, CHUNK) if pairs[pi][1] == 0 else slice(0, CHUNK - skip)
            m = za[n][live, :N].astype(BF16)
            z2 = jnp.concatenate([za[n], zb[n]], axis=1).astype(BF16)
            p = jnp.dot(m, z2, preferred_element_type=jnp.float32)
            keep_a = jnp.where(half[1], za[n], 0.0)
            pieces_a = [keep_a[:live.start], p[:, :LANES] + keep_a[live], keep_a[live.stop:]]
            pieces_b = [zb[n][:live.start], p[:, LANES:] + zb[n][live], zb[n][live.stop:]]
            new_za.append(jnp.concatenate([x for x in pieces_a if x.shape[0]], axis=0))
            new_zb.append(jnp.concatenate([x for x in pieces_b if x.shape[0]], axis=0))
        za, zb = new_za, new_zb

    n_pairs = range(len(pairs))
    s_bf = [s_p[pi].astype(BF16) for pi in n_pairs]
    wr = [jnp.concatenate([jnp.where(half[0], pltpu.roll(za[2 * pi], N, 1), za[2 * pi + 1]).astype(BF16),
                           r_p[pi]], axis=0) for pi in n_pairs]
    ws = [lax.dot_general(wr[pi], s_bf[pi], _NT, preferred_element_type=jnp.float32) for pi in n_pairs]
    u_bf = [(-(ws[pi][:CHUNK] + jnp.where(half[0], zb[2 * pi], zb[2 * pi + 1]))).astype(BF16)
            for pi in n_pairs]
    uv = [jnp.concatenate([u_bf[pi], v_p[pi]], axis=0) for pi in n_pairs]
    y_eo = [jnp.dot(jnp.concatenate([bot[2 * pi], bot[2 * pi + 1]], axis=0), uv[pi],
                    preferred_element_type=jnp.float32) for pi in n_pairs]
    upd = [lax.dot_general(uv[pi], jnp.concatenate([be_p[pi], ke_p[pi]], axis=0), _TN,
                           preferred_element_type=jnp.float32) for pi in n_pairs]
    for pi, (q, d, j) in enumerate(pairs):
        y = ws[pi][CHUNK:] + jnp.where(half[0], y_eo[pi][:CHUNK], y_eo[pi][CHUNK:])
        dir_refs[d][3][q, :, j * LANES:(j + 1) * LANES] = y.astype(BF16)
        g_end = dir_refs[d][2][q, 0:1, j * LANES:(j + 1) * LANES]
        s_ref[q, d, j] = s_p[pi] * g_end + jnp.where(diag, upd[pi], 0.0)


def _rw_scan(ops, bkt, v_bf, gend):
    _, bsz, seq, _ = ops.shape
    n_chunks = seq // CHUNK
    rev = lambda c: n_chunks - 1 - c
    return pl.pallas_call(
        _rw_scan_kernel,
        grid=(bsz // SCAN_BATCH, n_chunks),
        in_specs=[
            pl.BlockSpec((None, SCAN_BATCH, CHUNK, RW_OPS * RW_WIDTH), lambda b, c: (0, b, c, 0)),
            pl.BlockSpec((None, SCAN_BATCH, CHUNK, RW_OPS * RW_WIDTH), lambda b, c: (1, b, rev(c), 0)),
            pl.BlockSpec((None, SCAN_BATCH, None, RW_WIDTH // LANES, 2 * CHUNK, LANES),
                         lambda b, c: (0, b, c, 0, 0, 0)),
            pl.BlockSpec((None, SCAN_BATCH, None, RW_WIDTH // LANES, 2 * CHUNK, LANES),
                         lambda b, c: (1, b, rev(c), 0, 0, 0)),
            pl.BlockSpec((SCAN_BATCH, CHUNK, RW_WIDTH), lambda b, c: (b, c, 0)),
            pl.BlockSpec((SCAN_BATCH, CHUNK, RW_WIDTH), lambda b, c: (b, rev(c), 0)),
            pl.BlockSpec((None, SCAN_BATCH, None, HALO, RW_WIDTH), lambda b, c: (0, b, c, 0, 0)),
            pl.BlockSpec((None, SCAN_BATCH, None, HALO, RW_WIDTH), lambda b, c: (1, b, rev(c), 0, 0)),
        ],
        out_specs=[
            pl.BlockSpec((SCAN_BATCH, CHUNK, RW_WIDTH), lambda b, c: (b, c, 0)),
            pl.BlockSpec((SCAN_BATCH, CHUNK, RW_WIDTH), lambda b, c: (b, rev(c), 0)),
        ],
        out_shape=[jax.ShapeDtypeStruct((bsz, seq, RW_WIDTH), BF16)] * 2,
        scratch_shapes=[pltpu.VMEM((SCAN_BATCH, N_DIR, RW_WIDTH // LANES, LANES, LANES), jnp.float32)],
        compiler_params=pltpu.CompilerParams(
            dimension_semantics=("arbitrary", "arbitrary"), vmem_limit_bytes=VMEM_LIMIT),
        name="rwkv7_scan",
    )(ops, ops, bkt, bkt, v_bf, v_bf, gend, gend)


def _hg_kernel(qf_ref, ff_ref, if_ref, qb_ref, fb_ref, ib_ref, lbl_ref, tri_ref,
               of_ref, ob_ref, s_ref, cs_ref, vs_ref, acc_ref, *, layer):
    c = pl.program_id(1)

    @pl.when(c == 0)
    def _():
        s_ref[...] = jnp.zeros_like(s_ref)

    lg = lbl_ref[...]
    e = jnp.exp(lg - jnp.max(lg, axis=0, keepdims=True))
    lb = jnp.sum(e[0:layer + 1, :], axis=0, keepdims=True) / jnp.sum(e, axis=0, keepdims=True)

    heads = range(HG_HEADS)
    sls = [slice(h * HG_DK, (h + 1) * HG_DK) for h in heads]
    trow = lax.broadcasted_iota(jnp.int32, (SUB, HG_DK), 0)
    dir_refs = ((qf_ref, ff_ref, if_ref, of_ref), (qb_ref, fb_ref, ib_ref, ob_ref))
    probs = [(n, d) for n in range(SCAN_BATCH) for d in range(N_DIR)]

    def fast_path(k, out):
        n, d = probs[k]
        q_ref, f_ref, i_ref, o_ref = dir_refs[d]
        q, v = q_ref[n], i_ref[n]
        v_bf = v.astype(BF16)
        f = lb + (1.0 - lb) * _sigmoid(f_ref[n])
        kf = 1.0 - f
        yield
        b2 = _select_rows(tri_ref[d], jnp.log2(f), 2)
        yield
        (total,) = _chunk_totals(b2, d)
        q_in = (q * jnp.exp2(b2)).astype(BF16)
        k_end = (kf * jnp.exp2(total - b2)).astype(BF16)
        g_end = jnp.exp2(total)
        st = [s_ref[n, d, h] for h in heads]
        inter = [lax.dot_general(q_in[:, sls[h]], st[h].astype(BF16), _NT, preferred_element_type=jnp.float32)
                 for h in heads]
        for h in heads:
            s_ref[n, d, h] = st[h] * g_end[:, sls[h]] + lax.dot_general(
                v_bf[:, sls[h]], k_end[:, sls[h]], _TN, preferred_element_type=jnp.float32)
        yield
        mid = CHUNK // 2 - 1 if d == 0 else CHUNK // 2
        dev = b2 - b2[mid:mid + 1]
        q_c = (q * jnp.exp2(dev)).astype(BF16)
        k_c = (kf * jnp.exp2(-dev)).astype(BF16)
        scores = [lax.dot_general(q_c[:, sls[h]], k_c[:, sls[h]], _NT, preferred_element_type=jnp.float32)
                  for h in heads]
        out[k] = (q, kf, b2, v, v_bf, inter, jnp.max(jnp.abs(dev), axis=0, keepdims=True))
        yield
        tt = lax.broadcasted_iota(jnp.int32, (CHUNK, CHUNK), 0)
        ss = lax.broadcasted_iota(jnp.int32, (CHUNK, CHUNK), 1)
        seen = (tt >= ss) if d == 0 else (tt <= ss)
        for h in heads:
            o_ref[n, :, sls[h]] = (inter[h] + jnp.dot(jnp.where(seen, scores[h], 0.0).astype(BF16), v_bf[:, sls[h]],
                                                       preferred_element_type=jnp.float32)).astype(BF16)

    def safe_path(k, q, kf, b2, v, v_bf, inter):
        n, d = probs[k]
        o_ref = dir_refs[d][3]
        cs = b2 - jnp.log2(kf)
        for h in heads:
            cs_ref[h] = cs[:, sls[h]]
            vs_ref[h] = v[:, sls[h]]
        for lo in range(0, CHUNK, SUB):
            q_i, b_i = q[lo:lo + SUB], b2[lo:lo + SUB]
            acc = [inter[h][lo:lo + SUB] for h in heads]
            for s in range(SUB):
                row_s = pl.ds(lo + s, SUB, stride=0)
                m = (trow >= s) if d == 0 else (trow <= s)
                for h in heads:
                    xs = jnp.where(m, q_i[:, sls[h]] * jnp.exp2(b_i[:, sls[h]] - cs_ref[h, row_s, :]), 0.0)
                    w = jnp.sum(xs, axis=-1, keepdims=True)
                    acc[h] = acc[h] + w * vs_ref[h, row_s, :]
            for h in heads:
                acc_ref[lo:lo + SUB, sls[h]] = acc[h]

        def rows(p_lo, p_hi):
            lo = p_lo if d == 0 else CHUNK - p_hi
            return slice(lo, lo + (p_hi - p_lo))

        size = CHUNK // 2
        while size >= SUB:
            for p in range(0, CHUNK, 2 * size):
                s_sl, t_sl = rows(p, p + size), rows(p + size, p + 2 * size)
                ref = b2[rows(p + size - 1, p + size)]
                q_t = (q[t_sl] * jnp.exp2(b2[t_sl] - ref)).astype(BF16)
                k_s = (kf[s_sl] * jnp.exp2(ref - b2[s_sl])).astype(BF16)
                for h in heads:
                    sc = lax.dot_general(q_t[:, sls[h]], k_s[:, sls[h]], _NT, preferred_element_type=jnp.float32)
                    acc_ref[t_sl, sls[h]] += jnp.dot(sc.astype(BF16), v_bf[s_sl, sls[h]],
                                                     preferred_element_type=jnp.float32)
            size //= 2
        o_ref[n] = acc_ref[...].astype(BF16)

    ctxs = [None] * len(probs)
    for _ in itertools.zip_longest(*[fast_path(k, ctxs) for k in range(len(probs))]):
        pass
    worst = functools.reduce(jnp.maximum, [ctx[-1] for ctx in ctxs])

    @pl.when(jnp.max(worst) > SAFE_LOG2_DROP)
    def _():
        for k, ctx in enumerate(ctxs):
            safe_path(k, *ctx[:-1])


def _hg_mix(p_hg, lb_logits, layer, tri):
    bsz, seq, _ = p_hg.shape
    n_chunks = seq // CHUNK
    rev = lambda c: n_chunks - 1 - c
    blk = (SCAN_BATCH, CHUNK, HG_WIDTH)
    fwd = lambda j: pl.BlockSpec(blk, lambda g, c: (g, c, j))
    bwd = lambda j: pl.BlockSpec(blk, lambda g, c: (g, rev(c), j))
    full = lambda a: pl.BlockSpec(a.shape, lambda g, c: (0,) * a.ndim)
    return pl.pallas_call(
        functools.partial(_hg_kernel, layer=layer),
        grid=(bsz // SCAN_BATCH, n_chunks),
        in_specs=[fwd(0), fwd(1), fwd(3), bwd(0), bwd(2), bwd(3), full(lb_logits), full(tri)],
        out_specs=[fwd(0), bwd(0)],
        out_shape=[jax.ShapeDtypeStruct((bsz, seq, HG_WIDTH), BF16)] * 2,
        scratch_shapes=[pltpu.VMEM((SCAN_BATCH, N_DIR, HG_HEADS, HG_DK, HG_DK), jnp.float32),
                        pltpu.VMEM((HG_HEADS, CHUNK, HG_DK), jnp.float32),
                        pltpu.VMEM((HG_HEADS, CHUNK, HG_DK), jnp.float32),
                        pltpu.VMEM((CHUNK, HG_WIDTH), jnp.float32)],
        compiler_params=pltpu.CompilerParams(
            dimension_semantics=("arbitrary", "arbitrary"), vmem_limit_bytes=VMEM_LIMIT),
        name="hgrn2_mix",
    )(p_hg, p_hg, p_hg, p_hg, p_hg, p_hg, lb_logits, tri)


def _out_kernel(x_ref, yf_ref, yb_ref, bonus_ref, gate_ref, of_ref, ob_ref, hgate_ref,
                lnw_ref, lnb_ref, hgn_ref, seg_ref, w_ref, post_ref, out_ref):
    y = yf_ref[...].astype(jnp.float32) + yb_ref[...].astype(jnp.float32)
    seg = seg_ref[...]
    mu = _head_sums(y, seg) * (1.0 / RW_HEAD_DIM)
    yc = y - mu
    var = _head_sums(yc * yc, seg) * (1.0 / RW_HEAD_DIM)
    y = yc * lax.rsqrt(var + GN_EPS) * lnw_ref[...] + lnb_ref[...]
    rw = (y + bonus_ref[...].astype(jnp.float32)) * gate_ref[...].astype(jnp.float32)

    o = of_ref[...].astype(jnp.float32) + ob_ref[...].astype(jnp.float32)
    parts = []
    for h in range(HG_HEADS):
        oh = o[:, h * HG_DK:(h + 1) * HG_DK]
        parts.append(oh * lax.rsqrt(jnp.mean(oh * oh, axis=-1, keepdims=True) + NORM_EPS))
    hg = jnp.concatenate(parts, axis=1) * hgn_ref[...] * hgate_ref[...].astype(jnp.float32)

    cat = jnp.concatenate([rw, hg], axis=1).astype(BF16)
    yo = jnp.dot(cat, w_ref[...], preferred_element_type=jnp.float32)
    ms = jnp.mean(yo * yo, axis=-1, keepdims=True)
    out_ref[...] = x_ref[...] + yo * lax.rsqrt(ms + NORM_EPS) * post_ref[...]


def _out_proj(x2, y_f, y_b, bonus, gate, o_f, o_b, hg_gate, ln_w, ln_b, hg_norm, seg, w_bf16, post_g):
    rows = x2.shape[0]
    tile = lambda n: pl.BlockSpec((OUT_TILE, n), lambda i: (i, 0))
    row_spec = lambda n: pl.BlockSpec((1, n), lambda i: (0, 0))
    return pl.pallas_call(
        _out_kernel,
        grid=(rows // OUT_TILE,),
        in_specs=[
            tile(D_MODEL),
            tile(RW_WIDTH), tile(RW_WIDTH), tile(RW_WIDTH), tile(RW_WIDTH),
            tile(HG_WIDTH), tile(HG_WIDTH), tile(HG_WIDTH),
            row_spec(RW_WIDTH), row_spec(RW_WIDTH), row_spec(HG_WIDTH),
            pl.BlockSpec((MXU_TILE, MXU_TILE), lambda i: (0, 0)),
            pl.BlockSpec((D_MODEL, D_MODEL), lambda i: (0, 0)),
            row_spec(D_MODEL),
        ],
        out_specs=tile(D_MODEL),
        out_shape=jax.ShapeDtypeStruct((rows, D_MODEL), jnp.float32),
        compiler_params=pltpu.CompilerParams(
            dimension_semantics=("arbitrary",), vmem_limit_bytes=VMEM_LIMIT),
        name="out_proj",
    )(x2, y_f, y_b, bonus, gate, o_f, o_b, hg_gate, ln_w, ln_b, hg_norm, seg, w_bf16, post_g)


def _lora_weights(w):
    z = jnp.zeros_like(w[0])
    wp = jnp.stack([jnp.concatenate([w[0], z], axis=0), jnp.concatenate([z, w[1]], axis=0)])
    return wp.astype(BF16)


def _scan_order_selectors(tile):
    t = jnp.arange(tile)
    same = (t[:, None] // CHUNK) == (t[None, :] // CHUNK)
    fwd = same & (t[None, :] <= t[:, None])
    bwd = same & (t[None, :] >= t[:, None])
    return jnp.stack([fwd, bwd]).astype(BF16)


def kernel(x, pre_norm_g, w_in, rw_shift_prev, rw_shift_next, rw_w0, rw_w2, rw_a0, rw_a2, rw_k_k, rw_k_a,
           rw_r_k, rw_ln_w, rw_ln_b, hg_lb_logits, hg_norm_g, w_out, post_norm_g):
    bsz, seq, dm = x.shape
    depth = w_in.shape[0]
    rows = bsz * seq
    lane = jnp.arange(MXU_TILE) // RW_HEAD_DIM
    seg = (lane[:, None] == lane[None, :]).astype(BF16)
    tri = _scan_order_selectors(ROW_TILE)
    hg_tri = _scan_order_selectors(CHUNK)
    row = lambda t: t.reshape(1, -1)
    flat = lambda t: t.reshape(rows, t.shape[-1])
    for l in range(depth):
        x2 = x.reshape(rows, dm)
        p_hg, hg_gate, ops, bkt, v_bf, gend, bonus, gate = _in_rw(
            x, row(pre_norm_g[l]), w_in[l].astype(BF16),
            row(rw_shift_prev[l]), row(rw_shift_next[l]),
            rw_w0[l], _lora_weights(rw_w2[l]), rw_a0[l], _lora_weights(rw_a2[l]),
            row(rw_k_k[l]), row(rw_k_a[l]), row(rw_r_k[l]), seg, tri)
        y_f, y_b = _rw_scan(ops, bkt, v_bf, gend)
        o_f, o_b = _hg_mix(p_hg, hg_lb_logits, l, hg_tri)
        out = _out_proj(
            x2, flat(y_f), flat(y_b), flat(bonus), flat(gate), flat(o_f), flat(o_b), flat(hg_gate),
            row(rw_ln_w[l]), row(rw_ln_b[l]), row(jnp.tile(hg_norm_g[l], HG_HEADS)), seg,
            w_out[l].astype(BF16), row(post_norm_g[l]))
        x = out.reshape(bsz, seq, dm)
    return x
```

```python
import functools
import itertools

import jax
import jax.numpy as jnp
from jax import lax
from jax.experimental import pallas as pl
from jax.experimental.pallas import tpu as pltpu

D_MODEL = 1024
RW_HEAD_DIM = 64
RW_WIDTH = 512
HG_DK = 128
HG_WIDTH = 512
HG_HEADS = 4
LORA = 64
NORM_EPS = 1e-6
GN_EPS = 64e-5
RW_COLS = 4 * RW_WIDTH + 4 * LORA
HG_COLS = 5 * HG_WIDTH
HG_SCAN_COLS = 4 * HG_WIDTH
N_DIR = 2
RW_OPS = 4

LANES = 128
LHS_ROWS = 16
MXU_TILE = 256
CHUNK = 64
SUB = 8
SAFE_LOG2_DROP = 100.0
ROW_TILE = 256
OUT_TILE = 512
HALO = 8
SCAN_BATCH = 4
HG_LOCKSTEP = 4
VMEM_LIMIT = 56 * 1024 * 1024

_NT = (((1,), (1,)), ((), ()))
_TN = (((0,), (0,)), ((), ()))
BF16 = jnp.bfloat16
LOG2E = 1.4426950408889634


def _split_bf16(x, terms=2):
    pieces = []
    for _ in range(terms - 1):
        hi = x.astype(BF16)
        pieces.append(hi)
        x = x - hi.astype(jnp.float32)
    pieces.append(x.astype(BF16))
    return pieces


def _select_rows(sel, x, terms):
    acc = None
    for piece in _split_bf16(x, terms):
        t = jnp.dot(sel, piece, preferred_element_type=jnp.float32)
        acc = t if acc is None else acc + t
    return acc


def _head_sums(x, seg):
    xb = x.astype(BF16)
    return jnp.concatenate([jnp.dot(xb[:, c:c + MXU_TILE], seg, preferred_element_type=jnp.float32)
                            for c in range(0, x.shape[1], MXU_TILE)], axis=1)


def _sigmoid(x):
    return 1.0 / (1.0 + jnp.exp(-x))


class _PerPair:
    def __init__(self, load):
        self._load = load

    def __getitem__(self, pi):
        return self._load(pi)


def _chunk_totals(cum, d):
    last = CHUNK - 1 if d == 0 else 0
    return [cum[c * CHUNK + last:c * CHUNK + last + 1] for c in range(cum.shape[0] // CHUNK)]


def _expand_chunks(rows):
    return jnp.concatenate([jnp.broadcast_to(r, (CHUNK, r.shape[1])) for r in rows], axis=0)


def _in_rw_kernel(xc_ref, xp_ref, xn_ref, g_ref, w_ref,
                  mup_ref, mun_ref, w0_ref, w2_ref, a0_ref, a2_ref,
                  kk_ref, ka_ref, rk_ref, seg_ref, tri_ref,
                  phg_ref, hgate_ref, ops_ref, bkt_ref, v_ref, gend_ref, bonus_ref, gate_ref, *, n_tiles):
    j = pl.program_id(1)

    def pre_norm(x):
        return x * lax.rsqrt(jnp.mean(x * x, axis=-1, keepdims=True) + NORM_EPS) * g_ref[...]

    h_c = pre_norm(xc_ref[...])
    h_p = pre_norm(xp_ref[...]) * (j > 0).astype(jnp.float32)
    h_n = pre_norm(xn_ref[...]) * (j < n_tiles - 1).astype(jnp.float32)
    h = h_c.astype(BF16)
    h_ext = jnp.concatenate([h_p, h_c, h_n], axis=0).astype(BF16)

    hg_blocks = iter(range(HG_COLS // HG_WIDTH))

    def hg_block():
        i = next(hg_blocks, None)
        if i is not None:
            cols = slice(i * HG_WIDTH, (i + 1) * HG_WIDTH)
            blk = jnp.dot(h, w_ref[:, RW_COLS + cols.start:RW_COLS + cols.stop], preferred_element_type=jnp.float32)
            if i < HG_SCAN_COLS // HG_WIDTH:
                phg_ref[:, cols] = blk
            else:
                hgate_ref[...] = (blk * _sigmoid(blk)).astype(BF16)

    W = RW_WIDTH
    ext = ROW_TILE + 2 * HALO
    tile = slice(HALO, HALO + ROW_TILE)

    def shifted(cols):
        p_ext = jnp.dot(h_ext, w_ref[:, cols], preferred_element_type=jnp.float32)
        p, p_prev, p_next = p_ext[tile], pltpu.roll(p_ext, 1, 0)[tile], pltpu.roll(p_ext, ext - 1, 0)[tile]
        mup, mun = mup_ref[:, cols], mun_ref[:, cols]
        return p * (1.0 - mup - mun) + mup * p_prev + mun * p_next

    col = lambda i: slice(i * W, (i + 1) * W)
    lora_in = shifted(slice(4 * W, RW_COLS))
    tanh_wd = jnp.tanh(lora_in[:, :2 * LORA]).astype(BF16)
    ad = lora_in[:, 2 * LORA:].astype(BF16)
    k = shifted(col(1))
    w_raw = [w0_ref[d:d + 1, :] + jnp.dot(tanh_wd, w2_ref[d], preferred_element_type=jnp.float32)
             for d in range(N_DIR)]
    a_raw = [a0_ref[d:d + 1, :] + jnp.dot(ad, a2_ref[d], preferred_element_type=jnp.float32)
             for d in range(N_DIR)]
    r = shifted(col(0))

    seg = seg_ref[...]
    kk = k * kk_ref[...]
    kk = kk * lax.rsqrt(jnp.maximum(_head_sums(kk * kk, seg), 1e-24))
    lws = [(-LOG2E * jnp.exp(jnp.float32(-0.5))) * _sigmoid(w) for w in w_raw]
    v = shifted(col(2))
    cls = [_select_rows(tri_ref[d], lws[d], 2) for d in range(N_DIR)]
    v_ref[...] = v.astype(BF16)
    g = shifted(col(3))
    gate_ref[...] = (g * _sigmoid(g)).astype(BF16)
    hg_block()

    k_sum = None
    for d in range(N_DIR):
        lw, cl = lws[d], cls[d]
        a = _sigmoid(a_raw[d])
        k_d = k * (1.0 + (a - 1.0) * ka_ref[...])
        b_d = kk * a
        k_sum = k_d if k_sum is None else k_sum + k_d
        hg_block()
        g_end = [jnp.exp2(tot) for tot in _chunk_totals(cl, d)]
        e_out = jnp.exp2(-cl)
        e_end = e_out * _expand_chunks(g_end)
        ops = (r * jnp.exp2(cl),
               kk * jnp.exp2(cl - lw),
               k_d * e_end, b_d * e_end)
        for i, op in enumerate(ops):
            ops_ref[d, :, i * W:(i + 1) * W] = op.astype(BF16)
        b_t, k_t = b_d * e_out, k_d * e_out
        for q in range(ROW_TILE // CHUNK):
            rows = slice(q * CHUNK, (q + 1) * CHUNK)
            for t in range(W // LANES):
                cols = slice(t * LANES, (t + 1) * LANES)
                bk = jnp.concatenate([b_t[rows, cols], k_t[rows, cols]], axis=0)
                bkt_ref[d, q, t] = bk.astype(BF16).T
        for q, g_q in enumerate(g_end):
            gend_ref[d, q] = jnp.broadcast_to(g_q, (HALO, W))
        hg_block()

    bonus_ref[...] = (_head_sums(r * k_sum * rk_ref[...], seg) * v).astype(BF16)
    for _ in hg_blocks:
        raise AssertionError("HGRN2 column blocks left unissued")


def _in_rw(x, g, w_in, mup, mun, w0, w2, a0, a2, k_k, k_a, r_k, seg, tri):
    bsz, seq, _ = x.shape
    n_tiles = seq // ROW_TILE
    hb = ROW_TILE // HALO
    n_halo = seq // HALO
    n_chunks = seq // CHUNK
    cpt = ROW_TILE // CHUNK
    full = lambda a: pl.BlockSpec(a.shape, lambda b, j: (0,) * a.ndim)
    tile = lambda n: pl.BlockSpec((None, ROW_TILE, n), lambda b, j: (b, j, 0))
    consts = (g, w_in, mup, mun, w0, w2, a0, a2, k_k, k_a, r_k, seg, tri)
    return pl.pallas_call(
        functools.partial(_in_rw_kernel, n_tiles=n_tiles),
        grid=(bsz, n_tiles),
        in_specs=[
            tile(D_MODEL),
            pl.BlockSpec((None, HALO, D_MODEL), lambda b, j: (b, jnp.maximum(j * hb - 1, 0), 0)),
            pl.BlockSpec((None, HALO, D_MODEL), lambda b, j: (b, jnp.minimum((j + 1) * hb, n_halo - 1), 0)),
            *[full(a) for a in consts],
        ],
        out_specs=[
            tile(HG_SCAN_COLS),
            tile(HG_WIDTH),
            pl.BlockSpec((N_DIR, None, ROW_TILE, RW_OPS * RW_WIDTH), lambda b, j: (0, b, j, 0)),
            pl.BlockSpec((N_DIR, None, cpt, RW_WIDTH // LANES, 2 * CHUNK, LANES), lambda b, j: (0, b, j, 0, 0, 0)),
            tile(RW_WIDTH),
            pl.BlockSpec((N_DIR, None, cpt, HALO, RW_WIDTH), lambda b, j: (0, b, j, 0, 0)),
            tile(RW_WIDTH),
            tile(RW_WIDTH),
        ],
        out_shape=[
            jax.ShapeDtypeStruct((bsz, seq, HG_SCAN_COLS), jnp.float32),
            jax.ShapeDtypeStruct((bsz, seq, HG_WIDTH), BF16),
            jax.ShapeDtypeStruct((N_DIR, bsz, seq, RW_OPS * RW_WIDTH), BF16),
            jax.ShapeDtypeStruct((N_DIR, bsz, n_chunks, RW_WIDTH // LANES, 2 * CHUNK, LANES), BF16),
            jax.ShapeDtypeStruct((bsz, seq, RW_WIDTH), BF16),
            jax.ShapeDtypeStruct((N_DIR, bsz, n_chunks, HALO, RW_WIDTH), jnp.float32),
            jax.ShapeDtypeStruct((bsz, seq, RW_WIDTH), BF16),
            jax.ShapeDtypeStruct((bsz, seq, RW_WIDTH), BF16),
        ],
        compiler_params=pltpu.CompilerParams(
            dimension_semantics=("arbitrary", "arbitrary"), vmem_limit_bytes=VMEM_LIMIT),
        name="in_proj_rwkv7_prep",
    )(x, x, x, *consts)


def _rw_scan_kernel(of_ref, ob_ref, tf_ref, tb_ref, vf_ref, vb_ref, gf_ref, gb_ref, yf_ref, yb_ref, s_ref):
    c = pl.program_id(1)

    @pl.when(c == 0)
    def _():
        s_ref[...] = jnp.zeros_like(s_ref)

    N, W = RW_HEAD_DIM, RW_WIDTH
    lane = lax.broadcasted_iota(jnp.int32, (CHUNK, LANES), 1)
    tt = lax.broadcasted_iota(jnp.int32, (CHUNK, LANES), 0)
    half = (lane < N, lane >= N)
    diag = ((lax.broadcasted_iota(jnp.int32, (LANES, LANES), 0) < N)
            == (lax.broadcasted_iota(jnp.int32, (LANES, LANES), 1) < N))
    dir_refs = ((of_ref, vf_ref, gf_ref, yf_ref, tf_ref), (ob_ref, vb_ref, gb_ref, yb_ref, tb_ref))

    pairs = [(q, d, j) for q in range(SCAN_BATCH) for d in range(N_DIR) for j in range(W // LANES)]
    probs = [(pi, par) for pi in range(len(pairs)) for par in range(2)]

    def op(pi, i):
        q, d, j = pairs[pi]
        return dir_refs[d][0][q, :, i * W + j * LANES:i * W + (j + 1) * LANES]

    r_p, a_p, ke_p, be_p = [_PerPair(functools.partial(op, i=i)) for i in range(RW_OPS)]
    bkt_p = _PerPair(lambda pi: dir_refs[pairs[pi][1]][4][pairs[pi][0], pairs[pi][2]])
    v_p = _PerPair(lambda pi: dir_refs[pairs[pi][1]][1][pairs[pi][0], :,
                                                        pairs[pi][2] * LANES:(pairs[pi][2] + 1) * LANES])
    s_p = _PerPair(lambda pi: s_ref[pairs[pi]])

    def lag(pi):
        d = pairs[pi][1]
        delta = tt - (lane & (N - 1))
        return delta if d == 0 else -delta

    gram = []
    for pi, par in probs:
        lhs = jnp.concatenate([a_p[pi], r_p[pi]], axis=0)
        lhs = jnp.where(jnp.concatenate([half[par], half[par]], axis=0), lhs, jnp.zeros_like(lhs))
        gram.append(jnp.dot(lhs, bkt_p[pi], preferred_element_type=jnp.float32))

    za, l_ak, bot = [], [], []
    for n, (pi, par) in enumerate(probs):
        top = gram[n][:CHUNK]
        lg = lag(pi)
        m0 = jnp.where(jnp.where(half[0], lg, 0) > 0, -top, 0.0)
        l_ak.append(jnp.where(jnp.where(half[1], lg, 0) > 0, top, 0.0).astype(BF16))
        bot.append(jnp.where(lg >= 0, gram[n][CHUNK:], 0.0).astype(BF16))
        a_own = a_p[pi].astype(jnp.float32)
        za.append(jnp.where(half[1], a_own if par == 1 else pltpu.roll(a_own, N, 1), m0))
    zb = []
    for pi in range(len(pairs)):
        vv = jnp.concatenate([v_p[pi], v_p[pi]], axis=0)
        lv = jnp.dot(jnp.concatenate([l_ak[2 * pi], l_ak[2 * pi + 1]], axis=0), vv,
                     preferred_element_type=jnp.float32)
        zb += [lv[:CHUNK], lv[CHUNK:]]

    for level in range(6):
        skip = (2 ** level) // LHS_ROWS * LHS_ROWS
        new_za, new_zb = [], []
        for n, (pi, par) in enumerate(probs):
            live = slice(skip, CHUNK) if pairs[pi][1] == 0 else slice(0, CHUNK - skip)
            m = za[n][live, :N].astype(BF16)
            z2 = jnp.concatenate([za[n], zb[n]], axis=1).astype(BF16)
            p = jnp.dot(m, z2, preferred_element_type=jnp.float32)
            keep_a = jnp.where(half[1], za[n], 0.0)
            pieces_a = [keep_a[:live.start], p[:, :LANES] + keep_a[live], keep_a[live.stop:]]
            pieces_b = [zb[n][:live.start], p[:, LANES:] + zb[n][live], zb[n][live.stop:]]
            new_za.append(jnp.concatenate([x for x in pieces_a if x.shape[0]], axis=0))
            new_zb.append(jnp.concatenate([x for x in pieces_b if x.shape[0]], axis=0))
        za, zb = new_za, new_zb

    n_pairs = range(len(pairs))
    s_bf = [s_p[pi].astype(BF16) for pi in n_pairs]
    wr = [jnp.concatenate([jnp.where(half[0], pltpu.roll(za[2 * pi], N, 1), za[2 * pi + 1]).astype(BF16),
                           r_p[pi]], axis=0) for pi in n_pairs]
    ws = [lax.dot_general(wr[pi], s_bf[pi], _NT, preferred_element_type=jnp.float32) for pi in n_pairs]
    u_bf = [(-(ws[pi][:CHUNK] + jnp.where(half[0], zb[2 * pi], zb[2 * pi + 1]))).astype(BF16)
            for pi in n_pairs]
    uv = [jnp.concatenate([u_bf[pi], v_p[pi]], axis=0) for pi in n_pairs]
    y_eo = [jnp.dot(jnp.concatenate([bot[2 * pi], bot[2 * pi + 1]], axis=0), uv[pi],
                    preferred_element_type=jnp.float32) for pi in n_pairs]
    upd = [lax.dot_general(uv[pi], jnp.concatenate([be_p[pi], ke_p[pi]], axis=0), _TN,
                           preferred_element_type=jnp.float32) for pi in n_pairs]
    for pi, (q, d, j) in enumerate(pairs):
        y = ws[pi][CHUNK:] + jnp.where(half[0], y_eo[pi][:CHUNK], y_eo[pi][CHUNK:])
        dir_refs[d][3][q, :, j * LANES:(j + 1) * LANES] = y.astype(BF16)
        g_end = dir_refs[d][2][q, 0:1, j * LANES:(j + 1) * LANES]
        s_ref[q, d, j] = s_p[pi] * g_end + jnp.where(diag, upd[pi], 0.0)


def _rw_scan(ops, bkt, v_bf, gend):
    _, bsz, seq, _ = ops.shape
    n_chunks = seq // CHUNK
    rev = lambda c: n_chunks - 1 - c
    return pl.pallas_call(
        _rw_scan_kernel,
        grid=(bsz // SCAN_BATCH, n_chunks),
        in_specs=[
            pl.BlockSpec((None, SCAN_BATCH, CHUNK, RW_OPS * RW_WIDTH), lambda b, c: (0, b, c, 0)),
            pl.BlockSpec((None, SCAN_BATCH, CHUNK, RW_OPS * RW_WIDTH), lambda b, c: (1, b, rev(c), 0)),
            pl.BlockSpec((None, SCAN_BATCH, None, RW_WIDTH // LANES, 2 * CHUNK, LANES),
                         lambda b, c: (0, b, c, 0, 0, 0)),
            pl.BlockSpec((None, SCAN_BATCH, None, RW_WIDTH // LANES, 2 * CHUNK, LANES),
                         lambda b, c: (1, b, rev(c), 0, 0, 0)),
            pl.BlockSpec((SCAN_BATCH, CHUNK, RW_WIDTH), lambda b, c: (b, c, 0)),
            pl.BlockSpec((SCAN_BATCH, CHUNK, RW_WIDTH), lambda b, c: (b, rev(c), 0)),
            pl.BlockSpec((None, SCAN_BATCH, None, HALO, RW_WIDTH), lambda b, c: (0, b, c, 0, 0)),
            pl.BlockSpec((None, SCAN_BATCH, None, HALO, RW_WIDTH), lambda b, c: (1, b, rev(c), 0, 0)),
        ],
        out_specs=[
            pl.BlockSpec((SCAN_BATCH, CHUNK, RW_WIDTH), lambda b, c: (b, c, 0)),
            pl.BlockSpec((SCAN_BATCH, CHUNK, RW_WIDTH), lambda b, c: (b, rev(c), 0)),
        ],
        out_shape=[jax.ShapeDtypeStruct((bsz, seq, RW_WIDTH), BF16)] * 2,
        scratch_shapes=[pltpu.VMEM((SCAN_BATCH, N_DIR, RW_WIDTH // LANES, LANES, LANES), jnp.float32)],
        compiler_params=pltpu.CompilerParams(
            dimension_semantics=("arbitrary", "arbitrary"), vmem_limit_bytes=VMEM_LIMIT),
        name="rwkv7_scan",
    )(ops, ops, bkt, bkt, v_bf, v_bf, gend, gend)


def _hg_kernel(qf_ref, ff_ref, if_ref, qb_ref, fb_ref, ib_ref, lbl_ref, tri_ref,
               of_ref, ob_ref, s_ref, cs_ref, vs_ref, acc_ref, *, layer):
    c = pl.program_id(1)

    @pl.when(c == 0)
    def _():
        s_ref[...] = jnp.zeros_like(s_ref)

    lg = lbl_ref[...]
    e = jnp.exp(lg - jnp.max(lg, axis=0, keepdims=True))
    lb = jnp.sum(e[0:layer + 1, :], axis=0, keepdims=True) / jnp.sum(e, axis=0, keepdims=True)

    heads = range(HG_HEADS)
    sls = [slice(h * HG_DK, (h + 1) * HG_DK) for h in heads]
    trow = lax.broadcasted_iota(jnp.int32, (SUB, HG_DK), 0)
    dir_refs = ((qf_ref, ff_ref, if_ref, of_ref), (qb_ref, fb_ref, ib_ref, ob_ref))
    probs = [(n, d) for n in range(SCAN_BATCH) for d in range(N_DIR)]

    def fast_path(k, out):
        n, d = probs[k]
        q_ref, f_ref, i_ref, o_ref = dir_refs[d]
        q = lambda: q_ref[n]
        v_bf = lambda: i_ref[n].astype(BF16)
        f = lb + (1.0 - lb) * _sigmoid(f_ref[n])
        kf = 1.0 - f
        yield
        b2 = _select_rows(tri_ref[d], jnp.log2(f), 2)
        yield
        (total,) = _chunk_totals(b2, d)
        q_in = (q() * jnp.exp2(b2)).astype(BF16)
        k_end = (kf * jnp.exp2(total - b2)).astype(BF16)
        g_end = jnp.exp2(total)
        st = [s_ref[n, d, h] for h in heads]
        inter = [lax.dot_general(q_in[:, sls[h]], st[h].astype(BF16), _NT, preferred_element_type=jnp.float32)
                 for h in heads]
        v_c = v_bf()
        for h in heads:
            s_ref[n, d, h] = st[h] * g_end[:, sls[h]] + lax.dot_general(
                v_c[:, sls[h]], k_end[:, sls[h]], _TN, preferred_element_type=jnp.float32)
        yield
        mid = CHUNK // 2 - 1 if d == 0 else CHUNK // 2
        dev = b2 - b2[mid:mid + 1]
        q_c = (q() * jnp.exp2(dev)).astype(BF16)
        k_c = (kf * jnp.exp2(-dev)).astype(BF16)
        scores = [lax.dot_general(q_c[:, sls[h]], k_c[:, sls[h]], _NT, preferred_element_type=jnp.float32)
                  for h in heads]
        out[k] = (inter, jnp.max(jnp.abs(dev), axis=0, keepdims=True))
        yield
        tt = lax.broadcasted_iota(jnp.int32, (CHUNK, CHUNK), 0)
        ss = lax.broadcasted_iota(jnp.int32, (CHUNK, CHUNK), 1)
        seen = (tt >= ss) if d == 0 else (tt <= ss)
        v_c = v_bf()
        for h in heads:
            o_ref[n, :, sls[h]] = (inter[h] + jnp.dot(jnp.where(seen, scores[h], 0.0).astype(BF16), v_c[:, sls[h]],
                                                       preferred_element_type=jnp.float32)).astype(BF16)

    def safe_path(k, inter):
        n, d = probs[k]
        q_ref, f_ref, i_ref, o_ref = dir_refs[d]
        q, v = q_ref[n], i_ref[n]
        v_bf = v.astype(BF16)
        f = lb + (1.0 - lb) * _sigmoid(f_ref[n])
        kf = 1.0 - f
        b2 = _select_rows(tri_ref[d], jnp.log2(f), 2)
        cs = b2 - jnp.log2(kf)
        for h in heads:
            cs_ref[h] = cs[:, sls[h]]
            vs_ref[h] = v[:, sls[h]]
        for lo in range(0, CHUNK, SUB):
            q_i, b_i = q[lo:lo + SUB], b2[lo:lo + SUB]
            acc = [inter[h][lo:lo + SUB] for h in heads]
            for s in range(SUB):
                row_s = pl.ds(lo + s, SUB, stride=0)
                m = (trow >= s) if d == 0 else (trow <= s)
                for h in heads:
                    xs = jnp.where(m, q_i[:, sls[h]] * jnp.exp2(b_i[:, sls[h]] - cs_ref[h, row_s, :]), 0.0)
                    w = jnp.sum(xs, axis=-1, keepdims=True)
                    acc[h] = acc[h] + w * vs_ref[h, row_s, :]
            for h in heads:
                acc_ref[lo:lo + SUB, sls[h]] = acc[h]

        def rows(p_lo, p_hi):
            lo = p_lo if d == 0 else CHUNK - p_hi
            return slice(lo, lo + (p_hi - p_lo))

        size = CHUNK // 2
        while size >= SUB:
            for p in range(0, CHUNK, 2 * size):
                s_sl, t_sl = rows(p, p + size), rows(p + size, p + 2 * size)
                ref = b2[rows(p + size - 1, p + size)]
                q_t = (q[t_sl] * jnp.exp2(b2[t_sl] - ref)).astype(BF16)
                k_s = (kf[s_sl] * jnp.exp2(ref - b2[s_sl])).astype(BF16)
                for h in heads:
                    sc = lax.dot_general(q_t[:, sls[h]], k_s[:, sls[h]], _NT, preferred_element_type=jnp.float32)
                    acc_ref[t_sl, sls[h]] += jnp.dot(sc.astype(BF16), v_bf[s_sl, sls[h]],
                                                     preferred_element_type=jnp.float32)
            size //= 2
        o_ref[n] = acc_ref[...].astype(BF16)

    ctxs = [None] * len(probs)
    for k0 in range(0, len(probs), HG_LOCKSTEP):
        for _ in itertools.zip_longest(*[fast_path(k, ctxs) for k in range(k0, k0 + HG_LOCKSTEP)]):
            pass
    worst = functools.reduce(jnp.maximum, [ctx[-1] for ctx in ctxs])

    @pl.when(jnp.max(worst) > SAFE_LOG2_DROP)
    def _():
        for k, ctx in enumerate(ctxs):
            safe_path(k, *ctx[:-1])


def _hg_mix(p_hg, lb_logits, layer, tri):
    bsz, seq, _ = p_hg.shape
    n_chunks = seq // CHUNK
    rev = lambda c: n_chunks - 1 - c
    blk = (SCAN_BATCH, CHUNK, HG_WIDTH)
    fwd = lambda j: pl.BlockSpec(blk, lambda g, c: (g, c, j))
    bwd = lambda j: pl.BlockSpec(blk, lambda g, c: (g, rev(c), j))
    full = lambda a: pl.BlockSpec(a.shape, lambda g, c: (0,) * a.ndim)
    return pl.pallas_call(
        functools.partial(_hg_kernel, layer=layer),
        grid=(bsz // SCAN_BATCH, n_chunks),
        in_specs=[fwd(0), fwd(1), fwd(3), bwd(0), bwd(2), bwd(3), full(lb_logits), full(tri)],
        out_specs=[fwd(0), bwd(0)],
        out_shape=[jax.ShapeDtypeStruct((bsz, seq, HG_WIDTH), BF16)] * 2,
        scratch_shapes=[pltpu.VMEM((SCAN_BATCH, N_DIR, HG_HEADS, HG_DK, HG_DK), jnp.float32),
                        pltpu.VMEM((HG_HEADS, CHUNK, HG_DK), jnp.float32),
                        pltpu.VMEM((HG_HEADS, CHUNK, HG_DK), jnp.float32),
                        pltpu.VMEM((CHUNK, HG_WIDTH), jnp.float32)],
        compiler_params=pltpu.CompilerParams(
            dimension_semantics=("arbitrary", "arbitrary"), vmem_limit_bytes=VMEM_LIMIT),
        name="hgrn2_mix",
    )(p_hg, p_hg, p_hg, p_hg, p_hg, p_hg, lb_logits, tri)


def _out_kernel(x_ref, yf_ref, yb_ref, bonus_ref, gate_ref, of_ref, ob_ref, hgate_ref,
                lnw_ref, lnb_ref, hgn_ref, seg_ref, w_ref, post_ref, out_ref):
    y = yf_ref[...].astype(jnp.float32) + yb_ref[...].astype(jnp.float32)
    seg = seg_ref[...]
    mu = _head_sums(y, seg) * (1.0 / RW_HEAD_DIM)
    yc = y - mu
    var = _head_sums(yc * yc, seg) * (1.0 / RW_HEAD_DIM)
    y = yc * lax.rsqrt(var + GN_EPS) * lnw_ref[...] + lnb_ref[...]
    rw = (y + bonus_ref[...].astype(jnp.float32)) * gate_ref[...].astype(jnp.float32)

    o = of_ref[...].astype(jnp.float32) + ob_ref[...].astype(jnp.float32)
    parts = []
    for h in range(HG_HEADS):
        oh = o[:, h * HG_DK:(h + 1) * HG_DK]
        parts.append(oh * lax.rsqrt(jnp.mean(oh * oh, axis=-1, keepdims=True) + NORM_EPS))
    hg = jnp.concatenate(parts, axis=1) * hgn_ref[...] * hgate_ref[...].astype(jnp.float32)

    cat = jnp.concatenate([rw, hg], axis=1).astype(BF16)
    yo = jnp.dot(cat, w_ref[...], preferred_element_type=jnp.float32)
    ms = jnp.mean(yo * yo, axis=-1, keepdims=True)
    out_ref[...] = x_ref[...] + yo * lax.rsqrt(ms + NORM_EPS) * post_ref[...]


def _out_proj(x2, y_f, y_b, bonus, gate, o_f, o_b, hg_gate, ln_w, ln_b, hg_norm, seg, w_bf16, post_g):
    rows = x2.shape[0]
    tile = lambda n: pl.BlockSpec((OUT_TILE, n), lambda i: (i, 0))
    row_spec = lambda n: pl.BlockSpec((1, n), lambda i: (0, 0))
    return pl.pallas_call(
        _out_kernel,
        grid=(rows // OUT_TILE,),
        in_specs=[
            tile(D_MODEL),
            tile(RW_WIDTH), tile(RW_WIDTH), tile(RW_WIDTH), tile(RW_WIDTH),
            tile(HG_WIDTH), tile(HG_WIDTH), tile(HG_WIDTH),
            row_spec(RW_WIDTH), row_spec(RW_WIDTH), row_spec(HG_WIDTH),
            pl.BlockSpec((MXU_TILE, MXU_TILE), lambda i: (0, 0)),
            pl.BlockSpec((D_MODEL, D_MODEL), lambda i: (0, 0)),
            row_spec(D_MODEL),
        ],
        out_specs=tile(D_MODEL),
        out_shape=jax.ShapeDtypeStruct((rows, D_MODEL), jnp.float32),
        compiler_params=pltpu.CompilerParams(
            dimension_semantics=("arbitrary",), vmem_limit_bytes=VMEM_LIMIT),
        name="out_proj",
    )(x2, y_f, y_b, bonus, gate, o_f, o_b, hg_gate, ln_w, ln_b, hg_norm, seg, w_bf16, post_g)


def _lora_weights(w):
    z = jnp.zeros_like(w[0])
    wp = jnp.stack([jnp.concatenate([w[0], z], axis=0), jnp.concatenate([z, w[1]], axis=0)])
    return wp.astype(BF16)


def _scan_order_selectors(tile):
    t = jnp.arange(tile)
    same = (t[:, None] // CHUNK) == (t[None, :] // CHUNK)
    fwd = same & (t[None, :] <= t[:, None])
    bwd = same & (t[None, :] >= t[:, None])
    return jnp.stack([fwd, bwd]).astype(BF16)


def kernel(x, pre_norm_g, w_in, rw_shift_prev, rw_shift_next, rw_w0, rw_w2, rw_a0, rw_a2, rw_k_k, rw_k_a,
           rw_r_k, rw_ln_w, rw_ln_b, hg_lb_logits, hg_norm_g, w_out, post_norm_g):
    bsz, seq, dm = x.shape
    depth = w_in.shape[0]
    rows = bsz * seq
    lane = jnp.arange(MXU_TILE) // RW_HEAD_DIM
    seg = (lane[:, None] == lane[None, :]).astype(BF16)
    tri = _scan_order_selectors(ROW_TILE)
    hg_tri = _scan_order_selectors(CHUNK)
    row = lambda t: t.reshape(1, -1)
    flat = lambda t: t.reshape(rows, t.shape[-1])
    for l in range(depth):
        x2 = x.reshape(rows, dm)
        p_hg, hg_gate, ops, bkt, v_bf, gend, bonus, gate = _in_rw(
            x, row(pre_norm_g[l]), w_in[l].astype(BF16),
            row(rw_shift_prev[l]), row(rw_shift_next[l]),
            rw_w0[l], _lora_weights(rw_w2[l]), rw_a0[l], _lora_weights(rw_a2[l]),
            row(rw_k_k[l]), row(rw_k_a[l]), row(rw_r_k[l]), seg, tri)
        y_f, y_b = _rw_scan(ops, bkt, v_bf, gend)
        o_f, o_b = _hg_mix(p_hg, hg_lb_logits, l, hg_tri)
        out = _out_proj(
            x2, flat(y_f), flat(y_b), flat(bonus), flat(gate), flat(o_f), flat(o_b), flat(hg_gate),
            row(rw_ln_w[l]), row(rw_ln_b[l]), row(jnp.tile(hg_norm_g[l], HG_HEADS)), seg,
            w_out[l].astype(BF16), row(post_norm_g[l]))
        x = out.reshape(bsz, seq, dm)
    return x
```

```python
import functools
import itertools

import jax
import jax.numpy as jnp
from jax import lax
from jax.experimental import pallas as pl
from jax.experimental.pallas import tpu as pltpu

D_MODEL = 1024
RW_HEAD_DIM = 64
RW_WIDTH = 512
HG_DK = 128
HG_WIDTH = 512
HG_HEADS = 4
LORA = 64
NORM_EPS = 1e-6
GN_EPS = 64e-5
RW_COLS = 4 * RW_WIDTH + 4 * LORA
HG_COLS = 5 * HG_WIDTH
HG_SCAN_COLS = 4 * HG_WIDTH
N_DIR = 2
RW_OPS = 4

LANES = 128
LHS_ROWS = 16
MXU_TILE = 256
CHUNK = 64
SUB = 8
SAFE_LOG2_DROP = 100.0
ROW_TILE = 256
OUT_TILE = 1024
HALO = 8
SCAN_BATCH = 4
HG_LOCKSTEP = 4
VMEM_LIMIT = 56 * 1024 * 1024

_NT = (((1,), (1,)), ((), ()))
_TN = (((0,), (0,)), ((), ()))
BF16 = jnp.bfloat16
LOG2E = 1.4426950408889634


def _split_bf16(x, terms=2):
    pieces = []
    for _ in range(terms - 1):
        hi = x.astype(BF16)
        pieces.append(hi)
        x = x - hi.astype(jnp.float32)
    pieces.append(x.astype(BF16))
    return pieces


def _select_rows(sel, x, terms):
    acc = None
    for piece in _split_bf16(x, terms):
        t = jnp.dot(sel, piece, preferred_element_type=jnp.float32)
        acc = t if acc is None else acc + t
    return acc


def _head_sums(x, seg):
    xb = x.astype(BF16)
    return jnp.concatenate([jnp.dot(xb[:, c:c + MXU_TILE], seg, preferred_element_type=jnp.float32)
                            for c in range(0, x.shape[1], MXU_TILE)], axis=1)


def _sigmoid(x):
    return 1.0 / (1.0 + jnp.exp(-x))


class _PerPair:
    def __init__(self, load):
        self._load = load

    def __getitem__(self, pi):
        return self._load(pi)


def _chunk_totals(cum, d):
    last = CHUNK - 1 if d == 0 else 0
    return [cum[c * CHUNK + last:c * CHUNK + last + 1] for c in range(cum.shape[0] // CHUNK)]


def _expand_chunks(rows):
    return jnp.concatenate([jnp.broadcast_to(r, (CHUNK, r.shape[1])) for r in rows], axis=0)


def _in_rw_kernel(xc_ref, xp_ref, xn_ref, g_ref, w_ref,
                  mup_ref, mun_ref, w0_ref, w2_ref, a0_ref, a2_ref,
                  kk_ref, ka_ref, rk_ref, seg_ref, tri_ref,
                  phg_ref, hgate_ref, ops_ref, bkt_ref, v_ref, gend_ref, bonus_ref, gate_ref, *, n_tiles):
    j = pl.program_id(1)

    def pre_norm(x):
        return x * lax.rsqrt(jnp.mean(x * x, axis=-1, keepdims=True) + NORM_EPS) * g_ref[...]

    h_c = pre_norm(xc_ref[...])
    h_p = pre_norm(xp_ref[...]) * (j > 0).astype(jnp.float32)
    h_n = pre_norm(xn_ref[...]) * (j < n_tiles - 1).astype(jnp.float32)
    h = h_c.astype(BF16)
    h_ext = jnp.concatenate([h_p, h_c, h_n], axis=0).astype(BF16)

    hg_blocks = iter(range(HG_COLS // HG_WIDTH))

    def hg_block():
        i = next(hg_blocks, None)
        if i is not None:
            cols = slice(i * HG_WIDTH, (i + 1) * HG_WIDTH)
            blk = jnp.dot(h, w_ref[:, RW_COLS + cols.start:RW_COLS + cols.stop], preferred_element_type=jnp.float32)
            if i < HG_SCAN_COLS // HG_WIDTH:
                phg_ref[:, cols] = blk
            else:
                hgate_ref[...] = (blk * _sigmoid(blk)).astype(BF16)

    W = RW_WIDTH
    ext = ROW_TILE + 2 * HALO
    tile = slice(HALO, HALO + ROW_TILE)

    def shifted(cols):
        p_ext = jnp.dot(h_ext, w_ref[:, cols], preferred_element_type=jnp.float32)
        p, p_prev, p_next = p_ext[tile], pltpu.roll(p_ext, 1, 0)[tile], pltpu.roll(p_ext, ext - 1, 0)[tile]
        mup, mun = mup_ref[:, cols], mun_ref[:, cols]
        return p * (1.0 - mup - mun) + mup * p_prev + mun * p_next

    col = lambda i: slice(i * W, (i + 1) * W)
    lora_in = shifted(slice(4 * W, RW_COLS))
    tanh_wd = jnp.tanh(lora_in[:, :2 * LORA]).astype(BF16)
    ad = lora_in[:, 2 * LORA:].astype(BF16)
    k = shifted(col(1))
    w_raw = [w0_ref[d:d + 1, :] + jnp.dot(tanh_wd, w2_ref[d], preferred_element_type=jnp.float32)
             for d in range(N_DIR)]
    a_raw = [a0_ref[d:d + 1, :] + jnp.dot(ad, a2_ref[d], preferred_element_type=jnp.float32)
             for d in range(N_DIR)]
    r = shifted(col(0))

    seg = seg_ref[...]
    kk = k * kk_ref[...]
    kk = kk * lax.rsqrt(jnp.maximum(_head_sums(kk * kk, seg), 1e-24))
    lws = [(-LOG2E * jnp.exp(jnp.float32(-0.5))) * _sigmoid(w) for w in w_raw]
    v = shifted(col(2))
    cls = [_select_rows(tri_ref[d], lws[d], 2) for d in range(N_DIR)]
    v_ref[...] = v.astype(BF16)
    g = shifted(col(3))
    gate_ref[...] = (g * _sigmoid(g)).astype(BF16)
    hg_block()

    k_sum = None
    for d in range(N_DIR):
        lw, cl = lws[d], cls[d]
        a = _sigmoid(a_raw[d])
        k_d = k * (1.0 + (a - 1.0) * ka_ref[...])
        b_d = kk * a
        k_sum = k_d if k_sum is None else k_sum + k_d
        hg_block()
        g_end = [jnp.exp2(tot) for tot in _chunk_totals(cl, d)]
        e_out = jnp.exp2(-cl)
        e_end = e_out * _expand_chunks(g_end)
        ops = (r * jnp.exp2(cl),
               kk * jnp.exp2(cl - lw),
               k_d * e_end, b_d * e_end)
        for i, op in enumerate(ops):
            ops_ref[d, :, i * W:(i + 1) * W] = op.astype(BF16)
        b_t, k_t = b_d * e_out, k_d * e_out
        for q in range(ROW_TILE // CHUNK):
            rows = slice(q * CHUNK, (q + 1) * CHUNK)
            for t in range(W // LANES):
                cols = slice(t * LANES, (t + 1) * LANES)
                bk = jnp.concatenate([b_t[rows, cols], k_t[rows, cols]], axis=0)
                bkt_ref[d, q, t] = bk.astype(BF16).T
        for q, g_q in enumerate(g_end):
            gend_ref[d, q] = jnp.broadcast_to(g_q, (HALO, W))
        hg_block()

    bonus_ref[...] = (_head_sums(r * k_sum * rk_ref[...], seg) * v).astype(BF16)
    for _ in hg_blocks:
        raise AssertionError("HGRN2 column blocks left unissued")


def _in_rw(x, g, w_in, mup, mun, w0, w2, a0, a2, k_k, k_a, r_k, seg, tri):
    bsz, seq, _ = x.shape
    n_tiles = seq // ROW_TILE
    hb = ROW_TILE // HALO
    n_halo = seq // HALO
    n_chunks = seq // CHUNK
    cpt = ROW_TILE // CHUNK
    full = lambda a: pl.BlockSpec(a.shape, lambda b, j: (0,) * a.ndim)
    tile = lambda n: pl.BlockSpec((None, ROW_TILE, n), lambda b, j: (b, j, 0))
    consts = (g, w_in, mup, mun, w0, w2, a0, a2, k_k, k_a, r_k, seg, tri)
    return pl.pallas_call(
        functools.partial(_in_rw_kernel, n_tiles=n_tiles),
        grid=(bsz, n_tiles),
        in_specs=[
            tile(D_MODEL),
            pl.BlockSpec((None, HALO, D_MODEL), lambda b, j: (b, jnp.maximum(j * hb - 1, 0), 0)),
            pl.BlockSpec((None, HALO, D_MODEL), lambda b, j: (b, jnp.minimum((j + 1) * hb, n_halo - 1), 0)),
            *[full(a) for a in consts],
        ],
        out_specs=[
            tile(HG_SCAN_COLS),
            tile(HG_WIDTH),
            pl.BlockSpec((N_DIR, None, ROW_TILE, RW_OPS * RW_WIDTH), lambda b, j: (0, b, j, 0)),
            pl.BlockSpec((N_DIR, None, cpt, RW_WIDTH // LANES, 2 * CHUNK, LANES), lambda b, j: (0, b, j, 0, 0, 0)),
            tile(RW_WIDTH),
            pl.BlockSpec((N_DIR, None, cpt, HALO, RW_WIDTH), lambda b, j: (0, b, j, 0, 0)),
            tile(RW_WIDTH),
            tile(RW_WIDTH),
        ],
        out_shape=[
            jax.ShapeDtypeStruct((bsz, seq, HG_SCAN_COLS), jnp.float32),
            jax.ShapeDtypeStruct((bsz, seq, HG_WIDTH), BF16),
            jax.ShapeDtypeStruct((N_DIR, bsz, seq, RW_OPS * RW_WIDTH), BF16),
            jax.ShapeDtypeStruct((N_DIR, bsz, n_chunks, RW_WIDTH // LANES, 2 * CHUNK, LANES), BF16),
            jax.ShapeDtypeStruct((bsz, seq, RW_WIDTH), BF16),
            jax.ShapeDtypeStruct((N_DIR, bsz, n_chunks, HALO, RW_WIDTH), jnp.float32),
            jax.ShapeDtypeStruct((bsz, seq, RW_WIDTH), BF16),
            jax.ShapeDtypeStruct((bsz, seq, RW_WIDTH), BF16),
        ],
        compiler_params=pltpu.CompilerParams(
            dimension_semantics=("arbitrary", "arbitrary"), vmem_limit_bytes=VMEM_LIMIT),
        name="in_proj_rwkv7_prep",
    )(x, x, x, *consts)


def _rw_scan_kernel(of_ref, ob_ref, tf_ref, tb_ref, vf_ref, vb_ref, gf_ref, gb_ref, yf_ref, yb_ref, s_ref):
    c = pl.program_id(1)

    @pl.when(c == 0)
    def _():
        s_ref[...] = jnp.zeros_like(s_ref)

    N, W = RW_HEAD_DIM, RW_WIDTH
    lane = lax.broadcasted_iota(jnp.int32, (CHUNK, LANES), 1)
    tt = lax.broadcasted_iota(jnp.int32, (CHUNK, LANES), 0)
    half = (lane < N, lane >= N)
    diag = ((lax.broadcasted_iota(jnp.int32, (LANES, LANES), 0) < N)
            == (lax.broadcasted_iota(jnp.int32, (LANES, LANES), 1) < N))
    dir_refs = ((of_ref, vf_ref, gf_ref, yf_ref, tf_ref), (ob_ref, vb_ref, gb_ref, yb_ref, tb_ref))

    pairs = [(q, d, j) for q in range(SCAN_BATCH) for d in range(N_DIR) for j in range(W // LANES)]
    probs = [(pi, par) for pi in range(len(pairs)) for par in range(2)]

    def op(pi, i):
        q, d, j = pairs[pi]
        return dir_refs[d][0][q, :, i * W + j * LANES:i * W + (j + 1) * LANES]

    r_p, a_p, ke_p, be_p = [_PerPair(functools.partial(op, i=i)) for i in range(RW_OPS)]
    bkt_p = _PerPair(lambda pi: dir_refs[pairs[pi][1]][4][pairs[pi][0], pairs[pi][2]])
    v_p = _PerPair(lambda pi: dir_refs[pairs[pi][1]][1][pairs[pi][0], :,
                                                        pairs[pi][2] * LANES:(pairs[pi][2] + 1) * LANES])
    s_p = _PerPair(lambda pi: s_ref[pairs[pi]])

    def lag(pi):
        d = pairs[pi][1]
        delta = tt - (lane & (N - 1))
        return delta if d == 0 else -delta

    gram = []
    for pi, par in probs:
        lhs = jnp.concatenate([a_p[pi], r_p[pi]], axis=0)
        lhs = jnp.where(jnp.concatenate([half[par], half[par]], axis=0), lhs, jnp.zeros_like(lhs))
        gram.append(jnp.dot(lhs, bkt_p[pi], preferred_element_type=jnp.float32))

    za, l_ak, bot = [], [], []
    for n, (pi, par) in enumerate(probs):
        top = gram[n][:CHUNK]
        lg = lag(pi)
        m0 = jnp.where(jnp.where(half[0], lg, 0) > 0, -top, 0.0)
        l_ak.append(jnp.where(jnp.where(half[1], lg, 0) > 0, top, 0.0).astype(BF16))
        bot.append(jnp.where(lg >= 0, gram[n][CHUNK:], 0.0).astype(BF16))
        a_own = a_p[pi].astype(jnp.float32)
        za.append(jnp.where(half[1], a_own if par == 1 else pltpu.roll(a_own, N, 1), m0))
    zb = []
    for pi in range(len(pairs)):
        vv = jnp.concatenate([v_p[pi], v_p[pi]], axis=0)
        lv = jnp.dot(jnp.concatenate([l_ak[2 * pi], l_ak[2 * pi + 1]], axis=0), vv,
                     preferred_element_type=jnp.float32)
        zb += [lv[:CHUNK], lv[CHUNK:]]

    for level in range(6):
        skip = (2 ** level) // LHS_ROWS * LHS_ROWS
        new_za, new_zb = [], []
        for n, (pi, par) in enumerate(probs):
            live = slice(skip, CHUNK) if pairs[pi][1] == 0 else slice(0, CHUNK - skip)
            m = za[n][live, :N].astype(BF16)
            z2 = jnp.concatenate([za[n], zb[n]], axis=1).astype(BF16)
            p = jnp.dot(m, z2, preferred_element_type=jnp.float32)
            keep_a = jnp.where(half[1], za[n], 0.0)
            pieces_a = [keep_a[:live.start], p[:, :LANES] + keep_a[live], keep_a[live.stop:]]
            pieces_b = [zb[n][:live.start], p[:, LANES:] + zb[n][live], zb[n][live.stop:]]
            new_za.append(jnp.concatenate([x for x in pieces_a if x.shape[0]], axis=0))
            new_zb.append(jnp.concatenate([x for x in pieces_b if x.shape[0]], axis=0))
        za, zb = new_za, new_zb

    n_pairs = range(len(pairs))
    s_bf = [s_p[pi].astype(BF16) for pi in n_pairs]
    wr = [jnp.concatenate([jnp.where(half[0], pltpu.roll(za[2 * pi], N, 1), za[2 * pi + 1]).astype(BF16),
                           r_p[pi]], axis=0) for pi in n_pairs]
    ws = [lax.dot_general(wr[pi], s_bf[pi], _NT, preferred_element_type=jnp.float32) for pi in n_pairs]
    u_bf = [(-(ws[pi][:CHUNK] + jnp.where(half[0], zb[2 * pi], zb[2 * pi + 1]))).astype(BF16)
            for pi in n_pairs]
    uv = [jnp.concatenate([u_bf[pi], v_p[pi]], axis=0) for pi in n_pairs]
    y_eo = [jnp.dot(jnp.concatenate([bot[2 * pi], bot[2 * pi + 1]], axis=0), uv[pi],
                    preferred_element_type=jnp.float32) for pi in n_pairs]
    upd = [lax.dot_general(uv[pi], jnp.concatenate([be_p[pi], ke_p[pi]], axis=0), _TN,
                           preferred_element_type=jnp.float32) for pi in n_pairs]
    for pi, (q, d, j) in enumerate(pairs):
        y = ws[pi][CHUNK:] + jnp.where(half[0], y_eo[pi][:CHUNK], y_eo[pi][CHUNK:])
        dir_refs[d][3][q, :, j * LANES:(j + 1) * LANES] = y.astype(BF16)
        g_end = dir_refs[d][2][q, 0:1, j * LANES:(j + 1) * LANES]
        s_ref[q, d, j] = s_p[pi] * g_end + jnp.where(diag, upd[pi], 0.0)


def _rw_scan(ops, bkt, v_bf, gend):
    _, bsz, seq, _ = ops.shape
    n_chunks = seq // CHUNK
    rev = lambda c: n_chunks - 1 - c
    return pl.pallas_call(
        _rw_scan_kernel,
        grid=(bsz // SCAN_BATCH, n_chunks),
        in_specs=[
            pl.BlockSpec((None, SCAN_BATCH, CHUNK, RW_OPS * RW_WIDTH), lambda b, c: (0, b, c, 0)),
            pl.BlockSpec((None, SCAN_BATCH, CHUNK, RW_OPS * RW_WIDTH), lambda b, c: (1, b, rev(c), 0)),
            pl.BlockSpec((None, SCAN_BATCH, None, RW_WIDTH // LANES, 2 * CHUNK, LANES),
                         lambda b, c: (0, b, c, 0, 0, 0)),
            pl.BlockSpec((None, SCAN_BATCH, None, RW_WIDTH // LANES, 2 * CHUNK, LANES),
                         lambda b, c: (1, b, rev(c), 0, 0, 0)),
            pl.BlockSpec((SCAN_BATCH, CHUNK, RW_WIDTH), lambda b, c: (b, c, 0)),
            pl.BlockSpec((SCAN_BATCH, CHUNK, RW_WIDTH), lambda b, c: (b, rev(c), 0)),
            pl.BlockSpec((None, SCAN_BATCH, None, HALO, RW_WIDTH), lambda b, c: (0, b, c, 0, 0)),
            pl.BlockSpec((None, SCAN_BATCH, None, HALO, RW_WIDTH), lambda b, c: (1, b, rev(c), 0, 0)),
        ],
        out_specs=[
            pl.BlockSpec((SCAN_BATCH, CHUNK, RW_WIDTH), lambda b, c: (b, c, 0)),
            pl.BlockSpec((SCAN_BATCH, CHUNK, RW_WIDTH), lambda b, c: (b, rev(c), 0)),
        ],
        out_shape=[jax.ShapeDtypeStruct((bsz, seq, RW_WIDTH), BF16)] * 2,
        scratch_shapes=[pltpu.VMEM((SCAN_BATCH, N_DIR, RW_WIDTH // LANES, LANES, LANES), jnp.float32)],
        compiler_params=pltpu.CompilerParams(
            dimension_semantics=("arbitrary", "arbitrary"), vmem_limit_bytes=VMEM_LIMIT),
        name="rwkv7_scan",
    )(ops, ops, bkt, bkt, v_bf, v_bf, gend, gend)


def _hg_kernel(qf_ref, ff_ref, if_ref, qb_ref, fb_ref, ib_ref, lbl_ref, tri_ref,
               of_ref, ob_ref, s_ref, cs_ref, vs_ref, acc_ref, *, layer):
    c = pl.program_id(1)

    @pl.when(c == 0)
    def _():
        s_ref[...] = jnp.zeros_like(s_ref)

    lg = lbl_ref[...]
    e = jnp.exp(lg - jnp.max(lg, axis=0, keepdims=True))
    lb = jnp.sum(e[0:layer + 1, :], axis=0, keepdims=True) / jnp.sum(e, axis=0, keepdims=True)

    heads = range(HG_HEADS)
    sls = [slice(h * HG_DK, (h + 1) * HG_DK) for h in heads]
    trow = lax.broadcasted_iota(jnp.int32, (SUB, HG_DK), 0)
    dir_refs = ((qf_ref, ff_ref, if_ref, of_ref), (qb_ref, fb_ref, ib_ref, ob_ref))
    probs = [(n, d) for n in range(SCAN_BATCH) for d in range(N_DIR)]

    def fast_path(k, out):
        n, d = probs[k]
        q_ref, f_ref, i_ref, o_ref = dir_refs[d]
        q = lambda: q_ref[n]
        v_bf = lambda: i_ref[n].astype(BF16)
        f = lb + (1.0 - lb) * _sigmoid(f_ref[n])
        kf = 1.0 - f
        yield
        b2 = _select_rows(tri_ref[d], jnp.log2(f), 2)
        yield
        (total,) = _chunk_totals(b2, d)
        q_in = (q() * jnp.exp2(b2)).astype(BF16)
        k_end = (kf * jnp.exp2(total - b2)).astype(BF16)
        g_end = jnp.exp2(total)
        st = [s_ref[n, d, h] for h in heads]
        inter = [lax.dot_general(q_in[:, sls[h]], st[h].astype(BF16), _NT, preferred_element_type=jnp.float32)
                 for h in heads]
        v_c = v_bf()
        for h in heads:
            s_ref[n, d, h] = st[h] * g_end[:, sls[h]] + lax.dot_general(
                v_c[:, sls[h]], k_end[:, sls[h]], _TN, preferred_element_type=jnp.float32)
        yield
        mid = CHUNK // 2 - 1 if d == 0 else CHUNK // 2
        dev = b2 - b2[mid:mid + 1]
        q_c = (q() * jnp.exp2(dev)).astype(BF16)
        k_c = (kf * jnp.exp2(-dev)).astype(BF16)
        scores = [lax.dot_general(q_c[:, sls[h]], k_c[:, sls[h]], _NT, preferred_element_type=jnp.float32)
                  for h in heads]
        out[k] = (inter, jnp.max(jnp.abs(dev), axis=0, keepdims=True))
        yield
        tt = lax.broadcasted_iota(jnp.int32, (CHUNK, CHUNK), 0)
        ss = lax.broadcasted_iota(jnp.int32, (CHUNK, CHUNK), 1)
        seen = (tt >= ss) if d == 0 else (tt <= ss)
        v_c = v_bf()
        for h in heads:
            o_ref[n, :, sls[h]] = (inter[h] + jnp.dot(jnp.where(seen, scores[h], 0.0).astype(BF16), v_c[:, sls[h]],
                                                       preferred_element_type=jnp.float32)).astype(BF16)

    def safe_path(k, inter):
        n, d = probs[k]
        q_ref, f_ref, i_ref, o_ref = dir_refs[d]
        q, v = q_ref[n], i_ref[n]
        v_bf = v.astype(BF16)
        f = lb + (1.0 - lb) * _sigmoid(f_ref[n])
        kf = 1.0 - f
        b2 = _select_rows(tri_ref[d], jnp.log2(f), 2)
        cs = b2 - jnp.log2(kf)
        for h in heads:
            cs_ref[h] = cs[:, sls[h]]
            vs_ref[h] = v[:, sls[h]]
        for lo in range(0, CHUNK, SUB):
            q_i, b_i = q[lo:lo + SUB], b2[lo:lo + SUB]
            acc = [inter[h][lo:lo + SUB] for h in heads]
            for s in range(SUB):
                row_s = pl.ds(lo + s, SUB, stride=0)
                m = (trow >= s) if d == 0 else (trow <= s)
                for h in heads:
                    xs = jnp.where(m, q_i[:, sls[h]] * jnp.exp2(b_i[:, sls[h]] - cs_ref[h, row_s, :]), 0.0)
                    w = jnp.sum(xs, axis=-1, keepdims=True)
                    acc[h] = acc[h] + w * vs_ref[h, row_s, :]
            for h in heads:
                acc_ref[lo:lo + SUB, sls[h]] = acc[h]

        def rows(p_lo, p_hi):
            lo = p_lo if d == 0 else CHUNK - p_hi
            return slice(lo, lo + (p_hi - p_lo))

        size = CHUNK // 2
        while size >= SUB:
            for p in range(0, CHUNK, 2 * size):
                s_sl, t_sl = rows(p, p + size), rows(p + size, p + 2 * size)
                ref = b2[rows(p + size - 1, p + size)]
                q_t = (q[t_sl] * jnp.exp2(b2[t_sl] - ref)).astype(BF16)
                k_s = (kf[s_sl] * jnp.exp2(ref - b2[s_sl])).astype(BF16)
                for h in heads:
                    sc = lax.dot_general(q_t[:, sls[h]], k_s[:, sls[h]], _NT, preferred_element_type=jnp.float32)
                    acc_ref[t_sl, sls[h]] += jnp.dot(sc.astype(BF16), v_bf[s_sl, sls[h]],
                                                     preferred_element_type=jnp.float32)
            size //= 2
        o_ref[n] = acc_ref[...].astype(BF16)

    ctxs = [None] * len(probs)
    for k0 in range(0, len(probs), HG_LOCKSTEP):
        for _ in itertools.zip_longest(*[fast_path(k, ctxs) for k in range(k0, k0 + HG_LOCKSTEP)]):
            pass
    worst = functools.reduce(jnp.maximum, [ctx[-1] for ctx in ctxs])

    @pl.when(jnp.max(worst) > SAFE_LOG2_DROP)
    def _():
        for k, ctx in enumerate(ctxs):
            safe_path(k, *ctx[:-1])


def _hg_mix(p_hg, lb_logits, layer, tri):
    bsz, seq, _ = p_hg.shape
    n_chunks = seq // CHUNK
    rev = lambda c: n_chunks - 1 - c
    blk = (SCAN_BATCH, CHUNK, HG_WIDTH)
    fwd = lambda j: pl.BlockSpec(blk, lambda g, c: (g, c, j))
    bwd = lambda j: pl.BlockSpec(blk, lambda g, c: (g, rev(c), j))
    full = lambda a: pl.BlockSpec(a.shape, lambda g, c: (0,) * a.ndim)
    return pl.pallas_call(
        functools.partial(_hg_kernel, layer=layer),
        grid=(bsz // SCAN_BATCH, n_chunks),
        in_specs=[fwd(0), fwd(1), fwd(3), bwd(0), bwd(2), bwd(3), full(lb_logits), full(tri)],
        out_specs=[fwd(0), bwd(0)],
        out_shape=[jax.ShapeDtypeStruct((bsz, seq, HG_WIDTH), BF16)] * 2,
        scratch_shapes=[pltpu.VMEM((SCAN_BATCH, N_DIR, HG_HEADS, HG_DK, HG_DK), jnp.float32),
                        pltpu.VMEM((HG_HEADS, CHUNK, HG_DK), jnp.float32),
                        pltpu.VMEM((HG_HEADS, CHUNK, HG_DK), jnp.float32),
                        pltpu.VMEM((CHUNK, HG_WIDTH), jnp.float32)],
        compiler_params=pltpu.CompilerParams(
            dimension_semantics=("arbitrary", "arbitrary"), vmem_limit_bytes=VMEM_LIMIT),
        name="hgrn2_mix",
    )(p_hg, p_hg, p_hg, p_hg, p_hg, p_hg, lb_logits, tri)


def _out_kernel(x_ref, yf_ref, yb_ref, bonus_ref, gate_ref, of_ref, ob_ref, hgate_ref,
                lnw_ref, lnb_ref, hgn_ref, seg_ref, w_ref, post_ref, out_ref):
    y = yf_ref[...].astype(jnp.float32) + yb_ref[...].astype(jnp.float32)
    seg = seg_ref[...]
    mu = _head_sums(y, seg) * (1.0 / RW_HEAD_DIM)
    yc = y - mu
    var = _head_sums(yc * yc, seg) * (1.0 / RW_HEAD_DIM)
    y = yc * lax.rsqrt(var + GN_EPS) * lnw_ref[...] + lnb_ref[...]
    rw = (y + bonus_ref[...].astype(jnp.float32)) * gate_ref[...].astype(jnp.float32)

    o = of_ref[...].astype(jnp.float32) + ob_ref[...].astype(jnp.float32)
    parts = []
    for h in range(HG_HEADS):
        oh = o[:, h * HG_DK:(h + 1) * HG_DK]
        parts.append(oh * lax.rsqrt(jnp.mean(oh * oh, axis=-1, keepdims=True) + NORM_EPS))
    hg = jnp.concatenate(parts, axis=1) * hgn_ref[...] * hgate_ref[...].astype(jnp.float32)

    cat = jnp.concatenate([rw, hg], axis=1).astype(BF16)
    yo = jnp.dot(cat, w_ref[...], preferred_element_type=jnp.float32)
    ms = jnp.mean(yo * yo, axis=-1, keepdims=True)
    out_ref[...] = x_ref[...] + yo * lax.rsqrt(ms + NORM_EPS) * post_ref[...]


def _out_proj(x2, y_f, y_b, bonus, gate, o_f, o_b, hg_gate, ln_w, ln_b, hg_norm, seg, w_bf16, post_g):
    rows = x2.shape[0]
    tile = lambda n: pl.BlockSpec((OUT_TILE, n), lambda i: (i, 0))
    row_spec = lambda n: pl.BlockSpec((1, n), lambda i: (0, 0))
    return pl.pallas_call(
        _out_kernel,
        grid=(rows // OUT_TILE,),
        in_specs=[
            tile(D_MODEL),
            tile(RW_WIDTH), tile(RW_WIDTH), tile(RW_WIDTH), tile(RW_WIDTH),
            tile(HG_WIDTH), tile(HG_WIDTH), tile(HG_WIDTH),
            row_spec(RW_WIDTH), row_spec(RW_WIDTH), row_spec(HG_WIDTH),
            pl.BlockSpec((MXU_TILE, MXU_TILE), lambda i: (0, 0)),
            pl.BlockSpec((D_MODEL, D_MODEL), lambda i: (0, 0)),
            row_spec(D_MODEL),
        ],
        out_specs=tile(D_MODEL),
        out_shape=jax.ShapeDtypeStruct((rows, D_MODEL), jnp.float32),
        compiler_params=pltpu.CompilerParams(
            dimension_semantics=("arbitrary",), vmem_limit_bytes=VMEM_LIMIT),
        name="out_proj",
    )(x2, y_f, y_b, bonus, gate, o_f, o_b, hg_gate, ln_w, ln_b, hg_norm, seg, w_bf16, post_g)


def _lora_weights(w):
    z = jnp.zeros_like(w[0])
    wp = jnp.stack([jnp.concatenate([w[0], z], axis=0), jnp.concatenate([z, w[1]], axis=0)])
    return wp.astype(BF16)


def _scan_order_selectors(tile):
    t = jnp.arange(tile)
    same = (t[:, None] // CHUNK) == (t[None, :] // CHUNK)
    fwd = same & (t[None, :] <= t[:, None])
    bwd = same & (t[None, :] >= t[:, None])
    return jnp.stack([fwd, bwd]).astype(BF16)


def kernel(x, pre_norm_g, w_in, rw_shift_prev, rw_shift_next, rw_w0, rw_w2, rw_a0, rw_a2, rw_k_k, rw_k_a,
           rw_r_k, rw_ln_w, rw_ln_b, hg_lb_logits, hg_norm_g, w_out, post_norm_g):
    bsz, seq, dm = x.shape
    depth = w_in.shape[0]
    rows = bsz * seq
    lane = jnp.arange(MXU_TILE) // RW_HEAD_DIM
    seg = (lane[:, None] == lane[None, :]).astype(BF16)
    tri = _scan_order_selectors(ROW_TILE)
    hg_tri = _scan_order_selectors(CHUNK)
    row = lambda t: t.reshape(1, -1)
    flat = lambda t: t.reshape(rows, t.shape[-1])
    for l in range(depth):
        x2 = x.reshape(rows, dm)
        p_hg, hg_gate, ops, bkt, v_bf, gend, bonus, gate = _in_rw(
            x, row(pre_norm_g[l]), w_in[l].astype(BF16),
            row(rw_shift_prev[l]), row(rw_shift_next[l]),
            rw_w0[l], _lora_weights(rw_w2[l]), rw_a0[l], _lora_weights(rw_a2[l]),
            row(rw_k_k[l]), row(rw_k_a[l]), row(rw_r_k[l]), seg, tri)
        y_f, y_b = _rw_scan(ops, bkt, v_bf, gend)
        o_f, o_b = _hg_mix(p_hg, hg_lb_logits, l, hg_tri)
        out = _out_proj(
            x2, flat(y_f), flat(y_b), flat(bonus), flat(gate), flat(o_f), flat(o_b), flat(hg_gate),
            row(rw_ln_w[l]), row(rw_ln_b[l]), row(jnp.tile(hg_norm_g[l], HG_HEADS)), seg,
            w_out[l].astype(BF16), row(post_norm_g[l]))
        x = out.reshape(bsz, seq, dm)
    return x
```

```python
import functools
import itertools

import jax
import jax.numpy as jnp
from jax import lax
from jax.experimental import pallas as pl
from jax.experimental.pallas import tpu as pltpu

D_MODEL = 1024
RW_HEAD_DIM = 64
RW_WIDTH = 512
HG_DK = 128
HG_WIDTH = 512
HG_HEADS = 4
LORA = 64
NORM_EPS = 1e-6
GN_EPS = 64e-5
RW_COLS = 4 * RW_WIDTH + 4 * LORA
HG_COLS = 5 * HG_WIDTH
HG_SCAN_COLS = 4 * HG_WIDTH
N_DIR = 2
RW_OPS = 4

LANES = 128
LHS_ROWS = 16
MXU_TILE = 256
CHUNK = 64
SUB = 8
SAFE_LOG2_DROP = 100.0
ROW_TILE = 256
OUT_TILE = 1024
HALO = 8
SCAN_BATCH = 4
HG_LOCKSTEP = 4
VMEM_LIMIT = 56 * 1024 * 1024

_NT = (((1,), (1,)), ((), ()))
_TN = (((0,), (0,)), ((), ()))
BF16 = jnp.bfloat16
LOG2E = 1.4426950408889634


def _split_bf16(x, terms=2):
    pieces = []
    for _ in range(terms - 1):
        hi = x.astype(BF16)
        pieces.append(hi)
        x = x - hi.astype(jnp.float32)
    pieces.append(x.astype(BF16))
    return pieces


def _select_rows(sel, x, terms):
    acc = None
    for piece in _split_bf16(x, terms):
        t = jnp.dot(sel, piece, preferred_element_type=jnp.float32)
        acc = t if acc is None else acc + t
    return acc


def _head_sums(x, seg):
    xb = x.astype(BF16)
    return jnp.concatenate([jnp.dot(xb[:, c:c + MXU_TILE], seg, preferred_element_type=jnp.float32)
                            for c in range(0, x.shape[1], MXU_TILE)], axis=1)


def _sigmoid(x):
    return 1.0 / (1.0 + jnp.exp(-x))


def _gate_sigmoid(x):
    return 0.5 * jnp.tanh(0.5 * x) + 0.5


class _PerPair:
    def __init__(self, load):
        self._load = load

    def __getitem__(self, pi):
        return self._load(pi)


def _chunk_totals(cum, d):
    last = CHUNK - 1 if d == 0 else 0
    return [cum[c * CHUNK + last:c * CHUNK + last + 1] for c in range(cum.shape[0] // CHUNK)]


def _expand_chunks(rows):
    return jnp.concatenate([jnp.broadcast_to(r, (CHUNK, r.shape[1])) for r in rows], axis=0)


def _in_rw_kernel(xc_ref, xp_ref, xn_ref, g_ref, w_ref,
                  mup_ref, mun_ref, w0_ref, w2_ref, a0_ref, a2_ref,
                  kk_ref, ka_ref, rk_ref, seg_ref, tri_ref,
                  phg_ref, hgate_ref, ops_ref, bkt_ref, v_ref, gend_ref, bonus_ref, gate_ref, *, n_tiles):
    j = pl.program_id(1)

    def pre_norm(x):
        return x * lax.rsqrt(jnp.mean(x * x, axis=-1, keepdims=True) + NORM_EPS) * g_ref[...]

    h_c = pre_norm(xc_ref[...])
    h_p = pre_norm(xp_ref[...]) * (j > 0).astype(jnp.float32)
    h_n = pre_norm(xn_ref[...]) * (j < n_tiles - 1).astype(jnp.float32)
    h = h_c.astype(BF16)
    h_ext = jnp.concatenate([h_p, h_c, h_n], axis=0).astype(BF16)

    hg_blocks = iter(range(HG_COLS // HG_WIDTH))

    def hg_block():
        i = next(hg_blocks, None)
        if i is not None:
            cols = slice(i * HG_WIDTH, (i + 1) * HG_WIDTH)
            blk = jnp.dot(h, w_ref[:, RW_COLS + cols.start:RW_COLS + cols.stop], preferred_element_type=jnp.float32)
            if i < HG_SCAN_COLS // HG_WIDTH:
                phg_ref[:, cols] = blk
            else:
                hgate_ref[...] = (blk * _gate_sigmoid(blk)).astype(BF16)

    W = RW_WIDTH
    ext = ROW_TILE + 2 * HALO
    tile = slice(HALO, HALO + ROW_TILE)

    def shifted(cols):
        p_ext = jnp.dot(h_ext, w_ref[:, cols], preferred_element_type=jnp.float32)
        p, p_prev, p_next = p_ext[tile], pltpu.roll(p_ext, 1, 0)[tile], pltpu.roll(p_ext, ext - 1, 0)[tile]
        mup, mun = mup_ref[:, cols], mun_ref[:, cols]
        return p * (1.0 - mup - mun) + mup * p_prev + mun * p_next

    col = lambda i: slice(i * W, (i + 1) * W)
    lora_in = shifted(slice(4 * W, RW_COLS))
    tanh_wd = jnp.tanh(lora_in[:, :2 * LORA]).astype(BF16)
    ad = lora_in[:, 2 * LORA:].astype(BF16)
    k = shifted(col(1))
    w_raw = [w0_ref[d:d + 1, :] + jnp.dot(tanh_wd, w2_ref[d], preferred_element_type=jnp.float32)
             for d in range(N_DIR)]
    a_raw = [a0_ref[d:d + 1, :] + jnp.dot(ad, a2_ref[d], preferred_element_type=jnp.float32)
             for d in range(N_DIR)]
    r = shifted(col(0))

    seg = seg_ref[...]
    kk = k * kk_ref[...]
    kk = kk * lax.rsqrt(jnp.maximum(_head_sums(kk * kk, seg), 1e-24))
    lws = [(-LOG2E * jnp.exp(jnp.float32(-0.5))) * _gate_sigmoid(w) for w in w_raw]
    v = shifted(col(2))
    cls = [_select_rows(tri_ref[d], lws[d], 2) for d in range(N_DIR)]
    v_ref[...] = v.astype(BF16)
    g = shifted(col(3))
    gate_ref[...] = (g * _gate_sigmoid(g)).astype(BF16)
    hg_block()

    k_sum = None
    for d in range(N_DIR):
        lw, cl = lws[d], cls[d]
        a = _gate_sigmoid(a_raw[d])
        k_d = k * (1.0 + (a - 1.0) * ka_ref[...])
        b_d = kk * a
        k_sum = k_d if k_sum is None else k_sum + k_d
        hg_block()
        g_end = [jnp.exp2(tot) for tot in _chunk_totals(cl, d)]
        e_out = jnp.exp2(-cl)
        e_end = e_out * _expand_chunks(g_end)
        ops = (r * jnp.exp2(cl),
               kk * jnp.exp2(cl - lw),
               k_d * e_end, b_d * e_end)
        for i, op in enumerate(ops):
            ops_ref[d, :, i * W:(i + 1) * W] = op.astype(BF16)
        b_t, k_t = b_d * e_out, k_d * e_out
        for q in range(ROW_TILE // CHUNK):
            rows = slice(q * CHUNK, (q + 1) * CHUNK)
            for t in range(W // LANES):
                cols = slice(t * LANES, (t + 1) * LANES)
                bk = jnp.concatenate([b_t[rows, cols], k_t[rows, cols]], axis=0)
                bkt_ref[d, q, t] = bk.astype(BF16).T
        for q, g_q in enumerate(g_end):
            gend_ref[d, q] = jnp.broadcast_to(g_q, (HALO, W))
        hg_block()

    bonus_ref[...] = (_head_sums(r * k_sum * rk_ref[...], seg) * v).astype(BF16)
    for _ in hg_blocks:
        raise AssertionError("HGRN2 column blocks left unissued")


def _in_rw(x, g, w_in, mup, mun, w0, w2, a0, a2, k_k, k_a, r_k, seg, tri):
    bsz, seq, _ = x.shape
    n_tiles = seq // ROW_TILE
    hb = ROW_TILE // HALO
    n_halo = seq // HALO
    n_chunks = seq // CHUNK
    cpt = ROW_TILE // CHUNK
    full = lambda a: pl.BlockSpec(a.shape, lambda b, j: (0,) * a.ndim)
    tile = lambda n: pl.BlockSpec((None, ROW_TILE, n), lambda b, j: (b, j, 0))
    consts = (g, w_in, mup, mun, w0, w2, a0, a2, k_k, k_a, r_k, seg, tri)
    return pl.pallas_call(
        functools.partial(_in_rw_kernel, n_tiles=n_tiles),
        grid=(bsz, n_tiles),
        in_specs=[
            tile(D_MODEL),
            pl.BlockSpec((None, HALO, D_MODEL), lambda b, j: (b, jnp.maximum(j * hb - 1, 0), 0)),
            pl.BlockSpec((None, HALO, D_MODEL), lambda b, j: (b, jnp.minimum((j + 1) * hb, n_halo - 1), 0)),
            *[full(a) for a in consts],
        ],
        out_specs=[
            tile(HG_SCAN_COLS),
            tile(HG_WIDTH),
            pl.BlockSpec((N_DIR, None, ROW_TILE, RW_OPS * RW_WIDTH), lambda b, j: (0, b, j, 0)),
            pl.BlockSpec((N_DIR, None, cpt, RW_WIDTH // LANES, 2 * CHUNK, LANES), lambda b, j: (0, b, j, 0, 0, 0)),
            tile(RW_WIDTH),
            pl.BlockSpec((N_DIR, None, cpt, HALO, RW_WIDTH), lambda b, j: (0, b, j, 0, 0)),
            tile(RW_WIDTH),
            tile(RW_WIDTH),
        ],
        out_shape=[
            jax.ShapeDtypeStruct((bsz, seq, HG_SCAN_COLS), jnp.float32),
            jax.ShapeDtypeStruct((bsz, seq, HG_WIDTH), BF16),
            jax.ShapeDtypeStruct((N_DIR, bsz, seq, RW_OPS * RW_WIDTH), BF16),
            jax.ShapeDtypeStruct((N_DIR, bsz, n_chunks, RW_WIDTH // LANES, 2 * CHUNK, LANES), BF16),
            jax.ShapeDtypeStruct((bsz, seq, RW_WIDTH), BF16),
            jax.ShapeDtypeStruct((N_DIR, bsz, n_chunks, HALO, RW_WIDTH), jnp.float32),
            jax.ShapeDtypeStruct((bsz, seq, RW_WIDTH), BF16),
            jax.ShapeDtypeStruct((bsz, seq, RW_WIDTH), BF16),
        ],
        compiler_params=pltpu.CompilerParams(
            dimension_semantics=("arbitrary", "arbitrary"), vmem_limit_bytes=VMEM_LIMIT),
        name="in_proj_rwkv7_prep",
    )(x, x, x, *consts)


def _rw_scan_kernel(of_ref, ob_ref, tf_ref, tb_ref, vf_ref, vb_ref, gf_ref, gb_ref, yf_ref, yb_ref, s_ref):
    c = pl.program_id(1)

    @pl.when(c == 0)
    def _():
        s_ref[...] = jnp.zeros_like(s_ref)

    N, W = RW_HEAD_DIM, RW_WIDTH
    lane = lax.broadcasted_iota(jnp.int32, (CHUNK, LANES), 1)
    tt = lax.broadcasted_iota(jnp.int32, (CHUNK, LANES), 0)
    half = (lane < N, lane >= N)
    diag = ((lax.broadcasted_iota(jnp.int32, (LANES, LANES), 0) < N)
            == (lax.broadcasted_iota(jnp.int32, (LANES, LANES), 1) < N))
    dir_refs = ((of_ref, vf_ref, gf_ref, yf_ref, tf_ref), (ob_ref, vb_ref, gb_ref, yb_ref, tb_ref))

    pairs = [(q, d, j) for q in range(SCAN_BATCH) for d in range(N_DIR) for j in range(W // LANES)]
    probs = [(pi, par) for pi in range(len(pairs)) for par in range(2)]

    def op(pi, i):
        q, d, j = pairs[pi]
        return dir_refs[d][0][q, :, i * W + j * LANES:i * W + (j + 1) * LANES]

    r_p, a_p, ke_p, be_p = [_PerPair(functools.partial(op, i=i)) for i in range(RW_OPS)]
    bkt_p = _PerPair(lambda pi: dir_refs[pairs[pi][1]][4][pairs[pi][0], pairs[pi][2]])
    v_p = _PerPair(lambda pi: dir_refs[pairs[pi][1]][1][pairs[pi][0], :,
                                                        pairs[pi][2] * LANES:(pairs[pi][2] + 1) * LANES])
    s_p = _PerPair(lambda pi: s_ref[pairs[pi]])

    def lag(pi):
        d = pairs[pi][1]
        delta = tt - (lane & (N - 1))
        return delta if d == 0 else -delta

    gram = []
    for pi, par in probs:
        lhs = jnp.concatenate([a_p[pi], r_p[pi]], axis=0)
        lhs = jnp.where(jnp.concatenate([half[par], half[par]], axis=0), lhs, jnp.zeros_like(lhs))
        gram.append(jnp.dot(lhs, bkt_p[pi], preferred_element_type=jnp.float32))

    za, l_ak, bot = [], [], []
    for n, (pi, par) in enumerate(probs):
        top = gram[n][:CHUNK]
        lg = lag(pi)
        m0 = jnp.where(jnp.where(half[0], lg, 0) > 0, -top, 0.0)
        l_ak.append(jnp.where(jnp.where(half[1], lg, 0) > 0, top, 0.0).astype(BF16))
        bot.append(jnp.where(lg >= 0, gram[n][CHUNK:], 0.0).astype(BF16))
        a_own = a_p[pi].astype(jnp.float32)
        za.append(jnp.where(half[1], a_own if par == 1 else pltpu.roll(a_own, N, 1), m0))
    zb = []
    for pi in range(len(pairs)):
        vv = jnp.concatenate([v_p[pi], v_p[pi]], axis=0)
        lv = jnp.dot(jnp.concatenate([l_ak[2 * pi], l_ak[2 * pi + 1]], axis=0), vv,
                     preferred_element_type=jnp.float32)
        zb += [lv[:CHUNK], lv[CHUNK:]]

    for level in range(6):
        skip = (2 ** level) // LHS_ROWS * LHS_ROWS
        new_za, new_zb = [], []
        for n, (pi, par) in enumerate(probs):
            live = slice(skip, CHUNK) if pairs[pi][1] == 0 else slice(0, CHUNK - skip)
            m = za[n][live, :N].astype(BF16)
            z2 = jnp.concatenate([za[n], zb[n]], axis=1).astype(BF16)
            p = jnp.dot(m, z2, preferred_element_type=jnp.float32)
            keep_a = jnp.where(half[1], za[n], 0.0)
            pieces_a = [keep_a[:live.start], p[:, :LANES] + keep_a[live], keep_a[live.stop:]]
            pieces_b = [zb[n][:live.start], p[:, LANES:] + zb[n][live], zb[n][live.stop:]]
            new_za.append(jnp.concatenate([x for x in pieces_a if x.shape[0]], axis=0))
            new_zb.append(jnp.concatenate([x for x in pieces_b if x.shape[0]], axis=0))
        za, zb = new_za, new_zb

    n_pairs = range(len(pairs))
    s_bf = [s_p[pi].astype(BF16) for pi in n_pairs]
    wr = [jnp.concatenate([jnp.where(half[0], pltpu.roll(za[2 * pi], N, 1), za[2 * pi + 1]).astype(BF16),
                           r_p[pi]], axis=0) for pi in n_pairs]
    ws = [lax.dot_general(wr[pi], s_bf[pi], _NT, preferred_element_type=jnp.float32) for pi in n_pairs]
    u_bf = [(-(ws[pi][:CHUNK] + jnp.where(half[0], zb[2 * pi], zb[2 * pi + 1]))).astype(BF16)
            for pi in n_pairs]
    uv = [jnp.concatenate([u_bf[pi], v_p[pi]], axis=0) for pi in n_pairs]
    y_eo = [jnp.dot(jnp.concatenate([bot[2 * pi], bot[2 * pi + 1]], axis=0), uv[pi],
                    preferred_element_type=jnp.float32) for pi in n_pairs]
    upd = [lax.dot_general(uv[pi], jnp.concatenate([be_p[pi], ke_p[pi]], axis=0), _TN,
                           preferred_element_type=jnp.float32) for pi in n_pairs]
    for pi, (q, d, j) in enumerate(pairs):
        y = ws[pi][CHUNK:] + jnp.where(half[0], y_eo[pi][:CHUNK], y_eo[pi][CHUNK:])
        dir_refs[d][3][q, :, j * LANES:(j + 1) * LANES] = y.astype(BF16)
        g_end = dir_refs[d][2][q, 0:1, j * LANES:(j + 1) * LANES]
        s_ref[q, d, j] = s_p[pi] * g_end + jnp.where(diag, upd[pi], 0.0)


def _rw_scan(ops, bkt, v_bf, gend):
    _, bsz, seq, _ = ops.shape
    n_chunks = seq // CHUNK
    rev = lambda c: n_chunks - 1 - c
    return pl.pallas_call(
        _rw_scan_kernel,
        grid=(bsz // SCAN_BATCH, n_chunks),
        in_specs=[
            pl.BlockSpec((None, SCAN_BATCH, CHUNK, RW_OPS * RW_WIDTH), lambda b, c: (0, b, c, 0)),
            pl.BlockSpec((None, SCAN_BATCH, CHUNK, RW_OPS * RW_WIDTH), lambda b, c: (1, b, rev(c), 0)),
            pl.BlockSpec((None, SCAN_BATCH, None, RW_WIDTH // LANES, 2 * CHUNK, LANES),
                         lambda b, c: (0, b, c, 0, 0, 0)),
            pl.BlockSpec((None, SCAN_BATCH, None, RW_WIDTH // LANES, 2 * CHUNK, LANES),
                         lambda b, c: (1, b, rev(c), 0, 0, 0)),
            pl.BlockSpec((SCAN_BATCH, CHUNK, RW_WIDTH), lambda b, c: (b, c, 0)),
            pl.BlockSpec((SCAN_BATCH, CHUNK, RW_WIDTH), lambda b, c: (b, rev(c), 0)),
            pl.BlockSpec((None, SCAN_BATCH, None, HALO, RW_WIDTH), lambda b, c: (0, b, c, 0, 0)),
            pl.BlockSpec((None, SCAN_BATCH, None, HALO, RW_WIDTH), lambda b, c: (1, b, rev(c), 0, 0)),
        ],
        out_specs=[
            pl.BlockSpec((SCAN_BATCH, CHUNK, RW_WIDTH), lambda b, c: (b, c, 0)),
            pl.BlockSpec((SCAN_BATCH, CHUNK, RW_WIDTH), lambda b, c: (b, rev(c), 0)),
        ],
        out_shape=[jax.ShapeDtypeStruct((bsz, seq, RW_WIDTH), BF16)] * 2,
        scratch_shapes=[pltpu.VMEM((SCAN_BATCH, N_DIR, RW_WIDTH // LANES, LANES, LANES), jnp.float32)],
        compiler_params=pltpu.CompilerParams(
            dimension_semantics=("arbitrary", "arbitrary"), vmem_limit_bytes=VMEM_LIMIT),
        name="rwkv7_scan",
    )(ops, ops, bkt, bkt, v_bf, v_bf, gend, gend)


def _hg_kernel(qf_ref, ff_ref, if_ref, qb_ref, fb_ref, ib_ref, lbl_ref, tri_ref,
               of_ref, ob_ref, s_ref, cs_ref, vs_ref, acc_ref, *, layer):
    c = pl.program_id(1)

    @pl.when(c == 0)
    def _():
        s_ref[...] = jnp.zeros_like(s_ref)

    lg = lbl_ref[...]
    e = jnp.exp(lg - jnp.max(lg, axis=0, keepdims=True))
    lb = jnp.sum(e[0:layer + 1, :], axis=0, keepdims=True) / jnp.sum(e, axis=0, keepdims=True)

    heads = range(HG_HEADS)
    sls = [slice(h * HG_DK, (h + 1) * HG_DK) for h in heads]
    trow = lax.broadcasted_iota(jnp.int32, (SUB, HG_DK), 0)
    dir_refs = ((qf_ref, ff_ref, if_ref, of_ref), (qb_ref, fb_ref, ib_ref, ob_ref))
    probs = [(n, d) for n in range(SCAN_BATCH) for d in range(N_DIR)]

    def fast_path(k, out):
        n, d = probs[k]
        q_ref, f_ref, i_ref, o_ref = dir_refs[d]
        q = lambda: q_ref[n]
        v_bf = lambda: i_ref[n].astype(BF16)
        f = lb + (1.0 - lb) * _sigmoid(f_ref[n])
        kf = 1.0 - f
        yield
        b2 = _select_rows(tri_ref[d], jnp.log2(f), 2)
        yield
        (total,) = _chunk_totals(b2, d)
        q_in = (q() * jnp.exp2(b2)).astype(BF16)
        k_end = (kf * jnp.exp2(total - b2)).astype(BF16)
        g_end = jnp.exp2(total)
        st = [s_ref[n, d, h] for h in heads]
        inter = [lax.dot_general(q_in[:, sls[h]], st[h].astype(BF16), _NT, preferred_element_type=jnp.float32)
                 for h in heads]
        v_c = v_bf()
        for h in heads:
            s_ref[n, d, h] = st[h] * g_end[:, sls[h]] + lax.dot_general(
                v_c[:, sls[h]], k_end[:, sls[h]], _TN, preferred_element_type=jnp.float32)
        yield
        mid = CHUNK // 2 - 1 if d == 0 else CHUNK // 2
        dev = b2 - b2[mid:mid + 1]
        q_c = (q() * jnp.exp2(dev)).astype(BF16)
        k_c = (kf * jnp.exp2(-dev)).astype(BF16)
        scores = [lax.dot_general(q_c[:, sls[h]], k_c[:, sls[h]], _NT, preferred_element_type=jnp.float32)
                  for h in heads]
        out[k] = (inter, jnp.max(jnp.abs(dev), axis=0, keepdims=True))
        yield
        tt = lax.broadcasted_iota(jnp.int32, (CHUNK, CHUNK), 0)
        ss = lax.broadcasted_iota(jnp.int32, (CHUNK, CHUNK), 1)
        seen = (tt >= ss) if d == 0 else (tt <= ss)
        v_c = v_bf()
        for h in heads:
            o_ref[n, :, sls[h]] = (inter[h] + jnp.dot(jnp.where(seen, scores[h], 0.0).astype(BF16), v_c[:, sls[h]],
                                                       preferred_element_type=jnp.float32)).astype(BF16)

    def safe_path(k, inter):
        n, d = probs[k]
        q_ref, f_ref, i_ref, o_ref = dir_refs[d]
        q, v = q_ref[n], i_ref[n]
        v_bf = v.astype(BF16)
        f = lb + (1.0 - lb) * _sigmoid(f_ref[n])
        kf = 1.0 - f
        b2 = _select_rows(tri_ref[d], jnp.log2(f), 2)
        cs = b2 - jnp.log2(kf)
        for h in heads:
            cs_ref[h] = cs[:, sls[h]]
            vs_ref[h] = v[:, sls[h]]
        for lo in range(0, CHUNK, SUB):
            q_i, b_i = q[lo:lo + SUB], b2[lo:lo + SUB]
            acc = [inter[h][lo:lo + SUB] for h in heads]
            for s in range(SUB):
                row_s = pl.ds(lo + s, SUB, stride=0)
                m = (trow >= s) if d == 0 else (trow <= s)
                for h in heads:
                    xs = jnp.where(m, q_i[:, sls[h]] * jnp.exp2(b_i[:, sls[h]] - cs_ref[h, row_s, :]), 0.0)
                    w = jnp.sum(xs, axis=-1, keepdims=True)
                    acc[h] = acc[h] + w * vs_ref[h, row_s, :]
            for h in heads:
                acc_ref[lo:lo + SUB, sls[h]] = acc[h]

        def rows(p_lo, p_hi):
            lo = p_lo if d == 0 else CHUNK - p_hi
            return slice(lo, lo + (p_hi - p_lo))

        size = CHUNK // 2
        while size >= SUB:
            for p in range(0, CHUNK, 2 * size):
                s_sl, t_sl = rows(p, p + size), rows(p + size, p + 2 * size)
                ref = b2[rows(p + size - 1, p + size)]
                q_t = (q[t_sl] * jnp.exp2(b2[t_sl] - ref)).astype(BF16)
                k_s = (kf[s_sl] * jnp.exp2(ref - b2[s_sl])).astype(BF16)
                for h in heads:
                    sc = lax.dot_general(q_t[:, sls[h]], k_s[:, sls[h]], _NT, preferred_element_type=jnp.float32)
                    acc_ref[t_sl, sls[h]] += jnp.dot(sc.astype(BF16), v_bf[s_sl, sls[h]],
                                                     preferred_element_type=jnp.float32)
            size //= 2
        o_ref[n] = acc_ref[...].astype(BF16)

    ctxs = [None] * len(probs)
    for k0 in range(0, len(probs), HG_LOCKSTEP):
        for _ in itertools.zip_longest(*[fast_path(k, ctxs) for k in range(k0, k0 + HG_LOCKSTEP)]):
            pass
    worst = functools.reduce(jnp.maximum, [ctx[-1] for ctx in ctxs])

    @pl.when(jnp.max(worst) > SAFE_LOG2_DROP)
    def _():
        for k, ctx in enumerate(ctxs):
            safe_path(k, *ctx[:-1])


def _hg_mix(p_hg, lb_logits, layer, tri):
    bsz, seq, _ = p_hg.shape
    n_chunks = seq // CHUNK
    rev = lambda c: n_chunks - 1 - c
    blk = (SCAN_BATCH, CHUNK, HG_WIDTH)
    fwd = lambda j: pl.BlockSpec(blk, lambda g, c: (g, c, j))
    bwd = lambda j: pl.BlockSpec(blk, lambda g, c: (g, rev(c), j))
    full = lambda a: pl.BlockSpec(a.shape, lambda g, c: (0,) * a.ndim)
    return pl.pallas_call(
        functools.partial(_hg_kernel, layer=layer),
        grid=(bsz // SCAN_BATCH, n_chunks),
        in_specs=[fwd(0), fwd(1), fwd(3), bwd(0), bwd(2), bwd(3), full(lb_logits), full(tri)],
        out_specs=[fwd(0), bwd(0)],
        out_shape=[jax.ShapeDtypeStruct((bsz, seq, HG_WIDTH), BF16)] * 2,
        scratch_shapes=[pltpu.VMEM((SCAN_BATCH, N_DIR, HG_HEADS, HG_DK, HG_DK), jnp.float32),
                        pltpu.VMEM((HG_HEADS, CHUNK, HG_DK), jnp.float32),
                        pltpu.VMEM((HG_HEADS, CHUNK, HG_DK), jnp.float32),
                        pltpu.VMEM((CHUNK, HG_WIDTH), jnp.float32)],
        compiler_params=pltpu.CompilerParams(
            dimension_semantics=("arbitrary", "arbitrary"), vmem_limit_bytes=VMEM_LIMIT),
        name="hgrn2_mix",
    )(p_hg, p_hg, p_hg, p_hg, p_hg, p_hg, lb_logits, tri)


def _out_kernel(x_ref, yf_ref, yb_ref, bonus_ref, gate_ref, of_ref, ob_ref, hgate_ref,
                lnw_ref, lnb_ref, hgn_ref, seg_ref, w_ref, post_ref, out_ref):
    y = yf_ref[...].astype(jnp.float32) + yb_ref[...].astype(jnp.float32)
    seg = seg_ref[...]
    mu = _head_sums(y, seg) * (1.0 / RW_HEAD_DIM)
    yc = y - mu
    var = _head_sums(yc * yc, seg) * (1.0 / RW_HEAD_DIM)
    y = yc * lax.rsqrt(var + GN_EPS) * lnw_ref[...] + lnb_ref[...]
    rw = (y + bonus_ref[...].astype(jnp.float32)) * gate_ref[...].astype(jnp.float32)

    o = of_ref[...].astype(jnp.float32) + ob_ref[...].astype(jnp.float32)
    parts = []
    for h in range(HG_HEADS):
        oh = o[:, h * HG_DK:(h + 1) * HG_DK]
        parts.append(oh * lax.rsqrt(jnp.mean(oh * oh, axis=-1, keepdims=True) + NORM_EPS))
    hg = jnp.concatenate(parts, axis=1) * hgn_ref[...] * hgate_ref[...].astype(jnp.float32)

    cat = jnp.concatenate([rw, hg], axis=1).astype(BF16)
    yo = jnp.dot(cat, w_ref[...], preferred_element_type=jnp.float32)
    ms = jnp.mean(yo * yo, axis=-1, keepdims=True)
    out_ref[...] = x_ref[...] + yo * lax.rsqrt(ms + NORM_EPS) * post_ref[...]


def _out_proj(x2, y_f, y_b, bonus, gate, o_f, o_b, hg_gate, ln_w, ln_b, hg_norm, seg, w_bf16, post_g):
    rows = x2.shape[0]
    tile = lambda n: pl.BlockSpec((OUT_TILE, n), lambda i: (i, 0))
    row_spec = lambda n: pl.BlockSpec((1, n), lambda i: (0, 0))
    return pl.pallas_call(
        _out_kernel,
        grid=(rows // OUT_TILE,),
        in_specs=[
            tile(D_MODEL),
            tile(RW_WIDTH), tile(RW_WIDTH), tile(RW_WIDTH), tile(RW_WIDTH),
            tile(HG_WIDTH), tile(HG_WIDTH), tile(HG_WIDTH),
            row_spec(RW_WIDTH), row_spec(RW_WIDTH), row_spec(HG_WIDTH),
            pl.BlockSpec((MXU_TILE, MXU_TILE), lambda i: (0, 0)),
            pl.BlockSpec((D_MODEL, D_MODEL), lambda i: (0, 0)),
            row_spec(D_MODEL),
        ],
        out_specs=tile(D_MODEL),
        out_shape=jax.ShapeDtypeStruct((rows, D_MODEL), jnp.float32),
        compiler_params=pltpu.CompilerParams(
            dimension_semantics=("arbitrary",), vmem_limit_bytes=VMEM_LIMIT),
        name="out_proj",
    )(x2, y_f, y_b, bonus, gate, o_f, o_b, hg_gate, ln_w, ln_b, hg_norm, seg, w_bf16, post_g)


def _lora_weights(w):
    z = jnp.zeros_like(w[0])
    wp = jnp.stack([jnp.concatenate([w[0], z], axis=0), jnp.concatenate([z, w[1]], axis=0)])
    return wp.astype(BF16)


def _scan_order_selectors(tile):
    t = jnp.arange(tile)
    same = (t[:, None] // CHUNK) == (t[None, :] // CHUNK)
    fwd = same & (t[None, :] <= t[:, None])
    bwd = same & (t[None, :] >= t[:, None])
    return jnp.stack([fwd, bwd]).astype(BF16)


def kernel(x, pre_norm_g, w_in, rw_shift_prev, rw_shift_next, rw_w0, rw_w2, rw_a0, rw_a2, rw_k_k, rw_k_a,
           rw_r_k, rw_ln_w, rw_ln_b, hg_lb_logits, hg_norm_g, w_out, post_norm_g):
    bsz, seq, dm = x.shape
    depth = w_in.shape[0]
    rows = bsz * seq
    lane = jnp.arange(MXU_TILE) // RW_HEAD_DIM
    seg = (lane[:, None] == lane[None, :]).astype(BF16)
    tri = _scan_order_selectors(ROW_TILE)
    hg_tri = _scan_order_selectors(CHUNK)
    row = lambda t: t.reshape(1, -1)
    flat = lambda t: t.reshape(rows, t.shape[-1])
    for l in range(depth):
        x2 = x.reshape(rows, dm)
        p_hg, hg_gate, ops, bkt, v_bf, gend, bonus, gate = _in_rw(
            x, row(pre_norm_g[l]), w_in[l].astype(BF16),
            row(rw_shift_prev[l]), row(rw_shift_next[l]),
            rw_w0[l], _lora_weights(rw_w2[l]), rw_a0[l], _lora_weights(rw_a2[l]),
            row(rw_k_k[l]), row(rw_k_a[l]), row(rw_r_k[l]), seg, tri)
        y_f, y_b = _rw_scan(ops, bkt, v_bf, gend)
        o_f, o_b = _hg_mix(p_hg, hg_lb_logits, l, hg_tri)
        out = _out_proj(
            x2, flat(y_f), flat(y_b), flat(bonus), flat(gate), flat(o_f), flat(o_b), flat(hg_gate),
            row(rw_ln_w[l]), row(rw_ln_b[l]), row(jnp.tile(hg_norm_g[l], HG_HEADS)), seg,
            w_out[l].astype(BF16), row(post_norm_g[l]))
        x = out.reshape(bsz, seq, dm)
    return x
```
